```python
import jax
import jax.numpy as jnp
from jax import lax
import numpy as np

D_MODEL = 2048
BATCH = 4
SEQ = 2048
DEPTH = 4
DEC_BATCH = 8
DEC_SEQ = 4
PAST_LEN = 16384
PAGE_SIZE = 128

N_MIXERS = 3
LAYER_KIND = tuple(i % N_MIXERS for i in range(DEPTH))
LAYER_SLOT = tuple(i // N_MIXERS for i in range(DEPTH))
N_A = LAYER_KIND.count(0)
N_B = LAYER_KIND.count(1)
N_C = LAYER_KIND.count(2)
NORM_EPS = 1e-6

A_GROUPS = ((128, 1), (512, 4), (2048, 16))
N_A_GROUPS = len(A_GROUPS)
A_HEADS = 16
A_HEAD_DIM = D_MODEL // A_HEADS
A_WIDTH = A_HEADS * A_HEAD_DIM
A_ROT = A_HEAD_DIM // 4
A_ROPE_THETA = 500000.0
A_IN = 3 * N_A_GROUPS * A_WIDTH + A_WIDTH

B_HEADS = 8
B_DK = D_MODEL // B_HEADS
B_DV = 2 * B_DK
B_CHUNK = 128
B_ROPE_THETA = 10000.0
B_IN = 2 * B_HEADS * B_DK + 2 * B_HEADS * B_DV

C_HEADS = 4
C_DK = D_MODEL // (2 * C_HEADS)
C_DV = D_MODEL // C_HEADS
C_RANK = 16
C_TAU = 16.0
C_CHUNK = 64
C_IN = 2 * C_HEADS * C_DK + 2 * C_HEADS * C_DV + C_RANK

kernel_name = 'hybrid_dilated_retention_gla_step'

F32 = jnp.float32


def rms_norm(x, g):
    xf = x.astype(F32)
    y = xf * lax.rsqrt(jnp.mean(xf * xf, axis=-1, keepdims=True) + NORM_EPS)
    return (y * g.astype(F32)).astype(x.dtype)


def rotary(x, pos, n_rot, theta):
    half = n_rot // 2
    inv_freq = theta ** (-jnp.arange(half, dtype=F32) / half)
    ang = pos.astype(F32)[:, None] * inv_freq[None, :]
    cos = jnp.cos(ang)[None, :, None, :]
    sin = jnp.sin(ang)[None, :, None, :]
    xf = x.astype(F32)
    x1 = xf[..., :half]
    x2 = xf[..., half:n_rot]
    out = jnp.concatenate([x1 * cos - x2 * sin, x1 * sin + x2 * cos, xf[..., n_rot:]], axis=-1)
    return out.astype(x.dtype)


def softmax_with_lse(s):
    m = jnp.max(s, axis=-1, keepdims=True)
    p = jnp.exp(s - m)
    den = jnp.sum(p, axis=-1, keepdims=True)
    return p / den, (m + jnp.log(den))[..., 0]


def dilated_window_attn_prompt(q, k, v, dilation, nk):
    B, S, H, D = q.shape
    L = S // dilation
    blk = nk
    nb = -(-L // blk)
    Lp = nb * blk

    def to_sub(t):
        t = t.reshape(B, L, dilation, H, D).transpose(0, 2, 1, 3, 4)
        return jnp.pad(t, ((0, 0), (0, 0), (0, Lp - L), (0, 0), (0, 0)))

    def band(t):
        tp = jnp.pad(t, ((0, 0), (0, 0), (blk, 0), (0, 0), (0, 0))).reshape(B, dilation, nb + 1, blk, H, D)
        return jnp.concatenate([tp[:, :, :-1], tp[:, :, 1:]], axis=3)

    qb = to_sub(q).reshape(B, dilation, nb, blk, H, D)
    kb = band(to_sub(k))
    vb = band(to_sub(v))
    i = jnp.arange(blk)[:, None]
    c = jnp.arange(2 * blk)[None, :]
    local = (c >= i) & (c <= i + blk)
    started = (jnp.arange(nb)[:, None, None] > 0) | (c[None] >= blk)
    mask = local[None] & started
    s = jnp.einsum('brnqhd,brnkhd->brnhqk', qb, kb, preferred_element_type=F32) * (D ** -0.5)
    s = jnp.where(mask[None, None, :, None], s, -jnp.inf)
    p, lse = softmax_with_lse(s)
    o = jnp.einsum('brnhqk,brnkhd->brnqhd', p, vb.astype(F32))
    o = o.reshape(B, dilation, Lp, H, D)[:, :, :L].transpose(0, 2, 1, 3, 4).reshape(B, S, H, D)
    lse = lse.transpose(0, 1, 2, 4, 3).reshape(B, dilation, Lp, H)[:, :, :L]
    lse = lse.transpose(0, 2, 1, 3).reshape(B, S, H)
    return o, lse


def dilated_window_attn_sample(q, k_all, v_all, lbuf, dilation, nk):
    T = q.shape[1]
    D = q.shape[-1]
    idx = lbuf + jnp.arange(T)[:, None] - dilation * jnp.arange(nk + 1)[None, :]
    valid = idx >= 0
    idx = jnp.maximum(idx, 0)
    kg = jnp.take(k_all, idx, axis=1)
    vg = jnp.take(v_all, idx, axis=1)
    s = jnp.einsum('bthd,btkhd->bthk', q, kg, preferred_element_type=F32) * (D ** -0.5)
    s = jnp.where(valid[None, :, None, :], s, -jnp.inf)
    p, lse = softmax_with_lse(s)
    o = jnp.einsum('bthk,btkhd->bthd', p, vg.astype(F32))
    return o, lse


def a_project(h, pos, w_in):
    B, S, _ = h.shape
    z = h @ w_in
    n_qkv = 3 * N_A_GROUPS * A_WIDTH
    qkv = z[..., :n_qkv].reshape(B, S, N_A_GROUPS, 3, A_HEADS, A_HEAD_DIM)
    gate = z[..., n_qkv:]
    qs = [rotary(qkv[:, :, g, 0], pos, A_ROT, A_ROPE_THETA) for g in range(N_A_GROUPS)]
    ks = [rotary(qkv[:, :, g, 1], pos, A_ROT, A_ROPE_THETA) for g in range(N_A_GROUPS)]
    vs = [qkv[:, :, g, 2] for g in range(N_A_GROUPS)]
    return qs, ks, vs, gate


def a_merge(outs, lses, gate, w_out):
    wts = jax.nn.softmax(jnp.stack(lses), axis=0)
    o = jnp.einsum('gbsh,gbshd->bshd', wts, jnp.stack(outs))
    B, S = gate.shape[:2]
    o = o.reshape(B, S, A_WIDTH).astype(gate.dtype) * jax.nn.silu(gate)
    return o @ w_out


def attn_layer_prompt(h, pos, w_in, w_out):
    S = h.shape[1]
    qs, ks, vs, gate = a_project(h, pos, w_in)
    outs, lses, kv_rows = [], [], []
    for g, (win, dil) in enumerate(A_GROUPS):
        o, l = dilated_window_attn_prompt(qs[g], ks[g], vs[g], dil, win // dil)
        outs.append(o)
        lses.append(l)
        keep = min(win, S)
        kv_rows.append(jnp.stack([ks[g][:, S - keep:], vs[g][:, S - keep:]], axis=2))
    return a_merge(outs, lses, gate, w_out), kv_rows


def attn_layer_sample(h, pos, caches, w_in, w_out):
    qs, ks, vs, gate = a_project(h, pos, w_in)
    outs, lses, kv_rows = [], [], []
    for g, (win, dil) in enumerate(A_GROUPS):
        cache = caches[g].astype(ks[g].dtype)
        lbuf = cache.shape[1]
        k_all = jnp.concatenate([cache[:, :, 0], ks[g]], axis=1)
        v_all = jnp.concatenate([cache[:, :, 1], vs[g]], axis=1)
        o, l = dilated_window_attn_sample(qs[g], k_all, v_all, lbuf, dil, win // dil)
        outs.append(o)
        lses.append(l)
        kv_rows.append(jnp.stack([ks[g], vs[g]], axis=2))
    return a_merge(outs, lses, gate, w_out), kv_rows


def chunk_scan(chunk_fn, state0, seqs, chunk):
    B, S = seqs[0].shape[:2]
    nc = S // chunk

    def split(t):
        return t.reshape((B, nc, chunk) + t.shape[2:]).swapaxes(0, 1)

    def step(state, inp):
        out, state = chunk_fn(*inp, state)
        return state, out

    state, out = lax.scan(step, state0, tuple(split(t) for t in seqs))
    out = out.swapaxes(0, 1).reshape((B, S) + out.shape[3:])
    return out, state


def retention_log_decay():
    return jnp.log(1.0 - 2.0 ** (-5.0 - jnp.arange(B_HEADS, dtype=F32)))


def retention_chunk(q, k, v, state):
    q, k, v = q.astype(F32), k.astype(F32), v.astype(F32)
    C = q.shape[1]
    lg = retention_log_decay()
    idx = jnp.arange(C, dtype=F32)
    diff = idx[:, None] - idx[None, :]
    decay = jnp.where(diff >= 0, jnp.exp(lg[:, None, None] * jnp.maximum(diff, 0.0)), 0.0)
    scores = jnp.einsum('bqhd,bkhd->bhqk', q, k) * decay[None]
    inner = jnp.einsum('bhqk,bkhv->bqhv', scores, v)
    cross = jnp.einsum('bqhd,bhdv->bqhv', q, state) * jnp.exp(lg[None, :] * (idx[:, None] + 1.0))[None, :, :, None]
    k_dec = k * jnp.exp(lg[None, :] * (C - 1.0 - idx[:, None]))[None, :, :, None]
    new_state = jnp.exp(lg * C)[None, :, None, None] * state + jnp.einsum('bkhd,bkhv->bhdv', k_dec, v)
    return inner + cross, new_state


def retention_layer(h, pos, state0, chunk, w_in, gn, w_out):
    B, S, _ = h.shape
    z = h @ w_in
    nk = B_HEADS * B_DK
    nv = B_HEADS * B_DV
    q = rotary(z[..., :nk].reshape(B, S, B_HEADS, B_DK), pos, B_DK, B_ROPE_THETA)
    k = rotary(z[..., nk:2 * nk].reshape(B, S, B_HEADS, B_DK), pos, B_DK, B_ROPE_THETA) * (B_DK ** -0.5)
    v = z[..., 2 * nk:2 * nk + nv].reshape(B, S, B_HEADS, B_DV)
    gate = z[..., 2 * nk + nv:]
    o, state = chunk_scan(retention_chunk, state0, (q, k, v), chunk)
    mu = jnp.mean(o, axis=-1, keepdims=True)
    var = jnp.mean(jnp.square(o - mu), axis=-1, keepdims=True)
    y = (o - mu) * lax.rsqrt(var + NORM_EPS) * gn.astype(F32)
    y = y.reshape(B, S, nv).astype(h.dtype) * jax.nn.silu(gate)
    return y @ w_out, state


def gla_chunk(q, k, v, log_a, state):
    q, k, v = q.astype(F32), k.astype(F32), v.astype(F32)
    C = q.shape[1]
    b = jnp.cumsum(log_a, axis=1)
    idx = jnp.arange(C)
    tri = (idx[:, None] >= idx[None, :])[None, :, :, None, None]
    diff = b[:, :, None] - b[:, None, :]
    w = jnp.where(tri, jnp.exp(jnp.where(tri, diff, 0.0)), 0.0)
    att = jnp.einsum('btshd,bshd->bhts', q[:, :, None] * w, k)
    inner = jnp.einsum('bhts,bshv->bthv', att, v)
    cross = jnp.einsum('bthd,bhdv->bthv', q * jnp.exp(b), state)
    b_last = b[:, -1]
    new_state = jnp.exp(b_last)[..., None] * state + jnp.einsum('bshd,bshv->bhdv', k * jnp.exp(b_last[:, None] - b), v)
    return inner + cross, new_state


def gla_layer(h, state0, chunk, w_in, w_gate2, b_gate, gn, w_out):
    B, S, _ = h.shape
    z = h @ w_in
    nk = C_HEADS * C_DK
    nv = C_HEADS * C_DV
    q = z[..., :nk].reshape(B, S, C_HEADS, C_DK) * (C_DK ** -0.5)
    k = z[..., nk:2 * nk].reshape(B, S, C_HEADS, C_DK)
    v = z[..., 2 * nk:2 * nk + nv].reshape(B, S, C_HEADS, C_DV)
    gate = z[..., 2 * nk + nv:2 * nk + 2 * nv]
    low_rank = z[..., 2 * nk + 2 * nv:]
    log_a = jax.nn.log_sigmoid((low_rank @ w_gate2 + b_gate).astype(F32)) / C_TAU
    log_a = log_a.reshape(B, S, C_HEADS, C_DK)
    o, state = chunk_scan(gla_chunk, state0, (q, k, v, log_a), chunk)
    y = o * lax.rsqrt(jnp.mean(o * o, axis=-1, keepdims=True) + NORM_EPS) * gn.astype(F32)
    y = y.reshape(B, S, nv).astype(h.dtype) * jax.nn.silu(gate)
    return y @ w_out, state


def setup_inputs(seed: int = 0) -> dict:
    key = jax.random.key(seed)
    ks = jax.random.split(key, 20)

    def nrm(k, shape, scale):
        return jax.random.normal(k, shape, F32) * scale

    inp = {}
    inp['x_prompt'] = nrm(ks[0], (BATCH, SEQ, D_MODEL), 1.0)
    inp['x_sample'] = nrm(ks[1], (DEC_BATCH, DEC_SEQ, D_MODEL), 1.0)
    inp['cache_a_kv1'] = nrm(ks[2], (N_A, DEC_BATCH, min(A_GROUPS[0][0], PAST_LEN), 2, A_HEADS, A_HEAD_DIM), 1.0)
    inp['cache_a_kv2'] = nrm(ks[3], (N_A, DEC_BATCH, min(A_GROUPS[1][0], PAST_LEN), 2, A_HEADS, A_HEAD_DIM), 1.0)
    inp['cache_a_kv3'] = nrm(ks[4], (N_A, DEC_BATCH, min(A_GROUPS[2][0], PAST_LEN), 2, A_HEADS, A_HEAD_DIM), 1.0)
    inp['state_b'] = nrm(ks[5], (N_B, DEC_BATCH, B_HEADS, B_DK, B_DV), 0.1)
    inp['state_c'] = nrm(ks[6], (N_C, DEC_BATCH, C_HEADS, C_DK, C_DV), 0.1)
    inp['norm_g'] = 1.0 + nrm(ks[7], (DEPTH, D_MODEL), 0.1)
    inp['final_g'] = 1.0 + nrm(ks[8], (D_MODEL,), 0.1)
    inp['w_in_a'] = nrm(ks[9], (N_A, D_MODEL, A_IN), D_MODEL ** -0.5)
    inp['w_out_a'] = nrm(ks[10], (N_A, A_WIDTH, D_MODEL), A_WIDTH ** -0.5)
    inp['w_in_b'] = nrm(ks[11], (N_B, D_MODEL, B_IN), D_MODEL ** -0.5)
    inp['gn_b'] = 1.0 + nrm(ks[12], (N_B, B_HEADS, B_DV), 0.1)
    inp['w_out_b'] = nrm(ks[13], (N_B, B_HEADS * B_DV, D_MODEL), (B_HEADS * B_DV) ** -0.5)
    inp['w_in_c'] = nrm(ks[14], (N_C, D_MODEL, C_IN), D_MODEL ** -0.5)
    inp['w_gate2_c'] = nrm(ks[15], (N_C, C_RANK, C_HEADS * C_DK), C_RANK ** -0.5)
    inp['b_gate_c'] = nrm(ks[16], (N_C, C_HEADS * C_DK), 0.1)
    inp['gn_c'] = 1.0 + nrm(ks[17], (N_C, C_HEADS, C_DV), 0.1)
    inp['w_out_c'] = nrm(ks[18], (N_C, C_HEADS * C_DV, D_MODEL), (C_HEADS * C_DV) ** -0.5)
    return inp


def reference(x_prompt, x_sample, cache_a_kv1, cache_a_kv2, cache_a_kv3, state_b, state_c,
              norm_g, final_g, w_in_a, w_out_a, w_in_b, gn_b, w_out_b,
              w_in_c, w_gate2_c, b_gate_c, gn_c, w_out_c):
    Bp, S, _ = x_prompt.shape
    T = x_sample.shape[1]
    pos_p = jnp.arange(S)
    pos_s = PAST_LEN + jnp.arange(T)
    caches_a = (cache_a_kv1, cache_a_kv2, cache_a_kv3)
    xp, xs = x_prompt, x_sample
    a_rows_p, a_rows_s = [], []
    b_states_p, b_states_s, c_states_p, c_states_s = [], [], [], []
    for i in range(DEPTH):
        kind, slot = LAYER_KIND[i], LAYER_SLOT[i]
        hp = rms_norm(xp, norm_g[i])
        hs = rms_norm(xs, norm_g[i])
        if kind == 0:
            dp, rows_p = attn_layer_prompt(hp, pos_p, w_in_a[slot], w_out_a[slot])
            ds, rows_s = attn_layer_sample(hs, pos_s, [c[slot] for c in caches_a], w_in_a[slot], w_out_a[slot])
            a_rows_p.append(rows_p)
            a_rows_s.append(rows_s)
        elif kind == 1:
            zero_b = jnp.zeros((Bp, B_HEADS, B_DK, B_DV), F32)
            dp, st_p = retention_layer(hp, pos_p, zero_b, min(B_CHUNK, S), w_in_b[slot], gn_b[slot], w_out_b[slot])
            ds, st_s = retention_layer(hs, pos_s, state_b[slot].astype(F32), T, w_in_b[slot], gn_b[slot], w_out_b[slot])
            b_states_p.append(st_p.astype(state_b.dtype))
            b_states_s.append(st_s.astype(state_b.dtype))
        else:
            zero_c = jnp.zeros((Bp, C_HEADS, C_DK, C_DV), F32)
            dp, st_p = gla_layer(hp, zero_c, min(C_CHUNK, S), w_in_c[slot], w_gate2_c[slot], b_gate_c[slot], gn_c[slot], w_out_c[slot])
            ds, st_s = gla_layer(hs, state_c[slot].astype(F32), T, w_in_c[slot], w_gate2_c[slot], b_gate_c[slot], gn_c[slot], w_out_c[slot])
            c_states_p.append(st_p.astype(state_c.dtype))
            c_states_s.append(st_s.astype(state_c.dtype))
        xp = xp + dp
        xs = xs + ds
    y_prompt = rms_norm(xp, final_g)
    y_sample = rms_norm(xs, final_g)
    prompt_a_kv1 = jnp.stack([r[0] for r in a_rows_p])
    prompt_a_kv2 = jnp.stack([r[1] for r in a_rows_p])
    prompt_a_kv3 = jnp.stack([r[2] for r in a_rows_p])
    sample_a_kv1 = jnp.stack([r[0] for r in a_rows_s])
    sample_a_kv2 = jnp.stack([r[1] for r in a_rows_s])
    sample_a_kv3 = jnp.stack([r[2] for r in a_rows_s])
    prompt_state_b = jnp.stack(b_states_p)
    sample_state_b = jnp.stack(b_states_s)
    prompt_state_c = jnp.stack(c_states_p)
    sample_state_c = jnp.stack(c_states_s)
    return (y_prompt, y_sample, prompt_a_kv1, prompt_a_kv2, prompt_a_kv3, prompt_state_b, prompt_state_c,
            sample_a_kv1, sample_a_kv2, sample_a_kv3, sample_state_b, sample_state_c)
```

```python
import functools

import jax
import jax.numpy as jnp
from jax import lax
from jax.experimental import pallas as pl
from jax.experimental.pallas import tpu as pltpu

F32 = jnp.float32
BF16 = jnp.bfloat16

PAST_LEN = 16384
NORM_EPS = 1e-6
A_GROUPS = ((128, 1), (512, 4), (2048, 16))
A_HEADS = 16
A_HEAD_DIM = 128
A_ROT = A_HEAD_DIM // 4
A_ROPE_THETA = 500000.0
A_BAND = 128
B_HEADS = 8
B_DK = 256
B_DV = 512
B_CHUNK = 128
B_ROPE_THETA = 10000.0
C_HEADS = 4
C_DK = 256
C_DV = 512
C_RANK = 16
C_TAU = 16.0
C_CHUNK = 64
C_SUB = 16
LAYER_KINDS = (0, 1, 2, 0)
LAYER_SLOTS = (0, 0, 0, 1)

SAMPLE_PAD = 16
LANES = 128
VMEM_LIMIT = 56 * 1024 * 1024


def _params(sem, vmem=VMEM_LIMIT):
    return pltpu.CompilerParams(dimension_semantics=sem, vmem_limit_bytes=vmem)


def _silu(g):
    return g * (1.0 / (1.0 + jnp.exp(-g)))


def _nt(a, b):
    return lax.dot_general(a, b, (((1,), (1,)), ((), ())), preferred_element_type=F32)


def _nn(a, b):
    return jnp.dot(a, b, preferred_element_type=F32)


def _split(x):
    hi = x.astype(BF16)
    lo = (x - hi.astype(F32)).astype(BF16)
    return hi, lo


def _split3(x):
    hi = x.astype(BF16)
    r = x - hi.astype(F32)
    mid = r.astype(BF16)
    lo = (r - mid.astype(F32)).astype(BF16)
    return hi, mid, lo


def _nn_exact_lhs(sel, x):
    hi, mid, lo = _split3(x)
    return _nn(sel, hi) + _nn(sel, mid) + _nn(sel, lo)


def _nt_exact_lhs(sel, x):
    hi, mid, lo = _split3(x)
    return _nt(sel, hi) + _nt(sel, mid) + _nt(sel, lo)


def _nn_hi(a, b):
    ah, al = _split(a)
    bh, bl = _split(b)
    return _nn(ah, bh) + _nn(ah, bl) + _nn(al, bh)


def _eye(n, dtype):
    r = lax.broadcasted_iota(jnp.int32, (n, n), 0)
    c = lax.broadcasted_iota(jnp.int32, (n, n), 1)
    return (r == c).astype(dtype)


def _rmsnorm_body(x_ref, g_ref, o_ref):
    x = x_ref[...]
    ms = jnp.mean(x * x, axis=-1, keepdims=True)
    o_ref[...] = (x * lax.rsqrt(ms + NORM_EPS) * g_ref[...]).astype(o_ref.dtype)


def _rmsnorm(x, g, out_dtype):
    m, d = x.shape
    tm = min(m, 512)
    return pl.pallas_call(
        _rmsnorm_body,
        grid=(m // tm,),
        in_specs=[pl.BlockSpec((tm, d), lambda i: (i, 0)),
                  pl.BlockSpec((1, d), lambda i: (0, 0))],
        out_specs=pl.BlockSpec((tm, d), lambda i: (i, 0)),
        out_shape=jax.ShapeDtypeStruct((m, d), out_dtype),
        compiler_params=_params(("arbitrary",)),
        name="rmsnorm",
    )(x, g.reshape(1, d))


def _proj_body(h_ref, w_ref, *refs, kind, tn):
    if kind == "a":
        c_ref, s1_ref, s2_ref, z_ref, wb_ref = refs
    elif kind == "b":
        c_ref, s_ref, z_ref, wb_ref = refs
    else:
        z_ref, wb_ref = refs
    j = pl.program_id(0)
    i = pl.program_id(1)

    @pl.when(i == 0)
    def _():
        wb_ref[...] = w_ref[...].astype(BF16)

    z_ref[...] = _nn(h_ref[...], wb_ref[...])

    if kind == "a":
        seg = (j * tn) // (A_HEADS * A_HEAD_DIM)
        is_rot = jnp.logical_and(seg < 3 * len(A_GROUPS), seg % 3 < 2)

        @pl.when(is_rot)
        def _():
            c, s1, s2 = c_ref[...], s1_ref[...], s2_ref[...]
            half = A_ROT // 2
            for hh in range(tn // A_HEAD_DIM):
                sl = slice(hh * A_HEAD_DIM, (hh + 1) * A_HEAD_DIM)
                x = z_ref[:, sl]
                x_hi = pltpu.roll(x, A_HEAD_DIM - half, 1)
                x_lo = pltpu.roll(x, half, 1)
                z_ref[:, sl] = x * c + x_hi * s1 + x_lo * s2
    elif kind == "b":
        n_qk = B_HEADS * B_DK
        is_rot = (j * tn) < 2 * n_qk
        is_k = (j * tn) >= n_qk

        @pl.when(is_rot)
        def _():
            cos, sin = c_ref[...], s_ref[...]
            scale = jnp.where(is_k, B_DK ** -0.5, 1.0).astype(F32)
            half = B_DK // 2
            for hh in range(tn // B_DK):
                o = hh * B_DK
                x1 = z_ref[:, o:o + half]
                x2 = z_ref[:, o + half:o + B_DK]
                z_ref[:, o:o + half] = (x1 * cos - x2 * sin) * scale
                z_ref[:, o + half:o + B_DK] = (x1 * sin + x2 * cos) * scale


def _proj(h, w3, slot, n_out, kind, tabs, tab_blocks, tm, tn):
    m, k = h.shape
    in_specs = [pl.BlockSpec((tm, k), lambda j, i: (i, 0)),
                pl.BlockSpec((None, k, tn), lambda j, i: (slot, 0, j))]
    for _ in tabs:
        in_specs.append(pl.BlockSpec((tm, LANES), lambda j, i: (i % tab_blocks, 0)))
    return pl.pallas_call(
        functools.partial(_proj_body, kind=kind, tn=tn),
        grid=(n_out // tn, m // tm),
        in_specs=in_specs,
        out_specs=pl.BlockSpec((tm, tn), lambda j, i: (i, j)),
        out_shape=jax.ShapeDtypeStruct((m, n_out), F32),
        scratch_shapes=[pltpu.VMEM((k, tn), BF16)],
        compiler_params=_params(("arbitrary", "arbitrary")),
        name="proj_" + kind,
    )(h, w3, *tabs)


def _outproj_body(y_ref, w_ref, x_ref, g_ref, xo_ref, ho_ref, acc_ref, *, nk):
    kk = pl.program_id(1)

    @pl.when(kk == 0)
    def _():
        acc_ref[...] = jnp.zeros_like(acc_ref)

    acc_ref[...] += _nn(y_ref[...], w_ref[...].astype(BF16))

    @pl.when(kk == nk - 1)
    def _():
        xn = x_ref[...] + acc_ref[...]
        xo_ref[...] = xn
        ms = jnp.mean(xn * xn, axis=-1, keepdims=True)
        ho_ref[...] = (xn * lax.rsqrt(ms + NORM_EPS) * g_ref[...]).astype(ho_ref.dtype)


def _outproj(y, w3, slot, x, g, h_dtype, tm, tk):
    m, kd = y.shape
    d = x.shape[1]
    nk = kd // tk
    return pl.pallas_call(
        functools.partial(_outproj_body, nk=nk),
        grid=(m // tm, nk),
        in_specs=[pl.BlockSpec((tm, tk), lambda i, kk: (i, kk)),
                  pl.BlockSpec((None, tk, d), lambda i, kk: (slot, kk, 0)),
                  pl.BlockSpec((tm, d), lambda i, kk: (i, 0)),
                  pl.BlockSpec((1, d), lambda i, kk: (0, 0))],
        out_specs=[pl.BlockSpec((tm, d), lambda i, kk: (i, 0)),
                   pl.BlockSpec((tm, d), lambda i, kk: (i, 0))],
        out_shape=[jax.ShapeDtypeStruct((m, d), F32),
                   jax.ShapeDtypeStruct((m, d), h_dtype)],
        scratch_shapes=[pltpu.VMEM((tm, d), F32)],
        compiler_params=_params(("arbitrary", "arbitrary")),
        name="outproj",
    )(y, w3, x, g.reshape(1, d))


def _band_attention(q, k, v, mask):
    s = jnp.where(mask, _nt(q, k) * (A_HEAD_DIM ** -0.5), -jnp.inf)
    mx = jnp.max(s, axis=-1, keepdims=True)
    p = jnp.exp(s - mx)
    den = jnp.sum(p, axis=-1, keepdims=True)
    return _nn(p.astype(BF16), v) / den, mx + jnp.log(den)


def _rows(start, size, stride):
    return pl.ds(start, size) if stride == 1 else pl.ds(start, size, stride=stride)


def _attn_prompt_body(*refs, seq):
    qkv_refs, gate_ref, y_ref, o_s, l_s = refs[:9], refs[9], refs[10], refs[11], refs[12]
    row1 = lax.broadcasted_iota(jnp.int32, (A_BAND, A_BAND), 0)
    col1 = lax.broadcasted_iota(jnp.int32, (A_BAND, A_BAND), 1)
    mask_first = col1 <= row1
    row2 = lax.broadcasted_iota(jnp.int32, (A_BAND, 2 * A_BAND), 0)
    col2 = lax.broadcasted_iota(jnp.int32, (A_BAND, 2 * A_BAND), 1)
    mask_band = jnp.logical_and(col2 >= row2, col2 <= row2 + A_BAND)
    for g, (_, dil) in enumerate(A_GROUPS):
        q_ref, k_ref, v_ref = qkv_refs[3 * g:3 * g + 3]
        for r in range(dil):
            for n in range(seq // dil // A_BAND):
                qs = _rows(r + dil * A_BAND * n, A_BAND, dil)
                if n == 0:
                    ks, mask = qs, mask_first
                else:
                    ks, mask = _rows(r + dil * A_BAND * (n - 1), 2 * A_BAND, dil), mask_band
                o, lse = _band_attention(q_ref[qs, :].astype(BF16), k_ref[ks, :].astype(BF16),
                                         v_ref[ks, :].astype(BF16), mask)
                o_s[g, qs, :] = o
                l_s[g, qs, :] = lse
    l1, l2, l3 = l_s[0], l_s[1], l_s[2]
    mx = jnp.maximum(jnp.maximum(l1, l2), l3)
    e1, e2, e3 = jnp.exp(l1 - mx), jnp.exp(l2 - mx), jnp.exp(l3 - mx)
    tot = e1 + e2 + e3
    o = (e1 / tot) * o_s[0] + (e2 / tot) * o_s[1] + (e3 / tot) * o_s[2]
    y_ref[...] = (o * _silu(gate_ref[...])).astype(y_ref.dtype)


def _attn_prompt(z, batch, seq):
    n_groups = len(A_GROUPS)
    in_specs = []
    for c in range(3 * n_groups + 1):
        in_specs.append(pl.BlockSpec((seq, A_HEAD_DIM), lambda b, h, c=c: (b, c * A_HEADS + h)))
    return pl.pallas_call(
        functools.partial(_attn_prompt_body, seq=seq),
        grid=(batch, A_HEADS),
        in_specs=in_specs,
        out_specs=pl.BlockSpec((seq, A_HEAD_DIM), lambda b, h: (b, h)),
        out_shape=jax.ShapeDtypeStruct((batch * seq, A_HEADS * A_HEAD_DIM), BF16),
        scratch_shapes=[pltpu.VMEM((n_groups, seq, A_HEAD_DIM), F32),
                        pltpu.VMEM((n_groups, seq, 1), F32)],
        compiler_params=_params(("arbitrary", "arbitrary")),
        name="attn_prompt",
    )(*([z] * (3 * n_groups + 1)))


def _attn_sample_body(z_ref, c1_ref, c2_ref, c3_ref, y_ref):
    t = pl.program_id(1)
    scale = A_HEAD_DIM ** -0.5
    m_idx = lax.broadcasted_iota(jnp.int32, (A_BAND, A_HEADS, 1), 0)
    n_idx = lax.broadcasted_iota(jnp.int32, (SAMPLE_PAD, A_HEADS, 1), 0)
    caches = (c1_ref, c2_ref, c3_ref)
    outs, lses = [], []
    for g, (_, dil) in enumerate(A_GROUPS):
        base = 3 * g * A_HEADS
        q = z_ref[0, t, base:base + A_HEADS, :]
        kc = caches[g][0, :, 0:A_HEADS, :]
        vc = caches[g][0, :, A_HEADS:2 * A_HEADS, :]
        kn = z_ref[0, :, base + A_HEADS:base + 2 * A_HEADS, :]
        vn = z_ref[0, :, base + 2 * A_HEADS:base + 3 * A_HEADS, :]
        s_c = jnp.sum(kc * q[None], axis=-1, keepdims=True) * scale
        s_n = jnp.sum(kn * q[None], axis=-1, keepdims=True) * scale
        if dil == 1:
            s_c = jnp.where(m_idx >= t, s_c, -jnp.inf)
            s_n = jnp.where(n_idx <= t, s_n, -jnp.inf)
        else:
            s_n = jnp.where(n_idx == t, s_n, -jnp.inf)
        mx = jnp.maximum(jnp.max(s_c, axis=0), jnp.max(s_n, axis=0))
        p_c = jnp.exp(s_c - mx[None])
        p_n = jnp.exp(s_n - mx[None])
        den = jnp.sum(p_c, axis=0) + jnp.sum(p_n, axis=0)
        num = jnp.sum(p_c * vc, axis=0) + jnp.sum(p_n * vn, axis=0)
        outs.append(num / den)
        lses.append(mx + jnp.log(den))
    lmax = jnp.maximum(jnp.maximum(lses[0], lses[1]), lses[2])
    es = [jnp.exp(l - lmax) for l in lses]
    tot = es[0] + es[1] + es[2]
    o = (es[0] / tot) * outs[0] + (es[1] / tot) * outs[1] + (es[2] / tot) * outs[2]
    gate = z_ref[0, t, 3 * len(A_GROUPS) * A_HEADS:(3 * len(A_GROUPS) + 1) * A_HEADS, :]
    y_ref[0, 0] = o * _silu(gate)


def _attn_sample(zs, caches, slot, batch, t_real):
    n_in = zs.shape[1]
    z5 = zs.reshape(batch, SAMPLE_PAD, n_in // A_HEAD_DIM, A_HEAD_DIM)
    in_specs = [pl.BlockSpec((1, SAMPLE_PAD, n_in // A_HEAD_DIM, A_HEAD_DIM), lambda b, t: (b, 0, 0, 0))]
    args = [z5]
    for g, (win, dil) in enumerate(A_GROUPS):
        n_a = caches[g].shape[0]
        cv = caches[g].reshape(n_a * batch, win // dil, dil * 2 * A_HEADS, A_HEAD_DIM)
        if dil == 1:
            idx = lambda b, t: (slot * batch + b, 0, 0, 0)
        else:
            idx = lambda b, t: (slot * batch + b, 0, t, 0)
        in_specs.append(pl.BlockSpec((1, A_BAND, 2 * A_HEADS, A_HEAD_DIM), idx))
        args.append(cv)
    y = pl.pallas_call(
        _attn_sample_body,
        grid=(batch, t_real),
        in_specs=in_specs,
        out_specs=pl.BlockSpec((1, 1, A_HEADS, A_HEAD_DIM), lambda b, t: (b, t, 0, 0)),
        out_shape=jax.ShapeDtypeStruct((batch, t_real, A_HEADS, A_HEAD_DIM), F32),
        compiler_params=_params(("arbitrary", "arbitrary")),
        name="attn_sample",
    )(*args)
    return y.reshape(batch, t_real, A_HEADS * A_HEAD_DIM)


def _retention_body(*refs, has_state, nc):
    if has_state:
        q_ref, k_ref, v_ref, gate_ref, dec_ref, xi_ref, kd_ref, sd_ref, gn_ref, s0_ref, y_ref, so_ref, st_ref = refs
    else:
        q_ref, k_ref, v_ref, gate_ref, dec_ref, xi_ref, kd_ref, sd_ref, gn_ref, y_ref, so_ref, st_ref = refs
    c = pl.program_id(2)

    @pl.when(c == 0)
    def _():
        if has_state:
            st_ref[...] = s0_ref[0, 0]
        else:
            st_ref[...] = jnp.zeros_like(st_ref)

    qb = q_ref[...].astype(BF16)
    kf = k_ref[...]
    vb = v_ref[...].astype(BF16)
    state = st_ref[...]
    scores = _nt(qb, kf.astype(BF16)) * dec_ref[0]
    o = _nn(scores.astype(BF16), vb) + _nn(qb, state.astype(BF16)) * xi_ref[0]
    k_dec = (kf * kd_ref[0]).astype(BF16)
    k_dec_t = _nt(_eye(B_DK, BF16), k_dec).astype(BF16)
    st_ref[...] = sd_ref[0] * state + _nn(k_dec_t, vb)
    mu = jnp.mean(o, axis=-1, keepdims=True)
    var = jnp.mean(jnp.square(o - mu), axis=-1, keepdims=True)
    yv = (o - mu) * lax.rsqrt(var + NORM_EPS) * gn_ref[0]
    y_ref[...] = (yv * _silu(gate_ref[...])).astype(y_ref.dtype)

    @pl.when(c == nc - 1)
    def _():
        so_ref[0, 0] = st_ref[...]


def _retention(z, batch, chunk, t_real, gn, state0):
    rows = z.shape[0]
    nc = rows // (batch * chunk)
    lg = jnp.log(1.0 - 2.0 ** (-5.0 - jnp.arange(B_HEADS, dtype=F32)))[:, None, None]
    idx = jnp.arange(chunk, dtype=F32)
    diff = idx[:, None] - idx[None, :]
    dec = jnp.where(diff >= 0, jnp.exp(lg * jnp.maximum(diff, 0.0)), 0.0)
    xi = jnp.exp(lg * (idx[None, :, None] + 1.0))
    live = idx[None, :, None] < t_real
    kd = jnp.where(live, jnp.exp(lg * jnp.where(live, t_real - 1.0 - idx[None, :, None], 0.0)), 0.0)
    sd = jnp.exp(lg * t_real)
    in_specs = [pl.BlockSpec((chunk, B_DK), lambda b, h, c: (b * nc + c, h)),
                pl.BlockSpec((chunk, B_DK), lambda b, h, c: (b * nc + c, B_HEADS + h)),
                pl.BlockSpec((chunk, B_DV), lambda b, h, c: (b * nc + c, B_HEADS + h)),
                pl.BlockSpec((chunk, B_DV), lambda b, h, c: (b * nc + c, 2 * B_HEADS + h)),
                pl.BlockSpec((1, chunk, chunk), lambda b, h, c: (h, 0, 0)),
                pl.BlockSpec((1, chunk, 1), lambda b, h, c: (h, 0, 0)),
                pl.BlockSpec((1, chunk, 1), lambda b, h, c: (h, 0, 0)),
                pl.BlockSpec((1, 1, 1), lambda b, h, c: (h, 0, 0)),
                pl.BlockSpec((1, 1, B_DV), lambda b, h, c: (h, 0, 0))]
    args = [z, z, z, z, dec, xi, kd, sd, gn.reshape(B_HEADS, 1, B_DV)]
    if state0 is not None:
        in_specs.append(pl.BlockSpec((1, 1, B_DK, B_DV), lambda b, h, c: (b, h, 0, 0)))
        args.append(state0)
    return pl.pallas_call(
        functools.partial(_retention_body, has_state=state0 is not None, nc=nc),
        grid=(batch, B_HEADS, nc),
        in_specs=in_specs,
        out_specs=[pl.BlockSpec((chunk, B_DV), lambda b, h, c: (b * nc + c, h)),
                   pl.BlockSpec((1, 1, B_DK, B_DV), lambda b, h, c: (b, h, 0, 0))],
        out_shape=[jax.ShapeDtypeStruct((rows, B_HEADS * B_DV), BF16),
                   jax.ShapeDtypeStruct((batch, B_HEADS, B_DK, B_DV), F32)],
        scratch_shapes=[pltpu.VMEM((B_DK, B_DV), F32)],
        compiler_params=_params(("arbitrary", "arbitrary", "arbitrary")),
        name="retention",
    )(*args)


def _log_sigmoid(x):
    return jnp.minimum(x, 0.0) - jnp.log(1.0 + jnp.exp(-jnp.abs(x)))


def _gla_body(*refs, has_state, nc, chunk, t_sub):
    if has_state:
        q_ref, k_ref, v_ref, gate_ref, lr_ref, wg_ref, bg_ref, gn_ref, s0_ref, y_ref, so_ref, st_ref = refs
    else:
        q_ref, k_ref, v_ref, gate_ref, lr_ref, wg_ref, bg_ref, gn_ref, y_ref, so_ref, st_ref = refs
    c = pl.program_id(2)

    @pl.when(c == 0)
    def _():
        if has_state:
            st_ref[...] = s0_ref[0, 0]
        else:
            st_ref[...] = jnp.zeros_like(st_ref)

    pre = _nn_hi(lr_ref[:, 0:C_RANK], wg_ref[...]) + bg_ref[...]
    log_a = _log_sigmoid(pre) * (1.0 / C_TAU)
    ri = lax.broadcasted_iota(jnp.int32, (chunk, chunk), 0)
    ci = lax.broadcasted_iota(jnp.int32, (chunk, chunk), 1)
    tri = jnp.logical_and(ri // C_SUB == ci // C_SUB, ci <= ri).astype(BF16)
    bl_all = _nn_exact_lhs(tri, log_a)

    pj = lax.broadcasted_iota(jnp.int32, (C_SUB * C_SUB, 1), 0)
    pair_live = (pj // C_SUB >= pj % C_SUB).astype(F32)
    ej = lax.broadcasted_iota(jnp.int32, (C_SUB, C_SUB * C_SUB), 1)
    et = lax.broadcasted_iota(jnp.int32, (C_SUB, C_SUB * C_SUB), 0)
    pick_t = (ej // C_SUB == et).astype(BF16)
    sub_row = lax.broadcasted_iota(jnp.int32, (C_SUB, 1), 0)
    eye_k = _eye(C_DK, BF16)

    def rep_rows(x):
        return jnp.concatenate([jnp.broadcast_to(x[t:t + 1, :], (C_SUB, x.shape[1])) for t in range(C_SUB)], axis=0)

    def tile_rows(x):
        return jnp.concatenate([x] * C_SUB, axis=0)

    for i in range(chunk // C_SUB):
        rows = slice(i * C_SUB, (i + 1) * C_SUB)
        qf = q_ref[rows, :] * (C_DK ** -0.5)
        kf = k_ref[rows, :]
        vf = v_ref[rows, :]
        bl = bl_all[rows, :]
        state = st_ref[...]
        o = _nn((qf * jnp.exp(bl)).astype(BF16), state.astype(BF16))
        pair = rep_rows(qf) * tile_rows(kf) * jnp.exp(jnp.minimum(rep_rows(bl) - tile_rows(bl), 0.0))
        att = jnp.sum(pair, axis=1, keepdims=True) * pair_live
        o += _nn(pick_t, (att * tile_rows(vf)).astype(BF16))
        b_last = bl[t_sub - 1:t_sub, :]
        k_dec = jnp.where(sub_row < t_sub, kf * jnp.exp(jnp.minimum(b_last - bl, 0.0)), 0.0).astype(BF16)
        k_dec_t = _nt(eye_k, k_dec).astype(BF16)
        d_col = _nt_exact_lhs(eye_k, jnp.broadcast_to(jnp.exp(b_last), (8, C_DK)))[:, 0:1]
        st_ref[...] = d_col * state + _nn(k_dec_t, vf.astype(BF16))
        yv = o * lax.rsqrt(jnp.mean(o * o, axis=-1, keepdims=True) + NORM_EPS) * gn_ref[...]
        y_ref[rows, :] = (yv * _silu(gate_ref[rows, :])).astype(y_ref.dtype)

    @pl.when(c == nc - 1)
    def _():
        so_ref[0, 0] = st_ref[...]


def _gla(z, z_lr, batch, chunk, t_real, wg3, bg3, gn3, slot, state0):
    rows = z.shape[0]
    nc = rows // (batch * chunk)
    t_sub = min(t_real, C_SUB)
    kv = C_HEADS * C_DK * 2 // C_DV
    in_specs = [pl.BlockSpec((chunk, C_DK), lambda b, h, c: (b * nc + c, h)),
                pl.BlockSpec((chunk, C_DK), lambda b, h, c: (b * nc + c, C_HEADS + h)),
                pl.BlockSpec((chunk, C_DV), lambda b, h, c: (b * nc + c, kv + h)),
                pl.BlockSpec((chunk, C_DV), lambda b, h, c: (b * nc + c, kv + C_HEADS + h)),
                pl.BlockSpec((chunk, LANES), lambda b, h, c: (b * nc + c, 0)),
                pl.BlockSpec((None, C_RANK, C_DK), lambda b, h, c: (slot, 0, h)),
                pl.BlockSpec((None, 1, C_DK), lambda b, h, c: (slot, 0, h)),
                pl.BlockSpec((None, 1, C_DV), lambda b, h, c: (slot * C_HEADS + h, 0, 0))]
    args = [z, z, z, z, z_lr, wg3, bg3.reshape(bg3.shape[0], 1, -1), gn3.reshape(-1, 1, C_DV)]
    if state0 is not None:
        in_specs.append(pl.BlockSpec((1, 1, C_DK, C_DV), lambda b, h, c: (b, h, 0, 0)))
        args.append(state0)
    return pl.pallas_call(
        functools.partial(_gla_body, has_state=state0 is not None, nc=nc, chunk=chunk, t_sub=t_sub),
        grid=(batch, C_HEADS, nc),
        in_specs=in_specs,
        out_specs=[pl.BlockSpec((chunk, C_DV), lambda b, h, c: (b * nc + c, h)),
                   pl.BlockSpec((1, 1, C_DK, C_DV), lambda b, h, c: (b, h, 0, 0))],
        out_shape=[jax.ShapeDtypeStruct((rows, C_HEADS * C_DV), BF16),
                   jax.ShapeDtypeStruct((batch, C_HEADS, C_DK, C_DV), F32)],
        scratch_shapes=[pltpu.VMEM((C_DK, C_DV), F32)],
        compiler_params=_params(("arbitrary", "arbitrary", "arbitrary")),
        name="gla",
    )(*args)


def _rot_tables_a(pos):
    half = A_ROT // 2
    inv_freq = A_ROPE_THETA ** (-jnp.arange(half, dtype=F32) / half)
    ang = pos.astype(F32)[:, None] * inv_freq[None, :]
    cos, sin = jnp.cos(ang), jnp.sin(ang)
    n = pos.shape[0]
    rest = A_HEAD_DIM - A_ROT
    c = jnp.concatenate([cos, cos, jnp.ones((n, rest), F32)], axis=1)
    s1 = jnp.concatenate([-sin, jnp.zeros((n, half + rest), F32)], axis=1)
    s2 = jnp.concatenate([jnp.zeros((n, half), F32), sin, jnp.zeros((n, rest), F32)], axis=1)
    return c, s1, s2


def _rot_tables_b(pos):
    half = B_DK // 2
    inv_freq = B_ROPE_THETA ** (-jnp.arange(half, dtype=F32) / half)
    ang = pos.astype(F32)[:, None] * inv_freq[None, :]
    return jnp.cos(ang), jnp.sin(ang)


def kernel(x_prompt, x_sample, cache_a_kv1, cache_a_kv2, cache_a_kv3, state_b, state_c, norm_g, final_g,
           w_in_a, w_out_a, w_in_b, gn_b, w_out_b, w_in_c, w_gate2_c, b_gate_c, gn_c, w_out_c):
    bp, seq, d = x_prompt.shape
    bs, t_real, _ = x_sample.shape
    caches_a = (cache_a_kv1, cache_a_kv2, cache_a_kv3)
    width = A_HEADS * A_HEAD_DIM

    xp = x_prompt.reshape(bp * seq, d)
    xs = jnp.pad(x_sample, ((0, 0), (0, SAMPLE_PAD - t_real), (0, 0))).reshape(bs * SAMPLE_PAD, d)
    ms = xs.shape[0]
    tm_p = 1024

    pos_p = jnp.arange(seq)
    pos_s = jnp.tile(PAST_LEN + jnp.arange(SAMPLE_PAD), bs)
    tabs_a_p, tabs_a_s = _rot_tables_a(pos_p), _rot_tables_a(pos_s)
    tabs_b_p, tabs_b_s = _rot_tables_b(pos_p), _rot_tables_b(pos_s)

    hp = _rmsnorm(xp, norm_g[0], BF16)
    hs = _rmsnorm(xs, norm_g[0], BF16)

    a_rows_p, a_rows_s = [], []
    b_states_p, b_states_s, c_states_p, c_states_s = [], [], [], []
    n_layers = len(LAYER_KINDS)
    for i in range(n_layers):
        kind, slot = LAYER_KINDS[i], LAYER_SLOTS[i]
        last = i == n_layers - 1
        g_next = final_g if last else norm_g[i + 1]
        h_dtype = F32 if last else BF16
        if kind == 0:
            n_in = w_in_a.shape[2]
            zp = _proj(hp, w_in_a, slot, n_in, "a", tabs_a_p, seq // tm_p, tm_p, 1024)
            zs = _proj(hs, w_in_a, slot, n_in, "a", tabs_a_s, 1, ms, 1024)
            yp = _attn_prompt(zp, bp, seq)
            ys = _attn_sample(zs, caches_a, slot, bs, t_real)
            ys = jnp.pad(ys, ((0, 0), (0, SAMPLE_PAD - t_real), (0, 0))).reshape(ms, width).astype(BF16)
            zp3 = zp.reshape(bp, seq, n_in)
            zs3 = zs.reshape(bs, SAMPLE_PAD, n_in)
            rows_p, rows_s = [], []
            for g, (win, dil) in enumerate(A_GROUPS):
                keep = min(win, seq)
                lo = (3 * g + 1) * width
                rows_p.append(zp3[:, seq - keep:, lo:lo + 2 * width].reshape(bp, keep, 2, A_HEADS, A_HEAD_DIM))
                rows_s.append(zs3[:, :t_real, lo:lo + 2 * width].reshape(bs, t_real, 2, A_HEADS, A_HEAD_DIM))
            a_rows_p.append(rows_p)
            a_rows_s.append(rows_s)
            w_out = w_out_a
        elif kind == 1:
            n_in = w_in_b.shape[2]
            zp = _proj(hp, w_in_b, slot, n_in, "b", tabs_b_p, seq // tm_p, tm_p, 1024)
            zs = _proj(hs, w_in_b, slot, n_in, "b", tabs_b_s, 1, ms, 1024)
            yp, st_p = _retention(zp, bp, min(B_CHUNK, seq), min(B_CHUNK, seq), gn_b[slot], None)
            ys, st_s = _retention(zs, bs, SAMPLE_PAD, t_real, gn_b[slot], state_b[slot])
            b_states_p.append(st_p)
            b_states_s.append(st_s)
            w_out = w_out_b
        else:
            n_main = 2 * C_HEADS * C_DK + 2 * C_HEADS * C_DV
            w_lr = jnp.pad(w_in_c[slot][:, n_main:], ((0, 0), (0, LANES - C_RANK)))[None]
            zp = _proj(hp, w_in_c, slot, n_main, "c", (), 1, tm_p, 1024)
            zs = _proj(hs, w_in_c, slot, n_main, "c", (), 1, ms, 1024)
            zp_lr = _proj(hp, w_lr, 0, LANES, "c", (), 1, tm_p, LANES)
            zs_lr = _proj(hs, w_lr, 0, LANES, "c", (), 1, ms, LANES)
            yp, st_p = _gla(zp, zp_lr, bp, min(C_CHUNK, seq), min(C_CHUNK, seq),
                            w_gate2_c, b_gate_c, gn_c, slot, None)
            ys, st_s = _gla(zs, zs_lr, bs, SAMPLE_PAD, t_real, w_gate2_c, b_gate_c, gn_c, slot, state_c[slot])
            c_states_p.append(st_p)
            c_states_s.append(st_s)
            w_out = w_out_c
        xp, hp = _outproj(yp, w_out, slot, xp, g_next, h_dtype, 512, 512)
        xs, hs = _outproj(ys, w_out, slot, xs, g_next, h_dtype, ms, 512)

    y_prompt = hp.reshape(bp, seq, d)
    y_sample = hs.reshape(bs, SAMPLE_PAD, d)[:, :t_real]
    stack = lambda rows, g: jnp.stack([r[g] for r in rows])
    return (y_prompt, y_sample,
            stack(a_rows_p, 0), stack(a_rows_p, 1), stack(a_rows_p, 2),
            jnp.stack(b_states_p), jnp.stack(c_states_p),
            stack(a_rows_s, 0), stack(a_rows_s, 1), stack(a_rows_s, 2),
            jnp.stack(b_states_s), jnp.stack(c_states_s))
```

```python
import functools

import jax
import jax.numpy as jnp
from jax import lax
from jax.experimental import pallas as pl
from jax.experimental.pallas import tpu as pltpu

F32 = jnp.float32
BF16 = jnp.bfloat16

PAST_LEN = 16384
NORM_EPS = 1e-6
A_GROUPS = ((128, 1), (512, 4), (2048, 16))
A_HEADS = 16
A_HEAD_DIM = 128
A_ROT = A_HEAD_DIM // 4
A_ROPE_THETA = 500000.0
A_BAND = 128
B_HEADS = 8
B_DK = 256
B_DV = 512
B_CHUNK = 128
B_ROPE_THETA = 10000.0
C_HEADS = 4
C_DK = 256
C_DV = 512
C_RANK = 16
C_TAU = 16.0
C_CHUNK = 64
C_SUB = 16
LAYER_KINDS = (0, 1, 2, 0)
LAYER_SLOTS = (0, 0, 0, 1)

SAMPLE_PAD = 16
LANES = 128
VMEM_LIMIT = 56 * 1024 * 1024


def _params(sem, vmem=VMEM_LIMIT):
    return pltpu.CompilerParams(dimension_semantics=sem, vmem_limit_bytes=vmem)


def _silu(g):
    return g * (1.0 / (1.0 + jnp.exp(-g)))


def _nt(a, b):
    return lax.dot_general(a, b, (((1,), (1,)), ((), ())), preferred_element_type=F32)


def _nn(a, b):
    return jnp.dot(a, b, preferred_element_type=F32)


def _split(x):
    hi = x.astype(BF16)
    lo = (x - hi.astype(F32)).astype(BF16)
    return hi, lo


def _split3(x):
    hi = x.astype(BF16)
    r = x - hi.astype(F32)
    mid = r.astype(BF16)
    lo = (r - mid.astype(F32)).astype(BF16)
    return hi, mid, lo


def _nn_exact_lhs(sel, x):
    hi, mid, lo = _split3(x)
    return _nn(sel, hi) + _nn(sel, mid) + _nn(sel, lo)


def _nn_hi(a, b):
    ah, al = _split(a)
    bh, bl = _split(b)
    return _nn(ah, bh) + _nn(ah, bl) + _nn(al, bh)


def _rmsnorm_body(x_ref, g_ref, o_ref):
    x = x_ref[...]
    ms = jnp.mean(x * x, axis=-1, keepdims=True)
    o_ref[...] = (x * lax.rsqrt(ms + NORM_EPS) * g_ref[...]).astype(o_ref.dtype)


def _rmsnorm(x, g, out_dtype):
    m, d = x.shape
    tm = min(m, 512)
    return pl.pallas_call(
        _rmsnorm_body,
        grid=(m // tm,),
        in_specs=[pl.BlockSpec((tm, d), lambda i: (i, 0)),
                  pl.BlockSpec((1, d), lambda i: (0, 0))],
        out_specs=pl.BlockSpec((tm, d), lambda i: (i, 0)),
        out_shape=jax.ShapeDtypeStruct((m, d), out_dtype),
        compiler_params=_params(("arbitrary",)),
        name="rmsnorm",
    )(x, g.reshape(1, d))


def _proj_body(h_ref, w_ref, *refs, kind, tn):
    if kind == "a":
        c_ref, s1_ref, s2_ref, z_ref, wb_ref = refs
    elif kind == "b":
        c_ref, s_ref, z_ref, wb_ref = refs
    else:
        z_ref, wb_ref = refs
    j = pl.program_id(0)
    i = pl.program_id(1)

    @pl.when(i == 0)
    def _():
        wb_ref[...] = w_ref[...].astype(BF16)

    z_ref[...] = _nn(h_ref[...], wb_ref[...])

    if kind == "a":
        seg = (j * tn) // (A_HEADS * A_HEAD_DIM)
        is_rot = jnp.logical_and(seg < 3 * len(A_GROUPS), seg % 3 < 2)

        @pl.when(is_rot)
        def _():
            c, s1, s2 = c_ref[...], s1_ref[...], s2_ref[...]
            half = A_ROT // 2
            for hh in range(tn // A_HEAD_DIM):
                sl = slice(hh * A_HEAD_DIM, (hh + 1) * A_HEAD_DIM)
                x = z_ref[:, sl]
                x_hi = pltpu.roll(x, A_HEAD_DIM - half, 1)
                x_lo = pltpu.roll(x, half, 1)
                z_ref[:, sl] = x * c + x_hi * s1 + x_lo * s2
    elif kind == "b":
        n_qk = B_HEADS * B_DK
        is_rot = (j * tn) < 2 * n_qk
        is_k = (j * tn) >= n_qk

        @pl.when(is_rot)
        def _():
            cos, sin = c_ref[...], s_ref[...]
            scale = jnp.where(is_k, B_DK ** -0.5, 1.0).astype(F32)
            half = B_DK // 2
            for hh in range(tn // B_DK):
                o = hh * B_DK
                x1 = z_ref[:, o:o + half]
                x2 = z_ref[:, o + half:o + B_DK]
                z_ref[:, o:o + half] = (x1 * cos - x2 * sin) * scale
                z_ref[:, o + half:o + B_DK] = (x1 * sin + x2 * cos) * scale


def _proj(h, w3, slot, n_out, kind, tabs, tab_blocks, tm, tn):
    m, k = h.shape
    in_specs = [pl.BlockSpec((tm, k), lambda j, i: (i, 0)),
                pl.BlockSpec((None, k, tn), lambda j, i: (slot, 0, j))]
    for _ in tabs:
        in_specs.append(pl.BlockSpec((tm, LANES), lambda j, i: (i % tab_blocks, 0)))
    return pl.pallas_call(
        functools.partial(_proj_body, kind=kind, tn=tn),
        grid=(n_out // tn, m // tm),
        in_specs=in_specs,
        out_specs=pl.BlockSpec((tm, tn), lambda j, i: (i, j)),
        out_shape=jax.ShapeDtypeStruct((m, n_out), F32),
        scratch_shapes=[pltpu.VMEM((k, tn), BF16)],
        compiler_params=_params(("arbitrary", "arbitrary")),
        name="proj_" + kind,
    )(h, w3, *tabs)


def _cast_body(x_ref, o_ref):
    o_ref[...] = x_ref[...].astype(o_ref.dtype)


def _cast_bf16(w3):
    n, k, d = w3.shape
    rows = n * k
    tr = min(rows, 1024)
    out = pl.pallas_call(
        _cast_body,
        grid=(rows // tr,),
        in_specs=[pl.BlockSpec((tr, d), lambda i: (i, 0))],
        out_specs=pl.BlockSpec((tr, d), lambda i: (i, 0)),
        out_shape=jax.ShapeDtypeStruct((rows, d), BF16),
        compiler_params=_params(("arbitrary",)),
        name="cast_bf16",
    )(w3.reshape(rows, d))
    return out.reshape(n, k, d)


def _outproj_body(y_ref, w_ref, x_ref, g_ref, xo_ref, ho_ref):
    xn = x_ref[...] + _nn(y_ref[...], w_ref[...])
    xo_ref[...] = xn
    ms = jnp.mean(xn * xn, axis=-1, keepdims=True)
    ho_ref[...] = (xn * lax.rsqrt(ms + NORM_EPS) * g_ref[...]).astype(ho_ref.dtype)


def _outproj(y, w3, slot, x, g, h_dtype, tm):
    m, kd = y.shape
    d = x.shape[1]
    return pl.pallas_call(
        _outproj_body,
        grid=(m // tm,),
        in_specs=[pl.BlockSpec((tm, kd), lambda i: (i, 0)),
                  pl.BlockSpec((None, kd, d), lambda i: (slot, 0, 0), pipeline_mode=pl.Buffered(1)),
                  pl.BlockSpec((tm, d), lambda i: (i, 0)),
                  pl.BlockSpec((1, d), lambda i: (0, 0))],
        out_specs=[pl.BlockSpec((tm, d), lambda i: (i, 0)),
                   pl.BlockSpec((tm, d), lambda i: (i, 0))],
        out_shape=[jax.ShapeDtypeStruct((m, d), F32),
                   jax.ShapeDtypeStruct((m, d), h_dtype)],
        compiler_params=_params(("arbitrary",)),
        name="outproj",
    )(y, w3, x, g.reshape(1, d))


LOG2_E = 1.4426950408889634
LN_2 = 0.6931471805599453


def _band_attention(q, k, v, mask):
    s = jnp.where(mask, _nt(q, k), -jnp.inf)
    mx = jnp.max(s, axis=-1, keepdims=True)
    p = jnp.exp2(s - mx)
    den = jnp.sum(p, axis=-1, keepdims=True)
    return _nn(p.astype(BF16), v) / den, mx * LN_2 + jnp.log(den)


def _rows(start, size, stride):
    return pl.ds(start, size) if stride == 1 else pl.ds(start, size, stride=stride)


def _attn_prompt_body(*refs, seq):
    qkv_refs, gate_ref, y_ref, o_s, l_s = refs[:9], refs[9], refs[10], refs[11], refs[12]
    row1 = lax.broadcasted_iota(jnp.int32, (A_BAND, A_BAND), 0)
    col1 = lax.broadcasted_iota(jnp.int32, (A_BAND, A_BAND), 1)
    mask_first = col1 <= row1
    row2 = lax.broadcasted_iota(jnp.int32, (A_BAND, 2 * A_BAND), 0)
    col2 = lax.broadcasted_iota(jnp.int32, (A_BAND, 2 * A_BAND), 1)
    mask_band = jnp.logical_and(col2 >= row2, col2 <= row2 + A_BAND)
    q_scale = (A_HEAD_DIM ** -0.5) * LOG2_E
    for g, (_, dil) in enumerate(A_GROUPS):
        q_ref, k_ref, v_ref = qkv_refs[3 * g:3 * g + 3]
        for r in range(dil):
            k_prev = v_prev = None
            for n in range(seq // dil // A_BAND):
                qs = _rows(r + dil * A_BAND * n, A_BAND, dil)
                q = (q_ref[qs, :] * q_scale).astype(BF16)
                k_cur = k_ref[qs, :].astype(BF16)
                v_cur = v_ref[qs, :].astype(BF16)
                if n == 0:
                    o, lse = _band_attention(q, k_cur, v_cur, mask_first)
                else:
                    o, lse = _band_attention(q, jnp.concatenate([k_prev, k_cur], axis=0),
                                             jnp.concatenate([v_prev, v_cur], axis=0), mask_band)
                k_prev, v_prev = k_cur, v_cur
                o_s[g, qs, :] = o
                l_s[g, qs, :] = lse
    l1, l2, l3 = l_s[0], l_s[1], l_s[2]
    mx = jnp.maximum(jnp.maximum(l1, l2), l3)
    e1, e2, e3 = jnp.exp(l1 - mx), jnp.exp(l2 - mx), jnp.exp(l3 - mx)
    tot = e1 + e2 + e3
    o = (e1 / tot) * o_s[0] + (e2 / tot) * o_s[1] + (e3 / tot) * o_s[2]
    y_ref[...] = (o * _silu(gate_ref[...])).astype(y_ref.dtype)


def _attn_prompt(z, batch, seq):
    n_groups = len(A_GROUPS)
    in_specs = []
    for c in range(3 * n_groups + 1):
        in_specs.append(pl.BlockSpec((seq, A_HEAD_DIM), lambda b, h, c=c: (b, c * A_HEADS + h)))
    return pl.pallas_call(
        functools.partial(_attn_prompt_body, seq=seq),
        grid=(batch, A_HEADS),
        in_specs=in_specs,
        out_specs=pl.BlockSpec((seq, A_HEAD_DIM), lambda b, h: (b, h)),
        out_shape=jax.ShapeDtypeStruct((batch * seq, A_HEADS * A_HEAD_DIM), BF16),
        scratch_shapes=[pltpu.VMEM((n_groups, seq, A_HEAD_DIM), F32),
                        pltpu.VMEM((n_groups, seq, 1), F32)],
        compiler_params=_params(("arbitrary", "arbitrary")),
        name="attn_prompt",
    )(*([z] * (3 * n_groups + 1)))


def _attn_sample_body(z_ref, c1_ref, c2_ref, c3_ref, y_ref):
    t = pl.program_id(1)
    scale = A_HEAD_DIM ** -0.5
    m_idx = lax.broadcasted_iota(jnp.int32, (A_BAND, A_HEADS, 1), 0)
    n_idx = lax.broadcasted_iota(jnp.int32, (SAMPLE_PAD, A_HEADS, 1), 0)
    caches = (c1_ref, c2_ref, c3_ref)
    outs, lses = [], []
    for g, (_, dil) in enumerate(A_GROUPS):
        base = 3 * g * A_HEADS
        q = z_ref[0, t, base:base + A_HEADS, :]
        kc = caches[g][0, :, 0:A_HEADS, :]
        vc = caches[g][0, :, A_HEADS:2 * A_HEADS, :]
        kn = z_ref[0, :, base + A_HEADS:base + 2 * A_HEADS, :]
        vn = z_ref[0, :, base + 2 * A_HEADS:base + 3 * A_HEADS, :]
        s_c = jnp.sum(kc * q[None], axis=-1, keepdims=True) * scale
        s_n = jnp.sum(kn * q[None], axis=-1, keepdims=True) * scale
        if dil == 1:
            s_c = jnp.where(m_idx >= t, s_c, -jnp.inf)
            s_n = jnp.where(n_idx <= t, s_n, -jnp.inf)
        else:
            s_n = jnp.where(n_idx == t, s_n, -jnp.inf)
        mx = jnp.maximum(jnp.max(s_c, axis=0), jnp.max(s_n, axis=0))
        p_c = jnp.exp(s_c - mx[None])
        p_n = jnp.exp(s_n - mx[None])
        den = jnp.sum(p_c, axis=0) + jnp.sum(p_n, axis=0)
        num = jnp.sum(p_c * vc, axis=0) + jnp.sum(p_n * vn, axis=0)
        outs.append(num / den)
        lses.append(mx + jnp.log(den))
    lmax = jnp.maximum(jnp.maximum(lses[0], lses[1]), lses[2])
    es = [jnp.exp(l - lmax) for l in lses]
    tot = es[0] + es[1] + es[2]
    o = (es[0] / tot) * outs[0] + (es[1] / tot) * outs[1] + (es[2] / tot) * outs[2]
    gate = z_ref[0, t, 3 * len(A_GROUPS) * A_HEADS:(3 * len(A_GROUPS) + 1) * A_HEADS, :]
    y_ref[0, 0] = o * _silu(gate)


def _attn_sample(zs, caches, slot, batch, t_real):
    n_in = zs.shape[1]
    z5 = zs.reshape(batch, SAMPLE_PAD, n_in // A_HEAD_DIM, A_HEAD_DIM)
    in_specs = [pl.BlockSpec((1, SAMPLE_PAD, n_in // A_HEAD_DIM, A_HEAD_DIM), lambda b, t: (b, 0, 0, 0))]
    args = [z5]
    for g, (win, dil) in enumerate(A_GROUPS):
        n_a = caches[g].shape[0]
        cv = caches[g].reshape(n_a * batch, win // dil, dil * 2 * A_HEADS, A_HEAD_DIM)
        if dil == 1:
            idx = lambda b, t: (slot * batch + b, 0, 0, 0)
        else:
            idx = lambda b, t: (slot * batch + b, 0, t, 0)
        in_specs.append(pl.BlockSpec((1, A_BAND, 2 * A_HEADS, A_HEAD_DIM), idx))
        args.append(cv)
    y = pl.pallas_call(
        _attn_sample_body,
        grid=(batch, t_real),
        in_specs=in_specs,
        out_specs=pl.BlockSpec((1, 1, A_HEADS, A_HEAD_DIM), lambda b, t: (b, t, 0, 0)),
        out_shape=jax.ShapeDtypeStruct((batch, t_real, A_HEADS, A_HEAD_DIM), F32),
        compiler_params=_params(("arbitrary", "arbitrary")),
        name="attn_sample",
    )(*args)
    return y.reshape(batch, t_real, A_HEADS * A_HEAD_DIM)


def _tn(a, b):
    return lax.dot_general(a, b, (((0,), (0,)), ((), ())), preferred_element_type=F32)


def _retention_body(*refs, has_state):
    if has_state:
        q_ref, k_ref, v_ref, gate_ref, dec_ref, xi_ref, kd_ref, sd_ref, gn_ref, s0_ref, y_ref, st_ref = refs
    else:
        q_ref, k_ref, v_ref, gate_ref, dec_ref, xi_ref, kd_ref, sd_ref, gn_ref, y_ref, st_ref = refs
    c = pl.program_id(1)

    @pl.when(c == 0)
    def _():
        if has_state:
            st_ref[...] = s0_ref[...]
        else:
            st_ref[...] = jnp.zeros_like(st_ref)

    for h in range(B_HEADS):
        ks = slice(h * B_DK, (h + 1) * B_DK)
        vs = slice(h * B_DV, (h + 1) * B_DV)
        qb = q_ref[:, ks].astype(BF16)
        kf = k_ref[:, ks]
        vb = v_ref[:, vs].astype(BF16)
        state = st_ref[0, h]
        scores = _nt(qb, kf.astype(BF16)) * dec_ref[h]
        o = _nn(scores.astype(BF16), vb) + _nn(qb, state.astype(BF16)) * xi_ref[h]
        k_dec = (kf * kd_ref[h]).astype(BF16)
        st_ref[0, h] = sd_ref[h] * state + _tn(k_dec, vb)
        mu = jnp.mean(o, axis=-1, keepdims=True)
        var = jnp.mean(jnp.square(o - mu), axis=-1, keepdims=True)
        yv = (o - mu) * lax.rsqrt(var + NORM_EPS) * gn_ref[h]
        y_ref[:, vs] = (yv * _silu(gate_ref[:, vs])).astype(y_ref.dtype)


def _retention(z, batch, chunk, t_real, gn, state0):
    rows = z.shape[0]
    nc = rows // (batch * chunk)
    lg = jnp.log(1.0 - 2.0 ** (-5.0 - jnp.arange(B_HEADS, dtype=F32)))[:, None, None]
    idx = jnp.arange(chunk, dtype=F32)
    diff = idx[:, None] - idx[None, :]
    dec = jnp.where(diff >= 0, jnp.exp(lg * jnp.maximum(diff, 0.0)), 0.0)
    xi = jnp.exp(lg * (idx[None, :, None] + 1.0))
    live = idx[None, :, None] < t_real
    kd = jnp.where(live, jnp.exp(lg * jnp.where(live, t_real - 1.0 - idx[None, :, None], 0.0)), 0.0)
    sd = jnp.exp(lg * t_real)
    n_k, n_v = B_HEADS * B_DK, B_HEADS * B_DV
    const = lambda b, c: (0, 0, 0)
    in_specs = [pl.BlockSpec((chunk, n_k), lambda b, c: (b * nc + c, 0)),
                pl.BlockSpec((chunk, n_k), lambda b, c: (b * nc + c, 1)),
                pl.BlockSpec((chunk, n_v), lambda b, c: (b * nc + c, 2 * n_k // n_v)),
                pl.BlockSpec((chunk, n_v), lambda b, c: (b * nc + c, 2 * n_k // n_v + 1)),
                pl.BlockSpec((B_HEADS, chunk, chunk), const),
                pl.BlockSpec((B_HEADS, chunk, 1), const),
                pl.BlockSpec((B_HEADS, chunk, 1), const),
                pl.BlockSpec((B_HEADS, 1, 1), const),
                pl.BlockSpec((B_HEADS, 1, B_DV), const)]
    args = [z, z, z, z, dec, xi, kd, sd, gn.reshape(B_HEADS, 1, B_DV)]
    state_spec = pl.BlockSpec((1, B_HEADS, B_DK, B_DV), lambda b, c: (b, 0, 0, 0))
    if state0 is not None:
        in_specs.append(state_spec)
        args.append(state0)
    return pl.pallas_call(
        functools.partial(_retention_body, has_state=state0 is not None),
        grid=(batch, nc),
        in_specs=in_specs,
        out_specs=[pl.BlockSpec((chunk, n_v), lambda b, c: (b * nc + c, 0)), state_spec],
        out_shape=[jax.ShapeDtypeStruct((rows, n_v), BF16),
                   jax.ShapeDtypeStruct((batch, B_HEADS, B_DK, B_DV), F32)],
        compiler_params=_params(("arbitrary", "arbitrary")),
        name="retention",
    )(*args)


def _log_sigmoid(x):
    return jnp.minimum(x, 0.0) - jnp.log(1.0 + jnp.exp(-jnp.abs(x)))


def _gla_head(q, k, v, lr, wg, bg, state, chunk, t_sub):
    nsub = chunk // C_SUB
    log_a = _log_sigmoid(_nn_hi(lr, wg) + bg) * (1.0 / C_TAU)
    ri = lax.broadcasted_iota(jnp.int32, (chunk, chunk), 0)
    ci = lax.broadcasted_iota(jnp.int32, (chunk, chunk), 1)
    tri = jnp.logical_and(ri // C_SUB == ci // C_SUB, ci <= ri).astype(BF16)
    bl = _nn_exact_lhs(tri, log_a)

    def block_rows(vals):
        return jnp.concatenate([jnp.broadcast_to(x, (C_SUB, C_DK)) for x in vals], axis=0)

    tot = [bl[j * C_SUB + t_sub - 1:j * C_SUB + t_sub, :] for j in range(nsub)]
    beta = [jnp.zeros((1, C_DK), F32)]
    for j in range(nsub):
        beta.append(beta[-1] + tot[j])
    live = lax.broadcasted_iota(jnp.int32, (chunk, 1), 0) % C_SUB < t_sub
    qs = q * (C_DK ** -0.5)
    q_in = qs * jnp.exp(bl)
    k_out = jnp.where(live, k * jnp.exp(jnp.minimum(block_rows(tot) - bl, 0.0)), 0.0)
    vb = v.astype(BF16)

    o = _nn((q_in * block_rows([jnp.exp(x) for x in beta[:nsub]])).astype(BF16), state.astype(BF16))

    pj = lax.broadcasted_iota(jnp.int32, (C_SUB * C_SUB, 1), 0)
    pair_live = (pj // C_SUB >= pj % C_SUB).astype(F32)
    pc = lax.broadcasted_iota(jnp.int32, (C_SUB * C_SUB, chunk), 1)
    ps = lax.broadcasted_iota(jnp.int32, (C_SUB * C_SUB, chunk), 0) % C_SUB
    et = lax.broadcasted_iota(jnp.int32, (C_SUB, C_SUB * C_SUB), 0)
    ej = lax.broadcasted_iota(jnp.int32, (C_SUB, C_SUB * C_SUB), 1)
    pick_t = (ej // C_SUB == et).astype(BF16)

    def rep_rows(x):
        return jnp.concatenate([jnp.broadcast_to(x[t:t + 1, :], (C_SUB, C_DK)) for t in range(C_SUB)], axis=0)

    def tile_rows(x):
        return jnp.concatenate([x] * C_SUB, axis=0)

    a_rows = []
    for i in range(nsub):
        r = slice(i * C_SUB, (i + 1) * C_SUB)
        pair = rep_rows(qs[r]) * tile_rows(k[r]) * jnp.exp(jnp.minimum(rep_rows(bl[r]) - tile_rows(bl[r]), 0.0))
        att = jnp.sum(pair, axis=1, keepdims=True) * pair_live
        placed = jnp.where(pc == ps + i * C_SUB, att, 0.0).astype(BF16)
        a_i = _nn(pick_t, placed)
        if i > 0:
            between = [jnp.broadcast_to(jnp.exp(beta[i] - beta[j + 1]), (C_SUB, C_DK)) for j in range(i)]
            between.append(jnp.zeros(((nsub - i) * C_SUB, C_DK), F32))
            a_i += _nt(q_in[r].astype(BF16), (k_out * jnp.concatenate(between, axis=0)).astype(BF16))
        a_rows.append(a_i)
    a = a_rows[0] if nsub == 1 else jnp.concatenate(a_rows, axis=0)
    o += _nn(a.astype(BF16), vb)

    k_end = k_out * block_rows([jnp.exp(beta[nsub] - beta[j + 1]) for j in range(nsub)])
    d_col = jnp.broadcast_to(jnp.exp(beta[nsub]), (8, C_DK)).T[:, 0:1]
    return o, d_col * state + _tn(k_end.astype(BF16), vb)


def _gla_body(*refs, has_state, chunk, t_sub):
    if has_state:
        q_ref, k_ref, v_ref, gate_ref, lr_ref, wg_ref, bg_ref, gn_ref, s0_ref, y_ref, st_ref = refs
    else:
        q_ref, k_ref, v_ref, gate_ref, lr_ref, wg_ref, bg_ref, gn_ref, y_ref, st_ref = refs
    c = pl.program_id(1)

    @pl.when(c == 0)
    def _():
        if has_state:
            st_ref[...] = s0_ref[...]
        else:
            st_ref[...] = jnp.zeros_like(st_ref)

    lr = lr_ref[:, 0:C_RANK]
    for h in range(C_HEADS):
        ks = slice(h * C_DK, (h + 1) * C_DK)
        vs = slice(h * C_DV, (h + 1) * C_DV)
        o, st_ref[0, h] = _gla_head(q_ref[:, ks], k_ref[:, ks], v_ref[:, vs], lr, wg_ref[:, ks], bg_ref[:, ks],
                                    st_ref[0, h], chunk, t_sub)
        yv = o * lax.rsqrt(jnp.mean(o * o, axis=-1, keepdims=True) + NORM_EPS) * gn_ref[:, vs]
        y_ref[:, vs] = (yv * _silu(gate_ref[:, vs])).astype(y_ref.dtype)


def _gla(z, z_lr, batch, chunk, t_real, wg3, bg3, gn3, slot, state0):
    rows = z.shape[0]
    nc = rows // (batch * chunk)
    t_sub = min(t_real, C_SUB)
    n_k, n_v = C_HEADS * C_DK, C_HEADS * C_DV
    n_slots = wg3.shape[0]
    in_specs = [pl.BlockSpec((chunk, n_k), lambda b, c: (b * nc + c, 0)),
                pl.BlockSpec((chunk, n_k), lambda b, c: (b * nc + c, 1)),
                pl.BlockSpec((chunk, n_v), lambda b, c: (b * nc + c, 2 * n_k // n_v)),
                pl.BlockSpec((chunk, n_v), lambda b, c: (b * nc + c, 2 * n_k // n_v + 1)),
                pl.BlockSpec((chunk, LANES), lambda b, c: (b * nc + c, 0)),
                pl.BlockSpec((None, C_RANK, n_k), lambda b, c: (slot, 0, 0)),
                pl.BlockSpec((None, 1, n_k), lambda b, c: (slot, 0, 0)),
                pl.BlockSpec((None, 1, n_v), lambda b, c: (slot, 0, 0))]
    args = [z, z, z, z, z_lr, wg3, bg3.reshape(n_slots, 1, n_k), gn3.reshape(n_slots, 1, n_v)]
    state_spec = pl.BlockSpec((1, C_HEADS, C_DK, C_DV), lambda b, c: (b, 0, 0, 0))
    if state0 is not None:
        in_specs.append(state_spec)
        args.append(state0)
    return pl.pallas_call(
        functools.partial(_gla_body, has_state=state0 is not None, chunk=chunk, t_sub=t_sub),
        grid=(batch, nc),
        in_specs=in_specs,
        out_specs=[pl.BlockSpec((chunk, n_v), lambda b, c: (b * nc + c, 0)), state_spec],
        out_shape=[jax.ShapeDtypeStruct((rows, n_v), BF16),
                   jax.ShapeDtypeStruct((batch, C_HEADS, C_DK, C_DV), F32)],
        compiler_params=_params(("arbitrary", "arbitrary")),
        name="gla",
    )(*args)


def _rot_tables_a(pos):
    half = A_ROT // 2
    inv_freq = A_ROPE_THETA ** (-jnp.arange(half, dtype=F32) / half)
    ang = pos.astype(F32)[:, None] * inv_freq[None, :]
    cos, sin = jnp.cos(ang), jnp.sin(ang)
    n = pos.shape[0]
    rest = A_HEAD_DIM - A_ROT
    c = jnp.concatenate([cos, cos, jnp.ones((n, rest), F32)], axis=1)
    s1 = jnp.concatenate([-sin, jnp.zeros((n, half + rest), F32)], axis=1)
    s2 = jnp.concatenate([jnp.zeros((n, half), F32), sin, jnp.zeros((n, rest), F32)], axis=1)
    return c, s1, s2


def _rot_tables_b(pos):
    half = B_DK // 2
    inv_freq = B_ROPE_THETA ** (-jnp.arange(half, dtype=F32) / half)
    ang = pos.astype(F32)[:, None] * inv_freq[None, :]
    return jnp.cos(ang), jnp.sin(ang)


def kernel(x_prompt, x_sample, cache_a_kv1, cache_a_kv2, cache_a_kv3, state_b, state_c, norm_g, final_g,
           w_in_a, w_out_a, w_in_b, gn_b, w_out_b, w_in_c, w_gate2_c, b_gate_c, gn_c, w_out_c):
    bp, seq, d = x_prompt.shape
    bs, t_real, _ = x_sample.shape
    caches_a = (cache_a_kv1, cache_a_kv2, cache_a_kv3)
    width = A_HEADS * A_HEAD_DIM

    xp = x_prompt.reshape(bp * seq, d)
    xs = jnp.pad(x_sample, ((0, 0), (0, SAMPLE_PAD - t_real), (0, 0))).reshape(bs * SAMPLE_PAD, d)
    ms = xs.shape[0]
    tm_p = 1024

    pos_p = jnp.arange(seq)
    pos_s = jnp.tile(PAST_LEN + jnp.arange(SAMPLE_PAD), bs)
    tabs_a_p, tabs_a_s = _rot_tables_a(pos_p), _rot_tables_a(pos_s)
    tabs_b_p, tabs_b_s = _rot_tables_b(pos_p), _rot_tables_b(pos_s)

    hp = _rmsnorm(xp, norm_g[0], BF16)
    hs = _rmsnorm(xs, norm_g[0], BF16)
    w_out_a, w_out_b, w_out_c = _cast_bf16(w_out_a), _cast_bf16(w_out_b), _cast_bf16(w_out_c)

    a_rows_p, a_rows_s = [], []
    b_states_p, b_states_s, c_states_p, c_states_s = [], [], [], []
    n_layers = len(LAYER_KINDS)
    for i in range(n_layers):
        kind, slot = LAYER_KINDS[i], LAYER_SLOTS[i]
        last = i == n_layers - 1
        g_next = final_g if last else norm_g[i + 1]
        h_dtype = F32 if last else BF16
        if kind == 0:
            n_in = w_in_a.shape[2]
            zp = _proj(hp, w_in_a, slot, n_in, "a", tabs_a_p, seq // tm_p, tm_p, 1024)
            zs = _proj(hs, w_in_a, slot, n_in, "a", tabs_a_s, 1, ms, 1024)
            yp = _attn_prompt(zp, bp, seq)
            ys = _attn_sample(zs, caches_a, slot, bs, t_real)
            ys = jnp.pad(ys, ((0, 0), (0, SAMPLE_PAD - t_real), (0, 0))).reshape(ms, width).astype(BF16)
            zp3 = zp.reshape(bp, seq, n_in)
            zs3 = zs.reshape(bs, SAMPLE_PAD, n_in)
            rows_p, rows_s = [], []
            for g, (win, dil) in enumerate(A_GROUPS):
                keep = min(win, seq)
                lo = (3 * g + 1) * width
                rows_p.append(zp3[:, seq - keep:, lo:lo + 2 * width].reshape(bp, keep, 2, A_HEADS, A_HEAD_DIM))
                rows_s.append(zs3[:, :t_real, lo:lo + 2 * width].reshape(bs, t_real, 2, A_HEADS, A_HEAD_DIM))
            a_rows_p.append(rows_p)
            a_rows_s.append(rows_s)
            w_out = w_out_a
        elif kind == 1:
            n_in = w_in_b.shape[2]
            zp = _proj(hp, w_in_b, slot, n_in, "b", tabs_b_p, seq // tm_p, tm_p, 1024)
            zs = _proj(hs, w_in_b, slot, n_in, "b", tabs_b_s, 1, ms, 1024)
            yp, st_p = _retention(zp, bp, min(B_CHUNK, seq), min(B_CHUNK, seq), gn_b[slot], None)
            ys, st_s = _retention(zs, bs, SAMPLE_PAD, t_real, gn_b[slot], state_b[slot])
            b_states_p.append(st_p)
            b_states_s.append(st_s)
            w_out = w_out_b
        else:
            n_main = 2 * C_HEADS * C_DK + 2 * C_HEADS * C_DV
            w_lr = jnp.pad(w_in_c[slot][:, n_main:], ((0, 0), (0, LANES - C_RANK)))[None]
            zp = _proj(hp, w_in_c, slot, n_main, "c", (), 1, tm_p, 1024)
            zs = _proj(hs, w_in_c, slot, n_main, "c", (), 1, ms, 1024)
            zp_lr = _proj(hp, w_lr, 0, LANES, "c", (), 1, tm_p, LANES)
            zs_lr = _proj(hs, w_lr, 0, LANES, "c", (), 1, ms, LANES)
            yp, st_p = _gla(zp, zp_lr, bp, min(C_CHUNK, seq), min(C_CHUNK, seq),
                            w_gate2_c, b_gate_c, gn_c, slot, None)
            ys, st_s = _gla(zs, zs_lr, bs, SAMPLE_PAD, t_real, w_gate2_c, b_gate_c, gn_c, slot, state_c[slot])
            c_states_p.append(st_p)
            c_states_s.append(st_s)
            w_out = w_out_c
        xp, hp = _outproj(yp, w_out, slot, xp, g_next, h_dtype, 512)
        xs, hs = _outproj(ys, w_out, slot, xs, g_next, h_dtype, ms)

    y_prompt = hp.reshape(bp, seq, d)
    y_sample = hs.reshape(bs, SAMPLE_PAD, d)[:, :t_real]
    stack = lambda rows, g: jnp.stack([r[g] for r in rows])
    return (y_prompt, y_sample,
            stack(a_rows_p, 0), stack(a_rows_p, 1), stack(a_rows_p, 2),
            jnp.stack(b_states_p), jnp.stack(c_states_p),
            stack(a_rows_s, 0), stack(a_rows_s, 1), stack(a_rows_s, 2),
            jnp.stack(b_states_s), jnp.stack(c_states_s))
```

```python
import functools

import jax
import jax.numpy as jnp
from jax import lax
from jax.experimental import pallas as pl
from jax.experimental.pallas import tpu as pltpu

F32 = jnp.float32
BF16 = jnp.bfloat16

PAST_LEN = 16384
NORM_EPS = 1e-6
A_GROUPS = ((128, 1), (512, 4), (2048, 16))
A_HEADS = 16
A_HEAD_DIM = 128
A_ROT = A_HEAD_DIM // 4
A_ROPE_THETA = 500000.0
A_BAND = 128
B_HEADS = 8
B_DK = 256
B_DV = 512
B_CHUNK = 128
B_ROPE_THETA = 10000.0
C_HEADS = 4
C_DK = 256
C_DV = 512
C_RANK = 16
C_TAU = 16.0
C_CHUNK = 64
C_SUB = 16
LAYER_KINDS = (0, 1, 2, 0)
LAYER_SLOTS = (0, 0, 0, 1)

SAMPLE_PAD = 16
LANES = 128
MXU_N = 256
VMEM_LIMIT = 56 * 1024 * 1024


def _params(sem, vmem=VMEM_LIMIT):
    return pltpu.CompilerParams(dimension_semantics=sem, vmem_limit_bytes=vmem)


def _silu(g):
    return g * (1.0 / (1.0 + jnp.exp(-g)))


def _nt(a, b):
    return lax.dot_general(a, b, (((1,), (1,)), ((), ())), preferred_element_type=F32)


def _nn(a, b):
    return jnp.dot(a, b, preferred_element_type=F32)


def _split(x):
    hi = x.astype(BF16)
    lo = (x - hi.astype(F32)).astype(BF16)
    return hi, lo


def _split3(x):
    hi = x.astype(BF16)
    r = x - hi.astype(F32)
    mid = r.astype(BF16)
    lo = (r - mid.astype(F32)).astype(BF16)
    return hi, mid, lo


def _nn_exact_lhs(sel, x):
    hi, mid, lo = _split3(x)
    return _nn(sel, hi) + _nn(sel, mid) + _nn(sel, lo)


def _nn_hi(a, b):
    ah, al = _split(a)
    bh, bl = _split(b)
    return _nn(ah, bh) + _nn(ah, bl) + _nn(al, bh)


def _rmsnorm_body(x_ref, g_ref, o_ref):
    x = x_ref[...]
    ms = jnp.mean(x * x, axis=-1, keepdims=True)
    o_ref[...] = (x * lax.rsqrt(ms + NORM_EPS) * g_ref[...]).astype(o_ref.dtype)


def _rmsnorm(x, g, out_dtype):
    m, d = x.shape
    tm = min(m, 512)
    return pl.pallas_call(
        _rmsnorm_body,
        grid=(m // tm,),
        in_specs=[pl.BlockSpec((tm, d), lambda i: (i, 0)),
                  pl.BlockSpec((1, d), lambda i: (0, 0))],
        out_specs=pl.BlockSpec((tm, d), lambda i: (i, 0)),
        out_shape=jax.ShapeDtypeStruct((m, d), out_dtype),
        compiler_params=_params(("arbitrary",)),
        name="rmsnorm",
    )(x, g.reshape(1, d))


def _proj_body(h_ref, w_ref, *refs, kind, tn):
    if kind == "a":
        c_ref, s1_ref, s2_ref, z_ref, wb_ref = refs
    elif kind == "b":
        c_ref, s_ref, z_ref, wb_ref = refs
    else:
        z_ref, wb_ref = refs
    j = pl.program_id(0)
    i = pl.program_id(1)

    @pl.when(i == 0)
    def _():
        wb_ref[...] = w_ref[...].astype(BF16)

    def tile(epilogue):
        slab = min(tn, MXU_N)
        for cs in range(0, tn, slab):
            acc = _nn(h_ref[...], wb_ref[:, cs:cs + slab])
            if epilogue is None:
                z_ref[:, cs:cs + slab] = acc
            else:
                epilogue(acc, cs)

    if kind == "a":
        seg = (j * tn) // (A_HEADS * A_HEAD_DIM)
        is_rot = jnp.logical_and(seg < 3 * len(A_GROUPS), seg % 3 < 2)

        def rot_a(acc, cs):
            half = A_ROT // 2
            for o in range(0, MXU_N, A_HEAD_DIM):
                x = acc[:, o:o + A_HEAD_DIM]
                x_hi = pltpu.roll(x, A_HEAD_DIM - half, 1)
                x_lo = pltpu.roll(x, half, 1)
                z_ref[:, cs + o:cs + o + A_HEAD_DIM] = x * c_ref[...] + x_hi * s1_ref[...] + x_lo * s2_ref[...]

        pl.when(is_rot)(lambda: tile(rot_a))
        pl.when(jnp.logical_not(is_rot))(lambda: tile(None))
    elif kind == "b":
        n_qk = B_HEADS * B_DK
        is_rot = (j * tn) < 2 * n_qk
        is_k = (j * tn) >= n_qk

        def rot_b(acc, cs):
            scale = jnp.where(is_k, B_DK ** -0.5, 1.0).astype(F32)
            half = B_DK // 2
            for o in range(0, MXU_N, B_DK):
                x1 = acc[:, o:o + half]
                x2 = acc[:, o + half:o + B_DK]
                z_ref[:, cs + o:cs + o + half] = (x1 * c_ref[...] - x2 * s_ref[...]) * scale
                z_ref[:, cs + o + half:cs + o + B_DK] = (x1 * s_ref[...] + x2 * c_ref[...]) * scale

        pl.when(is_rot)(lambda: tile(rot_b))
        pl.when(jnp.logical_not(is_rot))(lambda: tile(None))
    else:
        tile(None)


def _proj(h, w3, slot, n_out, kind, tabs, tab_blocks, tm, tn):
    m, k = h.shape
    in_specs = [pl.BlockSpec((tm, k), lambda j, i: (i, 0)),
                pl.BlockSpec((None, k, tn), lambda j, i: (slot, 0, j))]
    for _ in tabs:
        in_specs.append(pl.BlockSpec((tm, LANES), lambda j, i: (i % tab_blocks, 0)))
    return pl.pallas_call(
        functools.partial(_proj_body, kind=kind, tn=tn),
        grid=(n_out // tn, m // tm),
        in_specs=in_specs,
        out_specs=pl.BlockSpec((tm, tn), lambda j, i: (i, j)),
        out_shape=jax.ShapeDtypeStruct((m, n_out), F32),
        scratch_shapes=[pltpu.VMEM((k, tn), BF16)],
        compiler_params=_params(("arbitrary", "arbitrary")),
        name="proj_" + kind,
    )(h, w3, *tabs)


def _cast_body(x_ref, o_ref):
    o_ref[...] = x_ref[...].astype(o_ref.dtype)


def _cast_bf16(w3):
    n, k, d = w3.shape
    rows = n * k
    tr = min(rows, 1024)
    out = pl.pallas_call(
        _cast_body,
        grid=(rows // tr,),
        in_specs=[pl.BlockSpec((tr, d), lambda i: (i, 0))],
        out_specs=pl.BlockSpec((tr, d), lambda i: (i, 0)),
        out_shape=jax.ShapeDtypeStruct((rows, d), BF16),
        compiler_params=_params(("arbitrary",)),
        name="cast_bf16",
    )(w3.reshape(rows, d))
    return out.reshape(n, k, d)


def _outproj_body(y_ref, w_ref, x_ref, g_ref, xo_ref, ho_ref):
    xn = x_ref[...] + _nn(y_ref[...], w_ref[...])
    xo_ref[...] = xn
    ms = jnp.mean(xn * xn, axis=-1, keepdims=True)
    ho_ref[...] = (xn * lax.rsqrt(ms + NORM_EPS) * g_ref[...]).astype(ho_ref.dtype)


def _outproj(y, w3, slot, x, g, h_dtype, tm):
    m, kd = y.shape
    d = x.shape[1]
    return pl.pallas_call(
        _outproj_body,
        grid=(m // tm,),
        in_specs=[pl.BlockSpec((tm, kd), lambda i: (i, 0)),
                  pl.BlockSpec((None, kd, d), lambda i: (slot, 0, 0), pipeline_mode=pl.Buffered(1)),
                  pl.BlockSpec((tm, d), lambda i: (i, 0)),
                  pl.BlockSpec((1, d), lambda i: (0, 0))],
        out_specs=[pl.BlockSpec((tm, d), lambda i: (i, 0)),
                   pl.BlockSpec((tm, d), lambda i: (i, 0))],
        out_shape=[jax.ShapeDtypeStruct((m, d), F32),
                   jax.ShapeDtypeStruct((m, d), h_dtype)],
        compiler_params=_params(("arbitrary",)),
        name="outproj",
    )(y, w3, x, g.reshape(1, d))


LOG2_E = 1.4426950408889634
LN_2 = 0.6931471805599453


def _band_attention(q, k, v_ones, bias):
    s = _nt(q, k) + bias
    mx = jnp.max(s, axis=-1, keepdims=True)
    p = jnp.exp2(s - mx)
    o_den = _nn(p.astype(BF16), v_ones)
    den = o_den[:, A_HEAD_DIM:]
    return o_den[:, :A_HEAD_DIM] / den, mx * LN_2 + jnp.log(den)


def _rows(start, size, stride):
    return pl.ds(start, size) if stride == 1 else pl.ds(start, size, stride=stride)


def _attn_prompt_body(*refs, seq):
    qkv_refs, gate_ref, bias_first_ref, bias_band_ref, y_ref, o_s, l_s = refs[:9], *refs[9:]
    q_scale = (A_HEAD_DIM ** -0.5) * LOG2_E
    ones = jnp.ones((A_BAND, A_HEAD_DIM), BF16)
    for g, (_, dil) in enumerate(A_GROUPS):
        q_ref, k_ref, v_ref = qkv_refs[3 * g:3 * g + 3]
        for r in range(dil):
            k_prev = v_prev = None
            for n in range(seq // dil // A_BAND):
                qs = _rows(r + dil * A_BAND * n, A_BAND, dil)
                q = (q_ref[qs, :] * q_scale).astype(BF16)
                k_cur = k_ref[qs, :].astype(BF16)
                v_cur = jnp.concatenate([v_ref[qs, :].astype(BF16), ones], axis=1)
                if n == 0:
                    o, lse = _band_attention(q, k_cur, v_cur, bias_first_ref[...])
                else:
                    o, lse = _band_attention(q, jnp.concatenate([k_prev, k_cur], axis=0),
                                             jnp.concatenate([v_prev, v_cur], axis=0), bias_band_ref[...])
                k_prev, v_prev = k_cur, v_cur
                o_s[g, qs, :] = o
                l_s[g, qs, :] = lse
    for tile in range(seq // A_BAND):
        rs = slice(tile * A_BAND, (tile + 1) * A_BAND)
        l1, l2, l3 = l_s[0, rs, :], l_s[1, rs, :], l_s[2, rs, :]
        mx = jnp.maximum(jnp.maximum(l1, l2), l3)
        e1, e2, e3 = jnp.exp(l1 - mx), jnp.exp(l2 - mx), jnp.exp(l3 - mx)
        o = (e1 * o_s[0, rs, :] + e2 * o_s[1, rs, :] + e3 * o_s[2, rs, :]) / (e1 + e2 + e3)
        y_ref[rs, :] = (o * _silu(gate_ref[rs, :])).astype(y_ref.dtype)


def _attn_prompt(z, batch, seq):
    n_groups = len(A_GROUPS)
    in_specs = []
    for c in range(3 * n_groups + 1):
        in_specs.append(pl.BlockSpec((seq, A_HEAD_DIM), lambda b, h, c=c: (b, c * A_HEADS + h)))
    row = jnp.arange(A_BAND)[:, None]
    col = jnp.arange(2 * A_BAND)[None, :]
    bias_first = jnp.where(col[:, :A_BAND] <= row, 0.0, -jnp.inf).astype(F32)
    bias_band = jnp.where((col >= row) & (col <= row + A_BAND), 0.0, -jnp.inf).astype(F32)
    in_specs.append(pl.BlockSpec((A_BAND, A_BAND), lambda b, h: (0, 0)))
    in_specs.append(pl.BlockSpec((A_BAND, 2 * A_BAND), lambda b, h: (0, 0)))
    return pl.pallas_call(
        functools.partial(_attn_prompt_body, seq=seq),
        grid=(batch, A_HEADS),
        in_specs=in_specs,
        out_specs=pl.BlockSpec((seq, A_HEAD_DIM), lambda b, h: (b, h)),
        out_shape=jax.ShapeDtypeStruct((batch * seq, A_HEADS * A_HEAD_DIM), BF16),
        scratch_shapes=[pltpu.VMEM((n_groups, seq, A_HEAD_DIM), F32),
                        pltpu.VMEM((n_groups, seq, A_HEAD_DIM), F32)],
        compiler_params=_params(("arbitrary", "arbitrary")),
        name="attn_prompt",
    )(*([z] * (3 * n_groups + 1)), bias_first, bias_band)


def _attn_sample_body(z_ref, c1_ref, c2_ref, c3_ref, y_ref):
    t = pl.program_id(1)
    scale = A_HEAD_DIM ** -0.5
    m_idx = lax.broadcasted_iota(jnp.int32, (A_BAND, A_HEADS, 1), 0)
    n_idx = lax.broadcasted_iota(jnp.int32, (SAMPLE_PAD, A_HEADS, 1), 0)
    caches = (c1_ref, c2_ref, c3_ref)
    outs, lses = [], []
    for g, (_, dil) in enumerate(A_GROUPS):
        base = 3 * g * A_HEADS
        q = z_ref[0, t, base:base + A_HEADS, :]
        kc = caches[g][0, :, 0:A_HEADS, :]
        vc = caches[g][0, :, A_HEADS:2 * A_HEADS, :]
        kn = z_ref[0, :, base + A_HEADS:base + 2 * A_HEADS, :]
        vn = z_ref[0, :, base + 2 * A_HEADS:base + 3 * A_HEADS, :]
        s_c = jnp.sum(kc * q[None], axis=-1, keepdims=True) * scale
        s_n = jnp.sum(kn * q[None], axis=-1, keepdims=True) * scale
        if dil == 1:
            s_c = jnp.where(m_idx >= t, s_c, -jnp.inf)
            s_n = jnp.where(n_idx <= t, s_n, -jnp.inf)
        else:
            s_n = jnp.where(n_idx == t, s_n, -jnp.inf)
        mx = jnp.maximum(jnp.max(s_c, axis=0), jnp.max(s_n, axis=0))
        p_c = jnp.exp(s_c - mx[None])
        p_n = jnp.exp(s_n - mx[None])
        den = jnp.sum(p_c, axis=0) + jnp.sum(p_n, axis=0)
        num = jnp.sum(p_c * vc, axis=0) + jnp.sum(p_n * vn, axis=0)
        outs.append(num / den)
        lses.append(mx + jnp.log(den))
    lmax = jnp.maximum(jnp.maximum(lses[0], lses[1]), lses[2])
    es = [jnp.exp(l - lmax) for l in lses]
    tot = es[0] + es[1] + es[2]
    o = (es[0] / tot) * outs[0] + (es[1] / tot) * outs[1] + (es[2] / tot) * outs[2]
    gate = z_ref[0, t, 3 * len(A_GROUPS) * A_HEADS:(3 * len(A_GROUPS) + 1) * A_HEADS, :]
    y_ref[0, 0] = o * _silu(gate)


def _attn_sample(zs, caches, slot, batch, t_real):
    n_in = zs.shape[1]
    z5 = zs.reshape(batch, SAMPLE_PAD, n_in // A_HEAD_DIM, A_HEAD_DIM)
    in_specs = [pl.BlockSpec((1, SAMPLE_PAD, n_in // A_HEAD_DIM, A_HEAD_DIM), lambda b, t: (b, 0, 0, 0))]
    args = [z5]
    for g, (win, dil) in enumerate(A_GROUPS):
        n_a = caches[g].shape[0]
        cv = caches[g].reshape(n_a * batch, win // dil, dil * 2 * A_HEADS, A_HEAD_DIM)
        if dil == 1:
            idx = lambda b, t: (slot * batch + b, 0, 0, 0)
        else:
            idx = lambda b, t: (slot * batch + b, 0, t, 0)
        in_specs.append(pl.BlockSpec((1, A_BAND, 2 * A_HEADS, A_HEAD_DIM), idx))
        args.append(cv)
    y = pl.pallas_call(
        _attn_sample_body,
        grid=(batch, t_real),
        in_specs=in_specs,
        out_specs=pl.BlockSpec((1, 1, A_HEADS, A_HEAD_DIM), lambda b, t: (b, t, 0, 0)),
        out_shape=jax.ShapeDtypeStruct((batch, t_real, A_HEADS, A_HEAD_DIM), F32),
        compiler_params=_params(("arbitrary", "arbitrary")),
        name="attn_sample",
    )(*args)
    return y.reshape(batch, t_real, A_HEADS * A_HEAD_DIM)


def _tn(a, b):
    return lax.dot_general(a, b, (((0,), (0,)), ((), ())), preferred_element_type=F32)


def _retention_body(*refs, has_state):
    if has_state:
        q_ref, k_ref, v_ref, gate_ref, dec_ref, xi_ref, kd_ref, sd_ref, gn_ref, s0_ref, y_ref, st_ref = refs
    else:
        q_ref, k_ref, v_ref, gate_ref, dec_ref, xi_ref, kd_ref, sd_ref, gn_ref, y_ref, st_ref = refs
    c = pl.program_id(1)

    @pl.when(c == 0)
    def _():
        if has_state:
            st_ref[...] = s0_ref[...]
        else:
            st_ref[...] = jnp.zeros_like(st_ref)

    for h in range(B_HEADS):
        ks = slice(h * B_DK, (h + 1) * B_DK)
        vs = slice(h * B_DV, (h + 1) * B_DV)
        qb = q_ref[:, ks].astype(BF16)
        kf = k_ref[:, ks]
        vb = v_ref[:, vs].astype(BF16)
        state = st_ref[0, h]
        scores = _nt(qb, kf.astype(BF16)) * dec_ref[h]
        o = _nn(scores.astype(BF16), vb) + _nn(qb, state.astype(BF16)) * xi_ref[h]
        k_dec = (kf * kd_ref[h]).astype(BF16)
        st_ref[0, h] = sd_ref[h] * state + _tn(k_dec, vb)
        mu = jnp.mean(o, axis=-1, keepdims=True)
        var = jnp.mean(jnp.square(o - mu), axis=-1, keepdims=True)
        yv = (o - mu) * lax.rsqrt(var + NORM_EPS) * gn_ref[h]
        y_ref[:, vs] = (yv * _silu(gate_ref[:, vs])).astype(y_ref.dtype)


def _retention(z, batch, chunk, t_real, gn, state0):
    rows = z.shape[0]
    nc = rows // (batch * chunk)
    lg = jnp.log(1.0 - 2.0 ** (-5.0 - jnp.arange(B_HEADS, dtype=F32)))[:, None, None]
    idx = jnp.arange(chunk, dtype=F32)
    diff = idx[:, None] - idx[None, :]
    dec = jnp.where(diff >= 0, jnp.exp(lg * jnp.maximum(diff, 0.0)), 0.0)
    xi = jnp.exp(lg * (idx[None, :, None] + 1.0))
    live = idx[None, :, None] < t_real
    kd = jnp.where(live, jnp.exp(lg * jnp.where(live, t_real - 1.0 - idx[None, :, None], 0.0)), 0.0)
    sd = jnp.exp(lg * t_real)
    n_k, n_v = B_HEADS * B_DK, B_HEADS * B_DV
    const = lambda b, c: (0, 0, 0)
    in_specs = [pl.BlockSpec((chunk, n_k), lambda b, c: (b * nc + c, 0)),
                pl.BlockSpec((chunk, n_k), lambda b, c: (b * nc + c, 1)),
                pl.BlockSpec((chunk, n_v), lambda b, c: (b * nc + c, 2 * n_k // n_v)),
                pl.BlockSpec((chunk, n_v), lambda b, c: (b * nc + c, 2 * n_k // n_v + 1)),
                pl.BlockSpec((B_HEADS, chunk, chunk), const),
                pl.BlockSpec((B_HEADS, chunk, 1), const),
                pl.BlockSpec((B_HEADS, chunk, 1), const),
                pl.BlockSpec((B_HEADS, 1, 1), const),
                pl.BlockSpec((B_HEADS, 1, B_DV), const)]
    args = [z, z, z, z, dec, xi, kd, sd, gn.reshape(B_HEADS, 1, B_DV)]
    state_spec = pl.BlockSpec((1, B_HEADS, B_DK, B_DV), lambda b, c: (b, 0, 0, 0))
    if state0 is not None:
        in_specs.append(state_spec)
        args.append(state0)
    return pl.pallas_call(
        functools.partial(_retention_body, has_state=state0 is not None),
        grid=(batch, nc),
        in_specs=in_specs,
        out_specs=[pl.BlockSpec((chunk, n_v), lambda b, c: (b * nc + c, 0)), state_spec],
        out_shape=[jax.ShapeDtypeStruct((rows, n_v), BF16),
                   jax.ShapeDtypeStruct((batch, B_HEADS, B_DK, B_DV), F32)],
        compiler_params=_params(("arbitrary", "arbitrary")),
        name="retention",
    )(*args)


def _log_sigmoid(x):
    return jnp.minimum(x, 0.0) - jnp.log(1.0 + jnp.exp(-jnp.abs(x)))


def _gla_head(q, k, v, lr, wg, bg, state, chunk, t_sub):
    nsub = chunk // C_SUB
    log_a = _log_sigmoid(_nn_hi(lr, wg) + bg) * (1.0 / C_TAU)
    ri = lax.broadcasted_iota(jnp.int32, (chunk, chunk), 0)
    ci = lax.broadcasted_iota(jnp.int32, (chunk, chunk), 1)
    tri = jnp.logical_and(ri // C_SUB == ci // C_SUB, ci <= ri).astype(BF16)
    bl = _nn_exact_lhs(tri, log_a)

    def block_rows(vals):
        return jnp.concatenate([jnp.broadcast_to(x, (C_SUB, C_DK)) for x in vals], axis=0)

    tot = [bl[j * C_SUB + t_sub - 1:j * C_SUB + t_sub, :] for j in range(nsub)]
    beta = [jnp.zeros((1, C_DK), F32)]
    for j in range(nsub):
        beta.append(beta[-1] + tot[j])
    live = lax.broadcasted_iota(jnp.int32, (chunk, 1), 0) % C_SUB < t_sub
    qs = q * (C_DK ** -0.5)
    q_in = qs * jnp.exp(bl)
    k_out = jnp.where(live, k * jnp.exp(jnp.minimum(block_rows(tot) - bl, 0.0)), 0.0)
    vb = v.astype(BF16)

    o = _nn((q_in * block_rows([jnp.exp(x) for x in beta[:nsub]])).astype(BF16), state.astype(BF16))

    pj = lax.broadcasted_iota(jnp.int32, (C_SUB * C_SUB, 1), 0)
    pair_live = (pj // C_SUB >= pj % C_SUB).astype(F32)
    pc = lax.broadcasted_iota(jnp.int32, (C_SUB * C_SUB, chunk), 1)
    ps = lax.broadcasted_iota(jnp.int32, (C_SUB * C_SUB, chunk), 0) % C_SUB
    et = lax.broadcasted_iota(jnp.int32, (C_SUB, C_SUB * C_SUB), 0)
    ej = lax.broadcasted_iota(jnp.int32, (C_SUB, C_SUB * C_SUB), 1)
    pick_t = (ej // C_SUB == et).astype(BF16)

    def rep_rows(x):
        return jnp.concatenate([jnp.broadcast_to(x[t:t + 1, :], (C_SUB, C_DK)) for t in range(C_SUB)], axis=0)

    def tile_rows(x):
        return jnp.concatenate([x] * C_SUB, axis=0)

    a_rows = []
    for i in range(nsub):
        r = slice(i * C_SUB, (i + 1) * C_SUB)
        pair = rep_rows(qs[r]) * tile_rows(k[r]) * jnp.exp(jnp.minimum(rep_rows(bl[r]) - tile_rows(bl[r]), 0.0))
        att = jnp.sum(pair, axis=1, keepdims=True) * pair_live
        placed = jnp.where(pc == ps + i * C_SUB, att, 0.0).astype(BF16)
        a_i = _nn(pick_t, placed)
        if i > 0:
            between = [jnp.broadcast_to(jnp.exp(beta[i] - beta[j + 1]), (C_SUB, C_DK)) for j in range(i)]
            between.append(jnp.zeros(((nsub - i) * C_SUB, C_DK), F32))
            a_i += _nt(q_in[r].astype(BF16), (k_out * jnp.concatenate(between, axis=0)).astype(BF16))
        a_rows.append(a_i)
    a = a_rows[0] if nsub == 1 else jnp.concatenate(a_rows, axis=0)
    o += _nn(a.astype(BF16), vb)

    k_end = k_out * block_rows([jnp.exp(beta[nsub] - beta[j + 1]) for j in range(nsub)])
    d_col = jnp.broadcast_to(jnp.exp(beta[nsub]), (8, C_DK)).T[:, 0:1]
    return o, d_col * state + _tn(k_end.astype(BF16), vb)


def _gla_body(*refs, has_state, chunk, t_sub):
    if has_state:
        q_ref, k_ref, v_ref, gate_ref, lr_ref, wg_ref, bg_ref, gn_ref, s0_ref, y_ref, st_ref = refs
    else:
        q_ref, k_ref, v_ref, gate_ref, lr_ref, wg_ref, bg_ref, gn_ref, y_ref, st_ref = refs
    c = pl.program_id(1)

    @pl.when(c == 0)
    def _():
        if has_state:
            st_ref[...] = s0_ref[...]
        else:
            st_ref[...] = jnp.zeros_like(st_ref)

    lr = lr_ref[:, 0:C_RANK]
    for h in range(C_HEADS):
        ks = slice(h * C_DK, (h + 1) * C_DK)
        vs = slice(h * C_DV, (h + 1) * C_DV)
        o, st_ref[0, h] = _gla_head(q_ref[:, ks], k_ref[:, ks], v_ref[:, vs], lr, wg_ref[:, ks], bg_ref[:, ks],
                                    st_ref[0, h], chunk, t_sub)
        yv = o * lax.rsqrt(jnp.mean(o * o, axis=-1, keepdims=True) + NORM_EPS) * gn_ref[:, vs]
        y_ref[:, vs] = (yv * _silu(gate_ref[:, vs])).astype(y_ref.dtype)


def _gla(z, z_lr, batch, chunk, t_real, wg3, bg3, gn3, slot, state0):
    rows = z.shape[0]
    nc = rows // (batch * chunk)
    t_sub = min(t_real, C_SUB)
    n_k, n_v = C_HEADS * C_DK, C_HEADS * C_DV
    n_slots = wg3.shape[0]
    in_specs = [pl.BlockSpec((chunk, n_k), lambda b, c: (b * nc + c, 0)),
                pl.BlockSpec((chunk, n_k), lambda b, c: (b * nc + c, 1)),
                pl.BlockSpec((chunk, n_v), lambda b, c: (b * nc + c, 2 * n_k // n_v)),
                pl.BlockSpec((chunk, n_v), lambda b, c: (b * nc + c, 2 * n_k // n_v + 1)),
                pl.BlockSpec((chunk, LANES), lambda b, c: (b * nc + c, 0)),
                pl.BlockSpec((None, C_RANK, n_k), lambda b, c: (slot, 0, 0)),
                pl.BlockSpec((None, 1, n_k), lambda b, c: (slot, 0, 0)),
                pl.BlockSpec((None, 1, n_v), lambda b, c: (slot, 0, 0))]
    args = [z, z, z, z, z_lr, wg3, bg3.reshape(n_slots, 1, n_k), gn3.reshape(n_slots, 1, n_v)]
    state_spec = pl.BlockSpec((1, C_HEADS, C_DK, C_DV), lambda b, c: (b, 0, 0, 0))
    if state0 is not None:
        in_specs.append(state_spec)
        args.append(state0)
    return pl.pallas_call(
        functools.partial(_gla_body, has_state=state0 is not None, chunk=chunk, t_sub=t_sub),
        grid=(batch, nc),
        in_specs=in_specs,
        out_specs=[pl.BlockSpec((chunk, n_v), lambda b, c: (b * nc + c, 0)), state_spec],
        out_shape=[jax.ShapeDtypeStruct((rows, n_v), BF16),
                   jax.ShapeDtypeStruct((batch, C_HEADS, C_DK, C_DV), F32)],
        compiler_params=_params(("arbitrary", "arbitrary")),
        name="gla",
    )(*args)


def _rot_tables_a(pos):
    half = A_ROT // 2
    inv_freq = A_ROPE_THETA ** (-jnp.arange(half, dtype=F32) / half)
    ang = pos.astype(F32)[:, None] * inv_freq[None, :]
    cos, sin = jnp.cos(ang), jnp.sin(ang)
    n = pos.shape[0]
    rest = A_HEAD_DIM - A_ROT
    c = jnp.concatenate([cos, cos, jnp.ones((n, rest), F32)], axis=1)
    s1 = jnp.concatenate([-sin, jnp.zeros((n, half + rest), F32)], axis=1)
    s2 = jnp.concatenate([jnp.zeros((n, half), F32), sin, jnp.zeros((n, rest), F32)], axis=1)
    return c, s1, s2


def _rot_tables_b(pos):
    half = B_DK // 2
    inv_freq = B_ROPE_THETA ** (-jnp.arange(half, dtype=F32) / half)
    ang = pos.astype(F32)[:, None] * inv_freq[None, :]
    return jnp.cos(ang), jnp.sin(ang)


def kernel(x_prompt, x_sample, cache_a_kv1, cache_a_kv2, cache_a_kv3, state_b, state_c, norm_g, final_g,
           w_in_a, w_out_a, w_in_b, gn_b, w_out_b, w_in_c, w_gate2_c, b_gate_c, gn_c, w_out_c):
    bp, seq, d = x_prompt.shape
    bs, t_real, _ = x_sample.shape
    caches_a = (cache_a_kv1, cache_a_kv2, cache_a_kv3)
    width = A_HEADS * A_HEAD_DIM

    xp = x_prompt.reshape(bp * seq, d)
    xs = jnp.pad(x_sample, ((0, 0), (0, SAMPLE_PAD - t_real), (0, 0))).reshape(bs * SAMPLE_PAD, d)
    ms = xs.shape[0]
    tm_p = 1024

    pos_p = jnp.arange(seq)
    pos_s = jnp.tile(PAST_LEN + jnp.arange(SAMPLE_PAD), bs)
    tabs_a_p, tabs_a_s = _rot_tables_a(pos_p), _rot_tables_a(pos_s)
    tabs_b_p, tabs_b_s = _rot_tables_b(pos_p), _rot_tables_b(pos_s)

    hp = _rmsnorm(xp, norm_g[0], BF16)
    hs = _rmsnorm(xs, norm_g[0], BF16)
    w_out_a, w_out_b, w_out_c = _cast_bf16(w_out_a), _cast_bf16(w_out_b), _cast_bf16(w_out_c)

    a_rows_p, a_rows_s = [], []
    b_states_p, b_states_s, c_states_p, c_states_s = [], [], [], []
    n_layers = len(LAYER_KINDS)
    for i in range(n_layers):
        kind, slot = LAYER_KINDS[i], LAYER_SLOTS[i]
        last = i == n_layers - 1
        g_next = final_g if last else norm_g[i + 1]
        h_dtype = F32 if last else BF16
        if kind == 0:
            n_in = w_in_a.shape[2]
            zp = _proj(hp, w_in_a, slot, n_in, "a", tabs_a_p, seq // tm_p, tm_p, 1024)
            zs = _proj(hs, w_in_a, slot, n_in, "a", tabs_a_s, 1, ms, 1024)
            yp = _attn_prompt(zp, bp, seq)
            ys = _attn_sample(zs, caches_a, slot, bs, t_real)
            ys = jnp.pad(ys, ((0, 0), (0, SAMPLE_PAD - t_real), (0, 0))).reshape(ms, width).astype(BF16)
            zp3 = zp.reshape(bp, seq, n_in)
            zs3 = zs.reshape(bs, SAMPLE_PAD, n_in)
            rows_p, rows_s = [], []
            for g, (win, dil) in enumerate(A_GROUPS):
                keep = min(win, seq)
                lo = (3 * g + 1) * width
                rows_p.append(zp3[:, seq - keep:, lo:lo + 2 * width].reshape(bp, keep, 2, A_HEADS, A_HEAD_DIM))
                rows_s.append(zs3[:, :t_real, lo:lo + 2 * width].reshape(bs, t_real, 2, A_HEADS, A_HEAD_DIM))
            a_rows_p.append(rows_p)
            a_rows_s.append(rows_s)
            w_out = w_out_a
        elif kind == 1:
            n_in = w_in_b.shape[2]
            zp = _proj(hp, w_in_b, slot, n_in, "b", tabs_b_p, seq // tm_p, tm_p, 1024)
            zs = _proj(hs, w_in_b, slot, n_in, "b", tabs_b_s, 1, ms, 1024)
            yp, st_p = _retention(zp, bp, min(B_CHUNK, seq), min(B_CHUNK, seq), gn_b[slot], None)
            ys, st_s = _retention(zs, bs, SAMPLE_PAD, t_real, gn_b[slot], state_b[slot])
            b_states_p.append(st_p)
            b_states_s.append(st_s)
            w_out = w_out_b
        else:
            n_main = 2 * C_HEADS * C_DK + 2 * C_HEADS * C_DV
            w_lr = jnp.pad(w_in_c[slot][:, n_main:], ((0, 0), (0, LANES - C_RANK)))[None]
            zp = _proj(hp, w_in_c, slot, n_main, "c", (), 1, tm_p, 1024)
            zs = _proj(hs, w_in_c, slot, n_main, "c", (), 1, ms, 1024)
            zp_lr = _proj(hp, w_lr, 0, LANES, "c", (), 1, tm_p, LANES)
            zs_lr = _proj(hs, w_lr, 0, LANES, "c", (), 1, ms, LANES)
            yp, st_p = _gla(zp, zp_lr, bp, min(C_CHUNK, seq), min(C_CHUNK, seq),
                            w_gate2_c, b_gate_c, gn_c, slot, None)
            ys, st_s = _gla(zs, zs_lr, bs, SAMPLE_PAD, t_real, w_gate2_c, b_gate_c, gn_c, slot, state_c[slot])
            c_states_p.append(st_p)
            c_states_s.append(st_s)
            w_out = w_out_c
        xp, hp = _outproj(yp, w_out, slot, xp, g_next, h_dtype, 512)
        xs, hs = _outproj(ys, w_out, slot, xs, g_next, h_dtype, ms)

    y_prompt = hp.reshape(bp, seq, d)
    y_sample = hs.reshape(bs, SAMPLE_PAD, d)[:, :t_real]
    stack = lambda rows, g: jnp.stack([r[g] for r in rows])
    return (y_prompt, y_sample,
            stack(a_rows_p, 0), stack(a_rows_p, 1), stack(a_rows_p, 2),
            jnp.stack(b_states_p), jnp.stack(c_states_p),
            stack(a_rows_s, 0), stack(a_rows_s, 1), stack(a_rows_s, 2),
            jnp.stack(b_states_s), jnp.stack(c_states_s))
```

```python
import functools

import jax
import jax.numpy as jnp
from jax import lax
from jax.experimental import pallas as pl
from jax.experimental.pallas import tpu as pltpu

F32 = jnp.float32
BF16 = jnp.bfloat16

PAST_LEN = 16384
NORM_EPS = 1e-6
A_GROUPS = ((128, 1), (512, 4), (2048, 16))
A_HEADS = 16
A_HEAD_DIM = 128
A_ROT = A_HEAD_DIM // 4
A_ROPE_THETA = 500000.0
A_BAND = 128
B_HEADS = 8
B_DK = 256
B_DV = 512
B_CHUNK = 128
B_ROPE_THETA = 10000.0
C_HEADS = 4
C_DK = 256
C_DV = 512
C_RANK = 16
C_TAU = 16.0
C_CHUNK = 64
C_SUB = 8
LAYER_KINDS = (0, 1, 2, 0)
LAYER_SLOTS = (0, 0, 0, 1)

SAMPLE_PAD = 16
LANES = 128
MXU_N = 256
VMEM_LIMIT = 56 * 1024 * 1024


def _params(sem, vmem=VMEM_LIMIT):
    return pltpu.CompilerParams(dimension_semantics=sem, vmem_limit_bytes=vmem)


def _silu(g):
    return g * (1.0 / (1.0 + jnp.exp(-g)))


def _nt(a, b):
    return lax.dot_general(a, b, (((1,), (1,)), ((), ())), preferred_element_type=F32)


def _nn(a, b):
    return jnp.dot(a, b, preferred_element_type=F32)


def _split(x):
    hi = x.astype(BF16)
    lo = (x - hi.astype(F32)).astype(BF16)
    return hi, lo


def _split3(x):
    hi = x.astype(BF16)
    r = x - hi.astype(F32)
    mid = r.astype(BF16)
    lo = (r - mid.astype(F32)).astype(BF16)
    return hi, mid, lo


def _nn_exact_lhs(sel, x):
    hi, mid, lo = _split3(x)
    return _nn(sel, hi) + _nn(sel, mid) + _nn(sel, lo)


def _nn_hi(a, b):
    ah, al = _split(a)
    bh, bl = _split(b)
    return _nn(ah, bh) + _nn(ah, bl) + _nn(al, bh)


def _rmsnorm_body(x_ref, g_ref, o_ref):
    x = x_ref[...]
    ms = jnp.mean(x * x, axis=-1, keepdims=True)
    o_ref[...] = (x * lax.rsqrt(ms + NORM_EPS) * g_ref[...]).astype(o_ref.dtype)


def _rmsnorm(x, g, out_dtype):
    m, d = x.shape
    tm = min(m, 512)
    return pl.pallas_call(
        _rmsnorm_body,
        grid=(m // tm,),
        in_specs=[pl.BlockSpec((tm, d), lambda i: (i, 0)),
                  pl.BlockSpec((1, d), lambda i: (0, 0))],
        out_specs=pl.BlockSpec((tm, d), lambda i: (i, 0)),
        out_shape=jax.ShapeDtypeStruct((m, d), out_dtype),
        compiler_params=_params(("arbitrary",)),
        name="rmsnorm",
    )(x, g.reshape(1, d))


N_ROT_TABLES = {"a": 3, "b": 2, "c": 0}


def _rot_a(acc, cs, z_ref, tabs, j, tn):
    c_ref, s1_ref, s2_ref = tabs
    half = A_ROT // 2
    for o in range(0, acc.shape[1], A_HEAD_DIM):
        x = acc[:, o:o + A_HEAD_DIM]
        x_hi = pltpu.roll(x, A_HEAD_DIM - half, 1)
        x_lo = pltpu.roll(x, half, 1)
        z_ref[:, cs + o:cs + o + A_HEAD_DIM] = x * c_ref[...] + x_hi * s1_ref[...] + x_lo * s2_ref[...]


def _rot_b(acc, cs, z_ref, tabs, j, tn):
    c_ref, s_ref = tabs
    scale = jnp.where(j * tn >= B_HEADS * B_DK, B_DK ** -0.5, 1.0).astype(F32)
    half = B_DK // 2
    for o in range(0, acc.shape[1], B_DK):
        x1 = acc[:, o:o + half]
        x2 = acc[:, o + half:o + B_DK]
        z_ref[:, cs + o:cs + o + half] = (x1 * c_ref[...] - x2 * s_ref[...]) * scale
        z_ref[:, cs + o + half:cs + o + B_DK] = (x1 * s_ref[...] + x2 * c_ref[...]) * scale


def _proj_body(*refs, kind, tn):
    n_tab = N_ROT_TABLES[kind]
    h_ref, hs_ref, w_ref = refs[:3]
    tabs, tabs_s = refs[3:3 + n_tab], refs[3 + n_tab:3 + 2 * n_tab]
    z_ref, zs_ref, wb_ref = refs[3 + 2 * n_tab:]
    j = pl.program_id(0)
    i = pl.program_id(1)
    if kind == "a":
        seg = (j * tn) // (A_HEADS * A_HEAD_DIM)
        is_rot, epilogue = jnp.logical_and(seg < 3 * len(A_GROUPS), seg % 3 < 2), _rot_a
    elif kind == "b":
        is_rot, epilogue = (j * tn) < 2 * B_HEADS * B_DK, _rot_b
    else:
        is_rot, epilogue = None, None

    def tile(x_ref, o_ref, tables, rot):
        slab = min(tn, MXU_N)
        for cs in range(0, tn, slab):
            acc = _nn(x_ref[...], wb_ref[:, cs:cs + slab])
            if rot:
                epilogue(acc, cs, o_ref, tables, j, tn)
            else:
                o_ref[:, cs:cs + slab] = acc

    def run(x_ref, o_ref, tables):
        if epilogue is None:
            tile(x_ref, o_ref, tables, False)
        else:
            pl.when(is_rot)(lambda: tile(x_ref, o_ref, tables, True))
            pl.when(jnp.logical_not(is_rot))(lambda: tile(x_ref, o_ref, tables, False))

    @pl.when(i == 0)
    def _():
        wb_ref[...] = w_ref[...].astype(BF16)
        run(hs_ref, zs_ref, tabs_s)

    run(h_ref, z_ref, tabs)


def _proj(h, hs, w3, slot, n_out, kind, tabs, tabs_s, tab_blocks, tm, tn):
    m, k = h.shape
    ms = hs.shape[0]
    in_specs = [pl.BlockSpec((tm, k), lambda j, i: (i, 0)),
                pl.BlockSpec((ms, k), lambda j, i: (0, 0)),
                pl.BlockSpec((None, k, tn), lambda j, i: (slot, 0, j))]
    in_specs += [pl.BlockSpec((tm, LANES), lambda j, i: (i % tab_blocks, 0)) for _ in tabs]
    in_specs += [pl.BlockSpec((ms, LANES), lambda j, i: (0, 0)) for _ in tabs_s]
    return pl.pallas_call(
        functools.partial(_proj_body, kind=kind, tn=tn),
        grid=(n_out // tn, m // tm),
        in_specs=in_specs,
        out_specs=[pl.BlockSpec((tm, tn), lambda j, i: (i, j)),
                   pl.BlockSpec((ms, tn), lambda j, i: (0, j))],
        out_shape=[jax.ShapeDtypeStruct((m, n_out), F32),
                   jax.ShapeDtypeStruct((ms, n_out), F32)],
        scratch_shapes=[pltpu.VMEM((k, tn), BF16)],
        compiler_params=_params(("arbitrary", "arbitrary")),
        name="proj_" + kind,
    )(h, hs, w3, *tabs, *tabs_s)


def _cast_body(x_ref, o_ref):
    o_ref[...] = x_ref[...].astype(o_ref.dtype)


def _cast_bf16(w3):
    n, k, d = w3.shape
    rows = n * k
    tr = min(rows, 1024)
    out = pl.pallas_call(
        _cast_body,
        grid=(rows // tr,),
        in_specs=[pl.BlockSpec((tr, d), lambda i: (i, 0))],
        out_specs=pl.BlockSpec((tr, d), lambda i: (i, 0)),
        out_shape=jax.ShapeDtypeStruct((rows, d), BF16),
        compiler_params=_params(("arbitrary",)),
        name="cast_bf16",
    )(w3.reshape(rows, d))
    return out.reshape(n, k, d)


def _outproj_body(y_ref, w_ref, x_ref, g_ref, xo_ref, ho_ref):
    xn = x_ref[...] + _nn(y_ref[...], w_ref[...])
    xo_ref[...] = xn
    ms = jnp.mean(xn * xn, axis=-1, keepdims=True)
    ho_ref[...] = (xn * lax.rsqrt(ms + NORM_EPS) * g_ref[...]).astype(ho_ref.dtype)


def _outproj(y, w3, slot, x, g, h_dtype, tm):
    m, kd = y.shape
    d = x.shape[1]
    return pl.pallas_call(
        _outproj_body,
        grid=(m // tm,),
        in_specs=[pl.BlockSpec((tm, kd), lambda i: (i, 0)),
                  pl.BlockSpec((None, kd, d), lambda i: (slot, 0, 0), pipeline_mode=pl.Buffered(1)),
                  pl.BlockSpec((tm, d), lambda i: (i, 0)),
                  pl.BlockSpec((1, d), lambda i: (0, 0))],
        out_specs=[pl.BlockSpec((tm, d), lambda i: (i, 0)),
                   pl.BlockSpec((tm, d), lambda i: (i, 0))],
        out_shape=[jax.ShapeDtypeStruct((m, d), F32),
                   jax.ShapeDtypeStruct((m, d), h_dtype)],
        compiler_params=_params(("arbitrary",)),
        name="outproj",
    )(y, w3, x, g.reshape(1, d))


LOG2_E = 1.4426950408889634
LN_2 = 0.6931471805599453


def _band_attention(q, k, v_ones, bias):
    s = _nt(q, k) + bias
    mx = jnp.max(s, axis=-1, keepdims=True)
    p = jnp.exp2(s - mx)
    o_den = _nn(p.astype(BF16), v_ones)
    den = o_den[:, A_HEAD_DIM:]
    return o_den[:, :A_HEAD_DIM] / den, mx * LN_2 + jnp.log(den)


def _rows(start, size, stride):
    return pl.ds(start, size) if stride == 1 else pl.ds(start, size, stride=stride)


def _attn_prompt_body(*refs, seq):
    qkv_refs, gate_ref, bias_first_ref, bias_band_ref, y_ref, o_s, l_s = refs[:9], *refs[9:]
    q_scale = (A_HEAD_DIM ** -0.5) * LOG2_E
    ones = jnp.ones((A_BAND, A_HEAD_DIM), BF16)
    for g, (_, dil) in enumerate(A_GROUPS):
        q_ref, k_ref, v_ref = qkv_refs[3 * g:3 * g + 3]
        for r in range(dil):
            k_prev = v_prev = None
            for n in range(seq // dil // A_BAND):
                qs = _rows(r + dil * A_BAND * n, A_BAND, dil)
                q = (q_ref[qs, :] * q_scale).astype(BF16)
                k_cur = k_ref[qs, :].astype(BF16)
                v_cur = jnp.concatenate([v_ref[qs, :].astype(BF16), ones], axis=1)
                if n == 0:
                    o, lse = _band_attention(q, k_cur, v_cur, bias_first_ref[...])
                else:
                    o, lse = _band_attention(q, jnp.concatenate([k_prev, k_cur], axis=0),
                                             jnp.concatenate([v_prev, v_cur], axis=0), bias_band_ref[...])
                k_prev, v_prev = k_cur, v_cur
                o_s[g, qs, :] = o
                l_s[g, qs, :] = lse
    for tile in range(seq // A_BAND):
        rs = slice(tile * A_BAND, (tile + 1) * A_BAND)
        l1, l2, l3 = l_s[0, rs, :], l_s[1, rs, :], l_s[2, rs, :]
        mx = jnp.maximum(jnp.maximum(l1, l2), l3)
        e1, e2, e3 = jnp.exp(l1 - mx), jnp.exp(l2 - mx), jnp.exp(l3 - mx)
        o = (e1 * o_s[0, rs, :] + e2 * o_s[1, rs, :] + e3 * o_s[2, rs, :]) / (e1 + e2 + e3)
        y_ref[rs, :] = (o * _silu(gate_ref[rs, :])).astype(y_ref.dtype)


def _attn_prompt(z, batch, seq):
    n_groups = len(A_GROUPS)
    in_specs = []
    for c in range(3 * n_groups + 1):
        in_specs.append(pl.BlockSpec((seq, A_HEAD_DIM), lambda b, h, c=c: (b, c * A_HEADS + h)))
    row = jnp.arange(A_BAND)[:, None]
    col = jnp.arange(2 * A_BAND)[None, :]
    bias_first = jnp.where(col[:, :A_BAND] <= row, 0.0, -jnp.inf).astype(F32)
    bias_band = jnp.where((col >= row) & (col <= row + A_BAND), 0.0, -jnp.inf).astype(F32)
    in_specs.append(pl.BlockSpec((A_BAND, A_BAND), lambda b, h: (0, 0)))
    in_specs.append(pl.BlockSpec((A_BAND, 2 * A_BAND), lambda b, h: (0, 0)))
    return pl.pallas_call(
        functools.partial(_attn_prompt_body, seq=seq),
        grid=(batch, A_HEADS),
        in_specs=in_specs,
        out_specs=pl.BlockSpec((seq, A_HEAD_DIM), lambda b, h: (b, h)),
        out_shape=jax.ShapeDtypeStruct((batch * seq, A_HEADS * A_HEAD_DIM), BF16),
        scratch_shapes=[pltpu.VMEM((n_groups, seq, A_HEAD_DIM), F32),
                        pltpu.VMEM((n_groups, seq, A_HEAD_DIM), F32)],
        compiler_params=_params(("arbitrary", "arbitrary")),
        name="attn_prompt",
    )(*([z] * (3 * n_groups + 1)), bias_first, bias_band)


def _attn_sample_body(z_ref, c1_ref, c2_ref, c3_ref, y_ref):
    t = pl.program_id(1)
    scale = A_HEAD_DIM ** -0.5
    m_idx = lax.broadcasted_iota(jnp.int32, (A_BAND, A_HEADS, 1), 0)
    n_idx = lax.broadcasted_iota(jnp.int32, (SAMPLE_PAD, A_HEADS, 1), 0)
    caches = (c1_ref, c2_ref, c3_ref)
    outs, lses = [], []
    for g, (_, dil) in enumerate(A_GROUPS):
        base = 3 * g * A_HEADS
        q = z_ref[0, t, base:base + A_HEADS, :]
        kc = caches[g][0, :, 0:A_HEADS, :]
        vc = caches[g][0, :, A_HEADS:2 * A_HEADS, :]
        kn = z_ref[0, :, base + A_HEADS:base + 2 * A_HEADS, :]
        vn = z_ref[0, :, base + 2 * A_HEADS:base + 3 * A_HEADS, :]
        s_c = jnp.sum(kc * q[None], axis=-1, keepdims=True) * scale
        s_n = jnp.sum(kn * q[None], axis=-1, keepdims=True) * scale
        if dil == 1:
            s_c = jnp.where(m_idx >= t, s_c, -jnp.inf)
            s_n = jnp.where(n_idx <= t, s_n, -jnp.inf)
        else:
            s_n = jnp.where(n_idx == t, s_n, -jnp.inf)
        mx = jnp.maximum(jnp.max(s_c, axis=0), jnp.max(s_n, axis=0))
        p_c = jnp.exp(s_c - mx[None])
        p_n = jnp.exp(s_n - mx[None])
        den = jnp.sum(p_c, axis=0) + jnp.sum(p_n, axis=0)
        num = jnp.sum(p_c * vc, axis=0) + jnp.sum(p_n * vn, axis=0)
        outs.append(num / den)
        lses.append(mx + jnp.log(den))
    lmax = jnp.maximum(jnp.maximum(lses[0], lses[1]), lses[2])
    es = [jnp.exp(l - lmax) for l in lses]
    tot = es[0] + es[1] + es[2]
    o = (es[0] / tot) * outs[0] + (es[1] / tot) * outs[1] + (es[2] / tot) * outs[2]
    gate = z_ref[0, t, 3 * len(A_GROUPS) * A_HEADS:(3 * len(A_GROUPS) + 1) * A_HEADS, :]
    y_ref[0, 0] = o * _silu(gate)


def _attn_sample(zs, caches, slot, batch, t_real):
    n_in = zs.shape[1]
    z5 = zs.reshape(batch, SAMPLE_PAD, n_in // A_HEAD_DIM, A_HEAD_DIM)
    in_specs = [pl.BlockSpec((1, SAMPLE_PAD, n_in // A_HEAD_DIM, A_HEAD_DIM), lambda b, t: (b, 0, 0, 0))]
    args = [z5]
    for g, (win, dil) in enumerate(A_GROUPS):
        n_a = caches[g].shape[0]
        cv = caches[g].reshape(n_a * batch, win // dil, dil * 2 * A_HEADS, A_HEAD_DIM)
        if dil == 1:
            idx = lambda b, t: (slot * batch + b, 0, 0, 0)
        else:
            idx = lambda b, t: (slot * batch + b, 0, t, 0)
        in_specs.append(pl.BlockSpec((1, A_BAND, 2 * A_HEADS, A_HEAD_DIM), idx))
        args.append(cv)
    y = pl.pallas_call(
        _attn_sample_body,
        grid=(batch, t_real),
        in_specs=in_specs,
        out_specs=pl.BlockSpec((1, 1, A_HEADS, A_HEAD_DIM), lambda b, t: (b, t, 0, 0)),
        out_shape=jax.ShapeDtypeStruct((batch, t_real, A_HEADS, A_HEAD_DIM), F32),
        compiler_params=_params(("arbitrary", "arbitrary")),
        name="attn_sample",
    )(*args)
    return y.reshape(batch, t_real, A_HEADS * A_HEAD_DIM)


def _kv_rows_body(*refs, n_layers):
    out_ref = refs[-1]
    layer = pl.program_id(0)
    for l in range(n_layers):
        k_ref, v_ref = refs[2 * l], refs[2 * l + 1]

        @pl.when(layer == l)
        def _():
            for c in range(A_HEADS):
                sl = slice(c * A_HEAD_DIM, (c + 1) * A_HEAD_DIM)
                out_ref[:, c, :] = k_ref[:, sl]
                out_ref[:, A_HEADS + c, :] = v_ref[:, sl]


def _kv_rows(zs, g, batch, seq, keep):
    n_layers = len(zs)
    width = A_HEADS * A_HEAD_DIM
    tt = min(keep, 256)
    first = (seq - keep) // tt
    in_specs, args = [], []
    for l in range(n_layers):
        for c in (1, 2):
            def idx(s, b, i, l=l, c=c):
                return (jnp.where(s == l, b * (seq // tt) + first + i, 0), 3 * g + c)
            in_specs.append(pl.BlockSpec((tt, width), idx))
            args.append(zs[l])
    out = pl.pallas_call(
        functools.partial(_kv_rows_body, n_layers=n_layers),
        grid=(n_layers, batch, keep // tt),
        in_specs=in_specs,
        out_specs=pl.BlockSpec((None, None, tt, 2 * A_HEADS, A_HEAD_DIM), lambda s, b, i: (s, b, i, 0, 0)),
        out_shape=jax.ShapeDtypeStruct((n_layers, batch, keep, 2 * A_HEADS, A_HEAD_DIM), F32),
        compiler_params=_params(("arbitrary", "arbitrary", "arbitrary")),
        name="kv_rows_g%d" % g,
    )(*args)
    return out.reshape(n_layers, batch, keep, 2, A_HEADS, A_HEAD_DIM)


def _tn(a, b):
    return lax.dot_general(a, b, (((0,), (0,)), ((), ())), preferred_element_type=F32)


def _retention_body(*refs, has_state):
    if has_state:
        q_ref, k_ref, v_ref, gate_ref, dec_ref, xi_ref, kd_ref, sd_ref, gn_ref, s0_ref, y_ref, st_ref = refs
    else:
        q_ref, k_ref, v_ref, gate_ref, dec_ref, xi_ref, kd_ref, sd_ref, gn_ref, y_ref, st_ref = refs
    c = pl.program_id(1)

    @pl.when(c == 0)
    def _():
        if has_state:
            st_ref[...] = s0_ref[...]
        else:
            st_ref[...] = jnp.zeros_like(st_ref)

    for h in range(B_HEADS):
        ks = slice(h * B_DK, (h + 1) * B_DK)
        vs = slice(h * B_DV, (h + 1) * B_DV)
        qb = q_ref[:, ks].astype(BF16)
        kf = k_ref[:, ks]
        vb = v_ref[:, vs].astype(BF16)
        state = st_ref[0, h]
        scores = _nt(qb, kf.astype(BF16)) * dec_ref[h]
        o = _nn(scores.astype(BF16), vb) + _nn(qb, state.astype(BF16)) * xi_ref[h]
        k_dec = (kf * kd_ref[h]).astype(BF16)
        st_ref[0, h] = sd_ref[h] * state + _tn(k_dec, vb)
        mu = jnp.mean(o, axis=-1, keepdims=True)
        var = jnp.mean(jnp.square(o - mu), axis=-1, keepdims=True)
        yv = (o - mu) * lax.rsqrt(var + NORM_EPS) * gn_ref[h]
        y_ref[:, vs] = (yv * _silu(gate_ref[:, vs])).astype(y_ref.dtype)


def _retention(z, batch, chunk, t_real, gn, state0):
    rows = z.shape[0]
    nc = rows // (batch * chunk)
    lg = jnp.log(1.0 - 2.0 ** (-5.0 - jnp.arange(B_HEADS, dtype=F32)))[:, None, None]
    idx = jnp.arange(chunk, dtype=F32)
    diff = idx[:, None] - idx[None, :]
    dec = jnp.where(diff >= 0, jnp.exp(lg * jnp.maximum(diff, 0.0)), 0.0)
    xi = jnp.exp(lg * (idx[None, :, None] + 1.0))
    live = idx[None, :, None] < t_real
    kd = jnp.where(live, jnp.exp(lg * jnp.where(live, t_real - 1.0 - idx[None, :, None], 0.0)), 0.0)
    sd = jnp.exp(lg * t_real)
    n_k, n_v = B_HEADS * B_DK, B_HEADS * B_DV
    const = lambda b, c: (0, 0, 0)
    in_specs = [pl.BlockSpec((chunk, n_k), lambda b, c: (b * nc + c, 0)),
                pl.BlockSpec((chunk, n_k), lambda b, c: (b * nc + c, 1)),
                pl.BlockSpec((chunk, n_v), lambda b, c: (b * nc + c, 2 * n_k // n_v)),
                pl.BlockSpec((chunk, n_v), lambda b, c: (b * nc + c, 2 * n_k // n_v + 1)),
                pl.BlockSpec((B_HEADS, chunk, chunk), const),
                pl.BlockSpec((B_HEADS, chunk, 1), const),
                pl.BlockSpec((B_HEADS, chunk, 1), const),
                pl.BlockSpec((B_HEADS, 1, 1), const),
                pl.BlockSpec((B_HEADS, 1, B_DV), const)]
    args = [z, z, z, z, dec, xi, kd, sd, gn.reshape(B_HEADS, 1, B_DV)]
    state_spec = pl.BlockSpec((1, B_HEADS, B_DK, B_DV), lambda b, c: (b, 0, 0, 0))
    if state0 is not None:
        in_specs.append(state_spec)
        args.append(state0)
    return pl.pallas_call(
        functools.partial(_retention_body, has_state=state0 is not None),
        grid=(batch, nc),
        in_specs=in_specs,
        out_specs=[pl.BlockSpec((chunk, n_v), lambda b, c: (b * nc + c, 0)), state_spec],
        out_shape=[jax.ShapeDtypeStruct((rows, n_v), BF16),
                   jax.ShapeDtypeStruct((batch, B_HEADS, B_DK, B_DV), F32)],
        compiler_params=_params(("arbitrary", "arbitrary")),
        name="retention",
    )(*args)


def _log_sigmoid(x):
    return jnp.minimum(x, 0.0) - jnp.log(1.0 + jnp.exp(-jnp.abs(x)))


def _gla_head(q, k, v, lr, wg, bg, state, consts, chunk, sub, t_sub):
    nsub = chunk // sub
    tri, pair_live, pair_col, pick_t = consts
    log_a = _log_sigmoid(_nn_hi(lr, wg) + bg) * (LOG2_E / C_TAU)
    bl = _nn_exact_lhs(tri, log_a)

    def block_rows(vals):
        return jnp.concatenate([jnp.broadcast_to(x, (sub, C_DK)) for x in vals], axis=0)

    tot = [bl[j * sub + t_sub - 1:j * sub + t_sub, :] for j in range(nsub)]
    beta = [jnp.zeros((1, C_DK), F32)]
    for j in range(nsub):
        beta.append(beta[-1] + tot[j])
    qs = q * (C_DK ** -0.5)
    q_in = qs * jnp.exp2(bl)
    k_out = k * jnp.exp2(block_rows(tot) - bl)
    if t_sub < sub:
        k_out = jnp.where(lax.broadcasted_iota(jnp.int32, (chunk, 1), 0) % sub < t_sub, k_out, 0.0)
    vb = v.astype(BF16)

    o = _nn((q_in * block_rows([jnp.exp2(x) for x in beta[:nsub]])).astype(BF16), state.astype(BF16))

    def rep_rows(x):
        return jnp.concatenate([jnp.broadcast_to(x[t:t + 1, :], (sub, C_DK)) for t in range(sub)], axis=0)

    def tile_rows(x):
        return jnp.concatenate([x] * sub, axis=0)

    a_rows = []
    for i in range(nsub):
        r = slice(i * sub, (i + 1) * sub)
        pair = rep_rows(qs[r]) * tile_rows(k[r]) * jnp.exp2(rep_rows(bl[r]) - tile_rows(bl[r]))
        att = jnp.where(pair_live, jnp.sum(pair, axis=1, keepdims=True), 0.0)
        placed = jnp.where(pair_col == i * sub, att, 0.0).astype(BF16)
        a_i = _nn(pick_t, placed)
        if i > 0:
            between = [jnp.broadcast_to(jnp.exp2(beta[i] - beta[j + 1]), (sub, C_DK)) for j in range(i)]
            between.append(jnp.zeros(((nsub - i) * sub, C_DK), F32))
            a_i += _nt(q_in[r].astype(BF16), (k_out * jnp.concatenate(between, axis=0)).astype(BF16))
        a_rows.append(a_i)
    a = a_rows[0] if nsub == 1 else jnp.concatenate(a_rows, axis=0)
    o += _nn(a.astype(BF16), vb)

    k_end = k_out * block_rows([jnp.exp2(beta[nsub] - beta[j + 1]) for j in range(nsub)])
    d_col = jnp.broadcast_to(jnp.exp2(beta[nsub]), (8, C_DK)).T[:, 0:1]
    return o, d_col * state + _tn(k_end.astype(BF16), vb)


def _gla_consts(chunk, sub):
    ri = lax.broadcasted_iota(jnp.int32, (chunk, chunk), 0)
    ci = lax.broadcasted_iota(jnp.int32, (chunk, chunk), 1)
    tri = jnp.logical_and(ri // sub == ci // sub, ci <= ri).astype(BF16)
    pj = lax.broadcasted_iota(jnp.int32, (sub * sub, 1), 0)
    pair_live = pj // sub >= pj % sub
    pc = lax.broadcasted_iota(jnp.int32, (sub * sub, chunk), 1)
    ps = lax.broadcasted_iota(jnp.int32, (sub * sub, chunk), 0) % sub
    et = lax.broadcasted_iota(jnp.int32, (sub, sub * sub), 0)
    ej = lax.broadcasted_iota(jnp.int32, (sub, sub * sub), 1)
    pick_t = (ej // sub == et).astype(BF16)
    return tri, pair_live, pc - ps, pick_t


def _gla_body(*refs, has_state, chunk, sub, t_sub):
    consts = _gla_consts(chunk, sub)
    if has_state:
        q_ref, k_ref, v_ref, gate_ref, lr_ref, wg_ref, bg_ref, gn_ref, s0_ref, y_ref, st_ref = refs
    else:
        q_ref, k_ref, v_ref, gate_ref, lr_ref, wg_ref, bg_ref, gn_ref, y_ref, st_ref = refs
    c = pl.program_id(1)

    @pl.when(c == 0)
    def _():
        if has_state:
            st_ref[...] = s0_ref[...]
        else:
            st_ref[...] = jnp.zeros_like(st_ref)

    lr = lr_ref[:, 0:C_RANK]
    for h in range(C_HEADS):
        ks = slice(h * C_DK, (h + 1) * C_DK)
        vs = slice(h * C_DV, (h + 1) * C_DV)
        o, st_ref[0, h] = _gla_head(q_ref[:, ks], k_ref[:, ks], v_ref[:, vs], lr, wg_ref[:, ks], bg_ref[:, ks],
                                    st_ref[0, h], consts, chunk, sub, t_sub)
        yv = o * lax.rsqrt(jnp.mean(o * o, axis=-1, keepdims=True) + NORM_EPS) * gn_ref[:, vs]
        y_ref[:, vs] = (yv * _silu(gate_ref[:, vs])).astype(y_ref.dtype)


def _gla(z, z_lr, batch, chunk, t_real, wg3, bg3, gn3, slot, state0):
    rows = z.shape[0]
    nc = rows // (batch * chunk)
    sub, t_sub = (C_SUB, C_SUB) if t_real == chunk else (chunk, t_real)
    n_k, n_v = C_HEADS * C_DK, C_HEADS * C_DV
    n_slots = wg3.shape[0]
    in_specs = [pl.BlockSpec((chunk, n_k), lambda b, c: (b * nc + c, 0)),
                pl.BlockSpec((chunk, n_k), lambda b, c: (b * nc + c, 1)),
                pl.BlockSpec((chunk, n_v), lambda b, c: (b * nc + c, 2 * n_k // n_v)),
                pl.BlockSpec((chunk, n_v), lambda b, c: (b * nc + c, 2 * n_k // n_v + 1)),
                pl.BlockSpec((chunk, LANES), lambda b, c: (b * nc + c, 0)),
                pl.BlockSpec((None, C_RANK, n_k), lambda b, c: (slot, 0, 0)),
                pl.BlockSpec((None, 1, n_k), lambda b, c: (slot, 0, 0)),
                pl.BlockSpec((None, 1, n_v), lambda b, c: (slot, 0, 0))]
    args = [z, z, z, z, z_lr, wg3, bg3.reshape(n_slots, 1, n_k), gn3.reshape(n_slots, 1, n_v)]
    state_spec = pl.BlockSpec((1, C_HEADS, C_DK, C_DV), lambda b, c: (b, 0, 0, 0))
    if state0 is not None:
        in_specs.append(state_spec)
        args.append(state0)
    return pl.pallas_call(
        functools.partial(_gla_body, has_state=state0 is not None, chunk=chunk, sub=sub, t_sub=t_sub),
        grid=(batch, nc),
        in_specs=in_specs,
        out_specs=[pl.BlockSpec((chunk, n_v), lambda b, c: (b * nc + c, 0)), state_spec],
        out_shape=[jax.ShapeDtypeStruct((rows, n_v), BF16),
                   jax.ShapeDtypeStruct((batch, C_HEADS, C_DK, C_DV), F32)],
        compiler_params=_params(("arbitrary", "arbitrary")),
        name="gla",
    )(*args)


def _rot_tables_a(pos):
    half = A_ROT // 2
    inv_freq = A_ROPE_THETA ** (-jnp.arange(half, dtype=F32) / half)
    ang = pos.astype(F32)[:, None] * inv_freq[None, :]
    cos, sin = jnp.cos(ang), jnp.sin(ang)
    n = pos.shape[0]
    rest = A_HEAD_DIM - A_ROT
    c = jnp.concatenate([cos, cos, jnp.ones((n, rest), F32)], axis=1)
    s1 = jnp.concatenate([-sin, jnp.zeros((n, half + rest), F32)], axis=1)
    s2 = jnp.concatenate([jnp.zeros((n, half), F32), sin, jnp.zeros((n, rest), F32)], axis=1)
    return c, s1, s2


def _rot_tables_b(pos):
    half = B_DK // 2
    inv_freq = B_ROPE_THETA ** (-jnp.arange(half, dtype=F32) / half)
    ang = pos.astype(F32)[:, None] * inv_freq[None, :]
    return jnp.cos(ang), jnp.sin(ang)


def kernel(x_prompt, x_sample, cache_a_kv1, cache_a_kv2, cache_a_kv3, state_b, state_c, norm_g, final_g,
           w_in_a, w_out_a, w_in_b, gn_b, w_out_b, w_in_c, w_gate2_c, b_gate_c, gn_c, w_out_c):
    bp, seq, d = x_prompt.shape
    bs, t_real, _ = x_sample.shape
    caches_a = (cache_a_kv1, cache_a_kv2, cache_a_kv3)
    width = A_HEADS * A_HEAD_DIM

    xp = x_prompt.reshape(bp * seq, d)
    xs = jnp.pad(x_sample, ((0, 0), (0, SAMPLE_PAD - t_real), (0, 0))).reshape(bs * SAMPLE_PAD, d)
    ms = xs.shape[0]
    tm_p = 1024

    pos_p = jnp.arange(seq)
    pos_s = jnp.tile(PAST_LEN + jnp.arange(SAMPLE_PAD), bs)
    tabs_a_p, tabs_a_s = _rot_tables_a(pos_p), _rot_tables_a(pos_s)
    tabs_b_p, tabs_b_s = _rot_tables_b(pos_p), _rot_tables_b(pos_s)

    hp = _rmsnorm(xp, norm_g[0], BF16)
    hs = _rmsnorm(xs, norm_g[0], BF16)
    w_out_a, w_out_b, w_out_c = _cast_bf16(w_out_a), _cast_bf16(w_out_b), _cast_bf16(w_out_c)

    a_z_p, a_rows_s = [], []
    b_states_p, b_states_s, c_states_p, c_states_s = [], [], [], []
    n_layers = len(LAYER_KINDS)
    for i in range(n_layers):
        kind, slot = LAYER_KINDS[i], LAYER_SLOTS[i]
        last = i == n_layers - 1
        g_next = final_g if last else norm_g[i + 1]
        h_dtype = F32 if last else BF16
        if kind == 0:
            n_in = w_in_a.shape[2]
            zp, zs = _proj(hp, hs, w_in_a, slot, n_in, "a", tabs_a_p, tabs_a_s, seq // tm_p, tm_p, 1024)
            yp = _attn_prompt(zp, bp, seq)
            ys = _attn_sample(zs, caches_a, slot, bs, t_real)
            ys = jnp.pad(ys, ((0, 0), (0, SAMPLE_PAD - t_real), (0, 0))).reshape(ms, width).astype(BF16)
            zs3 = zs.reshape(bs, SAMPLE_PAD, n_in)
            rows_s = []
            for g in range(len(A_GROUPS)):
                lo = (3 * g + 1) * width
                rows_s.append(zs3[:, :t_real, lo:lo + 2 * width].reshape(bs, t_real, 2, A_HEADS, A_HEAD_DIM))
            a_z_p.append(zp)
            a_rows_s.append(rows_s)
            w_out = w_out_a
        elif kind == 1:
            n_in = w_in_b.shape[2]
            zp, zs = _proj(hp, hs, w_in_b, slot, n_in, "b", tabs_b_p, tabs_b_s, seq // tm_p, tm_p, 1024)
            yp, st_p = _retention(zp, bp, min(B_CHUNK, seq), min(B_CHUNK, seq), gn_b[slot], None)
            ys, st_s = _retention(zs, bs, SAMPLE_PAD, t_real, gn_b[slot], state_b[slot])
            b_states_p.append(st_p)
            b_states_s.append(st_s)
            w_out = w_out_b
        else:
            n_main = 2 * C_HEADS * C_DK + 2 * C_HEADS * C_DV
            w_lr = jnp.pad(w_in_c[slot][:, n_main:], ((0, 0), (0, LANES - C_RANK)))[None]
            zp, zs = _proj(hp, hs, w_in_c, slot, n_main, "c", (), (), 1, tm_p, 1024)
            zp_lr, zs_lr = _proj(hp, hs, w_lr, 0, LANES, "c", (), (), 1, tm_p, LANES)
            yp, st_p = _gla(zp, zp_lr, bp, min(C_CHUNK, seq), min(C_CHUNK, seq),
                            w_gate2_c, b_gate_c, gn_c, slot, None)
            ys, st_s = _gla(zs, zs_lr, bs, SAMPLE_PAD, t_real, w_gate2_c, b_gate_c, gn_c, slot, state_c[slot])
            c_states_p.append(st_p)
            c_states_s.append(st_s)
            w_out = w_out_c
        xp, hp = _outproj(yp, w_out, slot, xp, g_next, h_dtype, 512)
        xs, hs = _outproj(ys, w_out, slot, xs, g_next, h_dtype, ms)

    y_prompt = hp.reshape(bp, seq, d)
    y_sample = hs.reshape(bs, SAMPLE_PAD, d)[:, :t_real]
    stack = lambda rows, g: jnp.stack([r[g] for r in rows])
    kv_p = [_kv_rows(a_z_p, g, bp, seq, min(win, seq)) for g, (win, _) in enumerate(A_GROUPS)]
    return (y_prompt, y_sample,
            kv_p[0], kv_p[1], kv_p[2],
            jnp.stack(b_states_p), jnp.stack(c_states_p),
            stack(a_rows_s, 0), stack(a_rows_s, 1), stack(a_rows_s, 2),
            jnp.stack(b_states_s), jnp.stack(c_states_s))
```

```python
import functools

import jax
import jax.numpy as jnp
from jax import lax
from jax.experimental import pallas as pl
from jax.experimental.pallas import tpu as pltpu

F32 = jnp.float32
BF16 = jnp.bfloat16

PAST_LEN = 16384
NORM_EPS = 1e-6
A_GROUPS = ((128, 1), (512, 4), (2048, 16))
A_HEADS = 16
A_HEAD_DIM = 128
A_ROT = A_HEAD_DIM // 4
A_ROPE_THETA = 500000.0
A_BAND = 128
B_HEADS = 8
B_DK = 256
B_DV = 512
B_CHUNK = 128
B_ROPE_THETA = 10000.0
C_HEADS = 4
C_DK = 256
C_DV = 512
C_RANK = 16
C_TAU = 16.0
C_CHUNK = 64
C_SUB = 8
LAYER_KINDS = (0, 1, 2, 0)
LAYER_SLOTS = (0, 0, 0, 1)

SAMPLE_PAD = 16
LANES = 128
MXU_N = 256
VMEM_LIMIT = 56 * 1024 * 1024


def _params(sem, vmem=VMEM_LIMIT):
    return pltpu.CompilerParams(dimension_semantics=sem, vmem_limit_bytes=vmem)


def _silu(g):
    return g * (1.0 / (1.0 + jnp.exp(-g)))


def _nt(a, b):
    return lax.dot_general(a, b, (((1,), (1,)), ((), ())), preferred_element_type=F32)


def _nn(a, b):
    return jnp.dot(a, b, preferred_element_type=F32)


def _split(x):
    hi = x.astype(BF16)
    lo = (x - hi.astype(F32)).astype(BF16)
    return hi, lo


def _split3(x):
    hi = x.astype(BF16)
    r = x - hi.astype(F32)
    mid = r.astype(BF16)
    lo = (r - mid.astype(F32)).astype(BF16)
    return hi, mid, lo


def _nn_exact_lhs(sel, x):
    hi, mid, lo = _split3(x)
    return _nn(sel, hi) + _nn(sel, mid) + _nn(sel, lo)


def _round_robin(gens):
    gens = list(gens)
    results = [None] * len(gens)
    live = list(range(len(gens)))
    while live:
        for n in list(live):
            try:
                next(gens[n])
            except StopIteration as stop:
                results[n] = stop.value
                live.remove(n)
    return results


def _staggered(gens):
    pending = list(gens)
    active = []
    while pending or active:
        if pending:
            active.append(pending.pop(0))
        for g in list(active):
            try:
                next(g)
            except StopIteration:
                active.remove(g)


def _nn_hi(a, b):
    ah, al = _split(a)
    bh, bl = _split(b)
    return _nn(ah, bh) + _nn(ah, bl) + _nn(al, bh)


def _rmsnorm_body(x_ref, g_ref, o_ref):
    x = x_ref[...]
    ms = jnp.mean(x * x, axis=-1, keepdims=True)
    o_ref[...] = (x * lax.rsqrt(ms + NORM_EPS) * g_ref[...]).astype(o_ref.dtype)


def _rmsnorm(x, g, out_dtype):
    m, d = x.shape
    tm = min(m, 512)
    return pl.pallas_call(
        _rmsnorm_body,
        grid=(m // tm,),
        in_specs=[pl.BlockSpec((tm, d), lambda i: (i, 0)),
                  pl.BlockSpec((1, d), lambda i: (0, 0))],
        out_specs=pl.BlockSpec((tm, d), lambda i: (i, 0)),
        out_shape=jax.ShapeDtypeStruct((m, d), out_dtype),
        compiler_params=_params(("arbitrary",)),
        name="rmsnorm",
    )(x, g.reshape(1, d))


N_ROT_TABLES = {"a": 3, "b": 2, "c": 0}


def _rot_a(acc, cs, z_ref, tabs, j, tn):
    c_ref, s1_ref, s2_ref = tabs
    half = A_ROT // 2
    for o in range(0, acc.shape[1], A_HEAD_DIM):
        x = acc[:, o:o + A_HEAD_DIM]
        x_hi = pltpu.roll(x, A_HEAD_DIM - half, 1)
        x_lo = pltpu.roll(x, half, 1)
        z_ref[:, cs + o:cs + o + A_HEAD_DIM] = x * c_ref[...] + x_hi * s1_ref[...] + x_lo * s2_ref[...]


def _rot_b(acc, cs, z_ref, tabs, j, tn):
    c_ref, s_ref = tabs
    scale = jnp.where(j * tn >= B_HEADS * B_DK, B_DK ** -0.5, 1.0).astype(F32)
    half = B_DK // 2
    for o in range(0, acc.shape[1], B_DK):
        x1 = acc[:, o:o + half]
        x2 = acc[:, o + half:o + B_DK]
        z_ref[:, cs + o:cs + o + half] = (x1 * c_ref[...] - x2 * s_ref[...]) * scale
        z_ref[:, cs + o + half:cs + o + B_DK] = (x1 * s_ref[...] + x2 * c_ref[...]) * scale


def _proj_body(*refs, kind, tn):
    n_tab = N_ROT_TABLES[kind]
    h_ref, hs_ref, w_ref = refs[:3]
    tabs, tabs_s = refs[3:3 + n_tab], refs[3 + n_tab:3 + 2 * n_tab]
    z_ref, zs_ref, wb_ref = refs[3 + 2 * n_tab:]
    j = pl.program_id(0)
    i = pl.program_id(1)
    if kind == "a":
        seg = (j * tn) // (A_HEADS * A_HEAD_DIM)
        is_rot, epilogue = jnp.logical_and(seg < 3 * len(A_GROUPS), seg % 3 < 2), _rot_a
    elif kind == "b":
        is_rot, epilogue = (j * tn) < 2 * B_HEADS * B_DK, _rot_b
    else:
        is_rot, epilogue = None, None

    def tile(x_ref, o_ref, tables, rot):
        slab = min(tn, MXU_N)
        for cs in range(0, tn, slab):
            acc = _nn(x_ref[...], wb_ref[:, cs:cs + slab])
            if rot:
                epilogue(acc, cs, o_ref, tables, j, tn)
            else:
                o_ref[:, cs:cs + slab] = acc

    def run(x_ref, o_ref, tables):
        if epilogue is None:
            tile(x_ref, o_ref, tables, False)
        else:
            pl.when(is_rot)(lambda: tile(x_ref, o_ref, tables, True))
            pl.when(jnp.logical_not(is_rot))(lambda: tile(x_ref, o_ref, tables, False))

    @pl.when(i == 0)
    def _():
        wb_ref[...] = w_ref[...].astype(BF16)
        run(hs_ref, zs_ref, tabs_s)

    run(h_ref, z_ref, tabs)


def _proj(h, hs, w3, slot, n_out, kind, tabs, tabs_s, tab_blocks, tm, tn):
    m, k = h.shape
    ms = hs.shape[0]
    in_specs = [pl.BlockSpec((tm, k), lambda j, i: (i, 0)),
                pl.BlockSpec((ms, k), lambda j, i: (0, 0)),
                pl.BlockSpec((None, k, tn), lambda j, i: (slot, 0, j))]
    in_specs += [pl.BlockSpec((tm, LANES), lambda j, i: (i % tab_blocks, 0)) for _ in tabs]
    in_specs += [pl.BlockSpec((ms, LANES), lambda j, i: (0, 0)) for _ in tabs_s]
    return pl.pallas_call(
        functools.partial(_proj_body, kind=kind, tn=tn),
        grid=(n_out // tn, m // tm),
        in_specs=in_specs,
        out_specs=[pl.BlockSpec((tm, tn), lambda j, i: (i, j)),
                   pl.BlockSpec((ms, tn), lambda j, i: (0, j))],
        out_shape=[jax.ShapeDtypeStruct((m, n_out), F32),
                   jax.ShapeDtypeStruct((ms, n_out), F32)],
        scratch_shapes=[pltpu.VMEM((k, tn), BF16)],
        compiler_params=_params(("arbitrary", "arbitrary")),
        name="proj_" + kind,
    )(h, hs, w3, *tabs, *tabs_s)


def _cast_body(x_ref, o_ref):
    o_ref[...] = x_ref[...].astype(o_ref.dtype)


def _cast_bf16(w3):
    n, k, d = w3.shape
    rows = n * k
    tr = min(rows, 1024)
    out = pl.pallas_call(
        _cast_body,
        grid=(rows // tr,),
        in_specs=[pl.BlockSpec((tr, d), lambda i: (i, 0))],
        out_specs=pl.BlockSpec((tr, d), lambda i: (i, 0)),
        out_shape=jax.ShapeDtypeStruct((rows, d), BF16),
        compiler_params=_params(("arbitrary",)),
        name="cast_bf16",
    )(w3.reshape(rows, d))
    return out.reshape(n, k, d)


def _outproj_body(y_ref, w_ref, x_ref, g_ref, xo_ref, ho_ref):
    xn = x_ref[...] + _nn(y_ref[...], w_ref[...])
    xo_ref[...] = xn
    ms = jnp.mean(xn * xn, axis=-1, keepdims=True)
    ho_ref[...] = (xn * lax.rsqrt(ms + NORM_EPS) * g_ref[...]).astype(ho_ref.dtype)


def _outproj(y, w3, slot, x, g, h_dtype, tm):
    m, kd = y.shape
    d = x.shape[1]
    return pl.pallas_call(
        _outproj_body,
        grid=(m // tm,),
        in_specs=[pl.BlockSpec((tm, kd), lambda i: (i, 0)),
                  pl.BlockSpec((None, kd, d), lambda i: (slot, 0, 0), pipeline_mode=pl.Buffered(1)),
                  pl.BlockSpec((tm, d), lambda i: (i, 0)),
                  pl.BlockSpec((1, d), lambda i: (0, 0))],
        out_specs=[pl.BlockSpec((tm, d), lambda i: (i, 0)),
                   pl.BlockSpec((tm, d), lambda i: (i, 0))],
        out_shape=[jax.ShapeDtypeStruct((m, d), F32),
                   jax.ShapeDtypeStruct((m, d), h_dtype)],
        compiler_params=_params(("arbitrary",)),
        name="outproj",
    )(y, w3, x, g.reshape(1, d))


LOG2_E = 1.4426950408889634


def _rows(start, size, stride):
    return pl.ds(start, size) if stride == 1 else pl.ds(start, size, stride=stride)


def _attn_prompt_body(*refs, seq):
    qkv_refs, gate_ref, bias_first_ref, bias_band_ref, y_ref = refs[:9], *refs[9:13]
    o_s, d_s, m_s, qb_s, kb_s, vb_s = refs[13:]
    q_scale = (A_HEAD_DIM ** -0.5) * LOG2_E
    for g, (_, dil) in enumerate(A_GROUPS):
        q_ref, k_ref, v_ref = qkv_refs[3 * g:3 * g + 3]
        ln = seq // dil
        for r in range(dil):
            src, dst = _rows(r, ln, dil), slice(r * ln, (r + 1) * ln)
            qb_s[dst, :] = (q_ref[src, :] * q_scale).astype(BF16)
            kb_s[dst, :] = k_ref[src, :].astype(BF16)
            vb_s[dst, :] = v_ref[src, :].astype(BF16)

        def block(r, n, g=g, dil=dil, ln=ln):
            lo = r * ln + n * A_BAND
            first = n == 0
            keys = slice(lo, lo + A_BAND) if first else slice(lo - A_BAND, lo + A_BAND)
            s = _nt(qb_s[lo:lo + A_BAND, :], kb_s[keys, :])
            yield
            s = s + (bias_first_ref[...] if first else bias_band_ref[...])
            mx = jnp.max(s, axis=-1, keepdims=True)
            p = jnp.exp2(s - mx).astype(BF16)
            v = vb_s[keys, :]
            od = _nn(p, jnp.concatenate([v, jnp.ones_like(v)], axis=1))
            yield
            rows = _rows(r + dil * A_BAND * n, A_BAND, dil)
            o_s[g, rows, :] = od[:, :A_HEAD_DIM]
            d_s[g, rows, :] = od[:, A_HEAD_DIM:]
            m_s[g, rows, :] = jnp.broadcast_to(mx, (A_BAND, A_HEAD_DIM))

        _staggered([block(r, n) for r in range(dil) for n in range(ln // A_BAND)])
    for tile in range(seq // A_BAND):
        rs = slice(tile * A_BAND, (tile + 1) * A_BAND)
        m1, m2, m3 = m_s[0, rs, :], m_s[1, rs, :], m_s[2, rs, :]
        mx = jnp.maximum(jnp.maximum(m1, m2), m3)
        f1, f2, f3 = jnp.exp2(m1 - mx), jnp.exp2(m2 - mx), jnp.exp2(m3 - mx)
        num = f1 * o_s[0, rs, :] + f2 * o_s[1, rs, :] + f3 * o_s[2, rs, :]
        den = f1 * d_s[0, rs, :] + f2 * d_s[1, rs, :] + f3 * d_s[2, rs, :]
        y_ref[rs, :] = (num / den * _silu(gate_ref[rs, :])).astype(y_ref.dtype)


def _attn_prompt(z, batch, seq):
    n_groups = len(A_GROUPS)
    in_specs = []
    for c in range(3 * n_groups + 1):
        in_specs.append(pl.BlockSpec((seq, A_HEAD_DIM), lambda b, h, c=c: (b, c * A_HEADS + h)))
    row = jnp.arange(A_BAND)[:, None]
    col = jnp.arange(2 * A_BAND)[None, :]
    bias_first = jnp.where(col[:, :A_BAND] <= row, 0.0, -jnp.inf).astype(F32)
    bias_band = jnp.where((col >= row) & (col <= row + A_BAND), 0.0, -jnp.inf).astype(F32)
    in_specs.append(pl.BlockSpec((A_BAND, A_BAND), lambda b, h: (0, 0)))
    in_specs.append(pl.BlockSpec((A_BAND, 2 * A_BAND), lambda b, h: (0, 0)))
    return pl.pallas_call(
        functools.partial(_attn_prompt_body, seq=seq),
        grid=(batch, A_HEADS),
        in_specs=in_specs,
        out_specs=pl.BlockSpec((seq, A_HEAD_DIM), lambda b, h: (b, h)),
        out_shape=jax.ShapeDtypeStruct((batch * seq, A_HEADS * A_HEAD_DIM), BF16),
        scratch_shapes=[pltpu.VMEM((n_groups, seq, A_HEAD_DIM), F32)] * 3
        + [pltpu.VMEM((seq, A_HEAD_DIM), BF16)] * 3,
        compiler_params=_params(("arbitrary", "arbitrary")),
        name="attn_prompt",
    )(*([z] * (3 * n_groups + 1)), bias_first, bias_band)


def _attn_sample_body(z_ref, c1_ref, c2_ref, c3_ref, y_ref):
    t = pl.program_id(1)
    scale = A_HEAD_DIM ** -0.5
    m_idx = lax.broadcasted_iota(jnp.int32, (A_BAND, A_HEADS, 1), 0)
    n_idx = lax.broadcasted_iota(jnp.int32, (SAMPLE_PAD, A_HEADS, 1), 0)
    caches = (c1_ref, c2_ref, c3_ref)
    outs, lses = [], []
    for g, (_, dil) in enumerate(A_GROUPS):
        base = 3 * g * A_HEADS
        q = z_ref[0, t, base:base + A_HEADS, :]
        kc = caches[g][0, :, 0:A_HEADS, :]
        vc = caches[g][0, :, A_HEADS:2 * A_HEADS, :]
        kn = z_ref[0, :, base + A_HEADS:base + 2 * A_HEADS, :]
        vn = z_ref[0, :, base + 2 * A_HEADS:base + 3 * A_HEADS, :]
        s_c = jnp.sum(kc * q[None], axis=-1, keepdims=True) * scale
        s_n = jnp.sum(kn * q[None], axis=-1, keepdims=True) * scale
        if dil == 1:
            s_c = jnp.where(m_idx >= t, s_c, -jnp.inf)
            s_n = jnp.where(n_idx <= t, s_n, -jnp.inf)
        else:
            s_n = jnp.where(n_idx == t, s_n, -jnp.inf)
        mx = jnp.maximum(jnp.max(s_c, axis=0), jnp.max(s_n, axis=0))
        p_c = jnp.exp(s_c - mx[None])
        p_n = jnp.exp(s_n - mx[None])
        den = jnp.sum(p_c, axis=0) + jnp.sum(p_n, axis=0)
        num = jnp.sum(p_c * vc, axis=0) + jnp.sum(p_n * vn, axis=0)
        outs.append(num / den)
        lses.append(mx + jnp.log(den))
    lmax = jnp.maximum(jnp.maximum(lses[0], lses[1]), lses[2])
    es = [jnp.exp(l - lmax) for l in lses]
    tot = es[0] + es[1] + es[2]
    o = (es[0] / tot) * outs[0] + (es[1] / tot) * outs[1] + (es[2] / tot) * outs[2]
    gate = z_ref[0, t, 3 * len(A_GROUPS) * A_HEADS:(3 * len(A_GROUPS) + 1) * A_HEADS, :]
    y_ref[0, 0] = o * _silu(gate)


def _attn_sample(zs, caches, slot, batch, t_real):
    n_in = zs.shape[1]
    z5 = zs.reshape(batch, SAMPLE_PAD, n_in // A_HEAD_DIM, A_HEAD_DIM)
    in_specs = [pl.BlockSpec((1, SAMPLE_PAD, n_in // A_HEAD_DIM, A_HEAD_DIM), lambda b, t: (b, 0, 0, 0))]
    args = [z5]
    for g, (win, dil) in enumerate(A_GROUPS):
        n_a = caches[g].shape[0]
        cv = caches[g].reshape(n_a * batch, win // dil, dil * 2 * A_HEADS, A_HEAD_DIM)
        if dil == 1:
            idx = lambda b, t: (slot * batch + b, 0, 0, 0)
        else:
            idx = lambda b, t: (slot * batch + b, 0, t, 0)
        in_specs.append(pl.BlockSpec((1, A_BAND, 2 * A_HEADS, A_HEAD_DIM), idx))
        args.append(cv)
    y = pl.pallas_call(
        _attn_sample_body,
        grid=(batch, t_real),
        in_specs=in_specs,
        out_specs=pl.BlockSpec((1, 1, A_HEADS, A_HEAD_DIM), lambda b, t: (b, t, 0, 0)),
        out_shape=jax.ShapeDtypeStruct((batch, t_real, A_HEADS, A_HEAD_DIM), F32),
        compiler_params=_params(("arbitrary", "arbitrary")),
        name="attn_sample",
    )(*args)
    return y.reshape(batch, t_real, A_HEADS * A_HEAD_DIM)


def _kv_rows_body(*refs, n_layers):
    out_ref = refs[-1]
    layer = pl.program_id(0)
    for l in range(n_layers):
        k_ref, v_ref = refs[2 * l], refs[2 * l + 1]

        @pl.when(layer == l)
        def _():
            sub = 8
            for base, src in ((0, k_ref), (A_HEADS, v_ref)):
                for c0 in range(0, A_HEADS, sub):
                    cols = [src[:, (c0 + c) * A_HEAD_DIM:(c0 + c + 1) * A_HEAD_DIM] for c in range(sub)]
                    out_ref[:, base + c0:base + c0 + sub, :] = jnp.swapaxes(jnp.stack(cols, axis=0), 0, 1)


def _kv_rows(zs, g, batch, seq, keep):
    n_layers = len(zs)
    width = A_HEADS * A_HEAD_DIM
    tt = min(keep, 256)
    first = (seq - keep) // tt
    in_specs, args = [], []
    for l in range(n_layers):
        for c in (1, 2):
            def idx(s, b, i, l=l, c=c):
                return (jnp.where(s == l, b * (seq // tt) + first + i, 0), 3 * g + c)
            in_specs.append(pl.BlockSpec((tt, width), idx))
            args.append(zs[l])
    out = pl.pallas_call(
        functools.partial(_kv_rows_body, n_layers=n_layers),
        grid=(n_layers, batch, keep // tt),
        in_specs=in_specs,
        out_specs=pl.BlockSpec((None, None, tt, 2 * A_HEADS, A_HEAD_DIM), lambda s, b, i: (s, b, i, 0, 0)),
        out_shape=jax.ShapeDtypeStruct((n_layers, batch, keep, 2 * A_HEADS, A_HEAD_DIM), F32),
        compiler_params=_params(("arbitrary", "arbitrary", "arbitrary")),
        name="kv_rows_g%d" % g,
    )(*args)
    return out.reshape(n_layers, batch, keep, 2, A_HEADS, A_HEAD_DIM)


def _tn(a, b):
    return lax.dot_general(a, b, (((0,), (0,)), ((), ())), preferred_element_type=F32)


def _retention_body(*refs, has_state):
    if has_state:
        q_ref, k_ref, v_ref, gate_ref, dec_ref, xi_ref, kd_ref, sd_ref, gn_ref, s0_ref, y_ref, st_ref = refs
    else:
        q_ref, k_ref, v_ref, gate_ref, dec_ref, xi_ref, kd_ref, sd_ref, gn_ref, y_ref, st_ref = refs
    c = pl.program_id(1)

    @pl.when(c == 0)
    def _():
        if has_state:
            st_ref[...] = s0_ref[...]
        else:
            st_ref[...] = jnp.zeros_like(st_ref)

    def head(h):
        ks = slice(h * B_DK, (h + 1) * B_DK)
        vs = slice(h * B_DV, (h + 1) * B_DV)
        qb = q_ref[:, ks].astype(BF16)
        kf = k_ref[:, ks]
        vb = v_ref[:, vs].astype(BF16)
        state = st_ref[0, h]
        scores = _nt(qb, kf.astype(BF16))
        cross = _nn(qb, state.astype(BF16))
        k_dec = (kf * kd_ref[h]).astype(BF16)
        st_ref[0, h] = sd_ref[h] * state + _tn(k_dec, vb)
        yield
        o = _nn((scores * dec_ref[h]).astype(BF16), vb) + cross * xi_ref[h]
        yield
        mu = jnp.mean(o, axis=-1, keepdims=True)
        var = jnp.mean(jnp.square(o - mu), axis=-1, keepdims=True)
        yv = (o - mu) * lax.rsqrt(var + NORM_EPS) * gn_ref[h]
        y_ref[:, vs] = (yv * _silu(gate_ref[:, vs])).astype(y_ref.dtype)

    _round_robin([head(h) for h in range(B_HEADS)])


def _retention(z, batch, chunk, t_real, gn, state0):
    rows = z.shape[0]
    nc = rows // (batch * chunk)
    lg = jnp.log(1.0 - 2.0 ** (-5.0 - jnp.arange(B_HEADS, dtype=F32)))[:, None, None]
    idx = jnp.arange(chunk, dtype=F32)
    diff = idx[:, None] - idx[None, :]
    dec = jnp.where(diff >= 0, jnp.exp(lg * jnp.maximum(diff, 0.0)), 0.0)
    xi = jnp.exp(lg * (idx[None, :, None] + 1.0))
    live = idx[None, :, None] < t_real
    kd = jnp.where(live, jnp.exp(lg * jnp.where(live, t_real - 1.0 - idx[None, :, None], 0.0)), 0.0)
    sd = jnp.exp(lg * t_real)
    n_k, n_v = B_HEADS * B_DK, B_HEADS * B_DV
    const = lambda b, c: (0, 0, 0)
    in_specs = [pl.BlockSpec((chunk, n_k), lambda b, c: (b * nc + c, 0)),
                pl.BlockSpec((chunk, n_k), lambda b, c: (b * nc + c, 1)),
                pl.BlockSpec((chunk, n_v), lambda b, c: (b * nc + c, 2 * n_k // n_v)),
                pl.BlockSpec((chunk, n_v), lambda b, c: (b * nc + c, 2 * n_k // n_v + 1)),
                pl.BlockSpec((B_HEADS, chunk, chunk), const),
                pl.BlockSpec((B_HEADS, chunk, 1), const),
                pl.BlockSpec((B_HEADS, chunk, 1), const),
                pl.BlockSpec((B_HEADS, 1, 1), const),
                pl.BlockSpec((B_HEADS, 1, B_DV), const)]
    args = [z, z, z, z, dec, xi, kd, sd, gn.reshape(B_HEADS, 1, B_DV)]
    state_spec = pl.BlockSpec((1, B_HEADS, B_DK, B_DV), lambda b, c: (b, 0, 0, 0))
    if state0 is not None:
        in_specs.append(state_spec)
        args.append(state0)
    return pl.pallas_call(
        functools.partial(_retention_body, has_state=state0 is not None),
        grid=(batch, nc),
        in_specs=in_specs,
        out_specs=[pl.BlockSpec((chunk, n_v), lambda b, c: (b * nc + c, 0)), state_spec],
        out_shape=[jax.ShapeDtypeStruct((rows, n_v), BF16),
                   jax.ShapeDtypeStruct((batch, B_HEADS, B_DK, B_DV), F32)],
        compiler_params=_params(("arbitrary", "arbitrary")),
        name="retention",
    )(*args)


def _log_sigmoid(x):
    return jnp.minimum(x, 0.0) - jnp.log(1.0 + jnp.exp(-jnp.abs(x)))


def _gla_head(q, k, v, lr, wg, bg, state, consts, chunk, sub, t_sub):
    nsub = chunk // sub
    tri, pair_live, pair_col, pick_t = consts
    pre = _nn_hi(lr, wg)
    yield
    log_a = _log_sigmoid(pre + bg) * (LOG2_E / C_TAU)
    bl = _nn_exact_lhs(tri, log_a)
    yield

    def block_rows(vals):
        return jnp.concatenate([jnp.broadcast_to(x, (sub, C_DK)) for x in vals], axis=0)

    tot = [bl[j * sub + t_sub - 1:j * sub + t_sub, :] for j in range(nsub)]
    beta = [jnp.zeros((1, C_DK), F32)]
    for j in range(nsub):
        beta.append(beta[-1] + tot[j])
    qs = q * (C_DK ** -0.5)
    q_in = qs * jnp.exp2(bl)
    k_out = k * jnp.exp2(block_rows(tot) - bl)
    if t_sub < sub:
        k_out = jnp.where(lax.broadcasted_iota(jnp.int32, (chunk, 1), 0) % sub < t_sub, k_out, 0.0)
    vb = v.astype(BF16)

    o = _nn((q_in * block_rows([jnp.exp2(x) for x in beta[:nsub]])).astype(BF16), state.astype(BF16))
    yield

    def rep_rows(x):
        return jnp.concatenate([jnp.broadcast_to(x[t:t + 1, :], (sub, C_DK)) for t in range(sub)], axis=0)

    def tile_rows(x):
        return jnp.concatenate([x] * sub, axis=0)

    a_rows = []
    for i in range(nsub):
        r = slice(i * sub, (i + 1) * sub)
        pair = rep_rows(qs[r]) * tile_rows(k[r]) * jnp.exp2(rep_rows(bl[r]) - tile_rows(bl[r]))
        att = jnp.where(pair_live, jnp.sum(pair, axis=1, keepdims=True), 0.0)
        placed = jnp.where(pair_col == i * sub, att, 0.0).astype(BF16)
        a_i = _nn(pick_t, placed)
        if i > 0:
            between = [jnp.broadcast_to(jnp.exp2(beta[i] - beta[j + 1]), (sub, C_DK)) for j in range(i)]
            between.append(jnp.zeros(((nsub - i) * sub, C_DK), F32))
            a_i += _nt(q_in[r].astype(BF16), (k_out * jnp.concatenate(between, axis=0)).astype(BF16))
        a_rows.append(a_i)
        if i % 2 == 1:
            yield
    a = a_rows[0] if nsub == 1 else jnp.concatenate(a_rows, axis=0)
    o += _nn(a.astype(BF16), vb)
    yield
    k_end = k_out * block_rows([jnp.exp2(beta[nsub] - beta[j + 1]) for j in range(nsub)])
    d_col = jnp.broadcast_to(jnp.exp2(beta[nsub]), (8, C_DK)).T[:, 0:1]
    return o, d_col * state + _tn(k_end.astype(BF16), vb)


def _gla_consts(chunk, sub):
    ri = lax.broadcasted_iota(jnp.int32, (chunk, chunk), 0)
    ci = lax.broadcasted_iota(jnp.int32, (chunk, chunk), 1)
    tri = jnp.logical_and(ri // sub == ci // sub, ci <= ri).astype(BF16)
    pj = lax.broadcasted_iota(jnp.int32, (sub * sub, 1), 0)
    pair_live = pj // sub >= pj % sub
    pc = lax.broadcasted_iota(jnp.int32, (sub * sub, chunk), 1)
    ps = lax.broadcasted_iota(jnp.int32, (sub * sub, chunk), 0) % sub
    et = lax.broadcasted_iota(jnp.int32, (sub, sub * sub), 0)
    ej = lax.broadcasted_iota(jnp.int32, (sub, sub * sub), 1)
    pick_t = (ej // sub == et).astype(BF16)
    return tri, pair_live, pc - ps, pick_t


def _gla_body(*refs, has_state, chunk, sub, t_sub):
    consts = _gla_consts(chunk, sub)
    if has_state:
        q_ref, k_ref, v_ref, gate_ref, lr_ref, wg_ref, bg_ref, gn_ref, s0_ref, y_ref, st_ref = refs
    else:
        q_ref, k_ref, v_ref, gate_ref, lr_ref, wg_ref, bg_ref, gn_ref, y_ref, st_ref = refs
    c = pl.program_id(1)

    @pl.when(c == 0)
    def _():
        if has_state:
            st_ref[...] = s0_ref[...]
        else:
            st_ref[...] = jnp.zeros_like(st_ref)

    lr = lr_ref[:, 0:C_RANK]
    heads = []
    for h in range(C_HEADS):
        ks = slice(h * C_DK, (h + 1) * C_DK)
        vs = slice(h * C_DV, (h + 1) * C_DV)
        heads.append(_gla_head(q_ref[:, ks], k_ref[:, ks], v_ref[:, vs], lr, wg_ref[:, ks], bg_ref[:, ks],
                               st_ref[0, h], consts, chunk, sub, t_sub))
    for h, (o, new_state) in enumerate(_round_robin(heads)):
        vs = slice(h * C_DV, (h + 1) * C_DV)
        st_ref[0, h] = new_state
        yv = o * lax.rsqrt(jnp.mean(o * o, axis=-1, keepdims=True) + NORM_EPS) * gn_ref[:, vs]
        y_ref[:, vs] = (yv * _silu(gate_ref[:, vs])).astype(y_ref.dtype)


def _gla(z, z_lr, batch, chunk, t_real, wg3, bg3, gn3, slot, state0):
    rows = z.shape[0]
    nc = rows // (batch * chunk)
    sub, t_sub = (C_SUB, C_SUB) if t_real == chunk else (chunk, t_real)
    n_k, n_v = C_HEADS * C_DK, C_HEADS * C_DV
    n_slots = wg3.shape[0]
    in_specs = [pl.BlockSpec((chunk, n_k), lambda b, c: (b * nc + c, 0)),
                pl.BlockSpec((chunk, n_k), lambda b, c: (b * nc + c, 1)),
                pl.BlockSpec((chunk, n_v), lambda b, c: (b * nc + c, 2 * n_k // n_v)),
                pl.BlockSpec((chunk, n_v), lambda b, c: (b * nc + c, 2 * n_k // n_v + 1)),
                pl.BlockSpec((chunk, LANES), lambda b, c: (b * nc + c, 0)),
                pl.BlockSpec((None, C_RANK, n_k), lambda b, c: (slot, 0, 0)),
                pl.BlockSpec((None, 1, n_k), lambda b, c: (slot, 0, 0)),
                pl.BlockSpec((None, 1, n_v), lambda b, c: (slot, 0, 0))]
    args = [z, z, z, z, z_lr, wg3, bg3.reshape(n_slots, 1, n_k), gn3.reshape(n_slots, 1, n_v)]
    state_spec = pl.BlockSpec((1, C_HEADS, C_DK, C_DV), lambda b, c: (b, 0, 0, 0))
    if state0 is not None:
        in_specs.append(state_spec)
        args.append(state0)
    return pl.pallas_call(
        functools.partial(_gla_body, has_state=state0 is not None, chunk=chunk, sub=sub, t_sub=t_sub),
        grid=(batch, nc),
        in_specs=in_specs,
        out_specs=[pl.BlockSpec((chunk, n_v), lambda b, c: (b * nc + c, 0)), state_spec],
        out_shape=[jax.ShapeDtypeStruct((rows, n_v), BF16),
                   jax.ShapeDtypeStruct((batch, C_HEADS, C_DK, C_DV), F32)],
        compiler_params=_params(("arbitrary", "arbitrary")),
        name="gla",
    )(*args)


def _rot_tables_a(pos):
    half = A_ROT // 2
    inv_freq = A_ROPE_THETA ** (-jnp.arange(half, dtype=F32) / half)
    ang = pos.astype(F32)[:, None] * inv_freq[None, :]
    cos, sin = jnp.cos(ang), jnp.sin(ang)
    n = pos.shape[0]
    rest = A_HEAD_DIM - A_ROT
    c = jnp.concatenate([cos, cos, jnp.ones((n, rest), F32)], axis=1)
    s1 = jnp.concatenate([-sin, jnp.zeros((n, half + rest), F32)], axis=1)
    s2 = jnp.concatenate([jnp.zeros((n, half), F32), sin, jnp.zeros((n, rest), F32)], axis=1)
    return c, s1, s2


def _rot_tables_b(pos):
    half = B_DK // 2
    inv_freq = B_ROPE_THETA ** (-jnp.arange(half, dtype=F32) / half)
    ang = pos.astype(F32)[:, None] * inv_freq[None, :]
    return jnp.cos(ang), jnp.sin(ang)


def kernel(x_prompt, x_sample, cache_a_kv1, cache_a_kv2, cache_a_kv3, state_b, state_c, norm_g, final_g,
           w_in_a, w_out_a, w_in_b, gn_b, w_out_b, w_in_c, w_gate2_c, b_gate_c, gn_c, w_out_c):
    bp, seq, d = x_prompt.shape
    bs, t_real, _ = x_sample.shape
    caches_a = (cache_a_kv1, cache_a_kv2, cache_a_kv3)
    width = A_HEADS * A_HEAD_DIM

    xp = x_prompt.reshape(bp * seq, d)
    xs = jnp.pad(x_sample, ((0, 0), (0, SAMPLE_PAD - t_real), (0, 0))).reshape(bs * SAMPLE_PAD, d)
    ms = xs.shape[0]
    tm_p = 1024

    pos_p = jnp.arange(seq)
    pos_s = jnp.tile(PAST_LEN + jnp.arange(SAMPLE_PAD), bs)
    tabs_a_p, tabs_a_s = _rot_tables_a(pos_p), _rot_tables_a(pos_s)
    tabs_b_p, tabs_b_s = _rot_tables_b(pos_p), _rot_tables_b(pos_s)

    hp = _rmsnorm(xp, norm_g[0], BF16)
    hs = _rmsnorm(xs, norm_g[0], BF16)
    w_out_a, w_out_b, w_out_c = _cast_bf16(w_out_a), _cast_bf16(w_out_b), _cast_bf16(w_out_c)

    a_z_p, a_rows_s = [], []
    b_states_p, b_states_s, c_states_p, c_states_s = [], [], [], []
    n_layers = len(LAYER_KINDS)
    for i in range(n_layers):
        kind, slot = LAYER_KINDS[i], LAYER_SLOTS[i]
        last = i == n_layers - 1
        g_next = final_g if last else norm_g[i + 1]
        h_dtype = F32 if last else BF16
        if kind == 0:
            n_in = w_in_a.shape[2]
            zp, zs = _proj(hp, hs, w_in_a, slot, n_in, "a", tabs_a_p, tabs_a_s, seq // tm_p, tm_p, 1024)
            yp = _attn_prompt(zp, bp, seq)
            ys = _attn_sample(zs, caches_a, slot, bs, t_real)
            ys = jnp.pad(ys, ((0, 0), (0, SAMPLE_PAD - t_real), (0, 0))).reshape(ms, width).astype(BF16)
            zs3 = zs.reshape(bs, SAMPLE_PAD, n_in)
            rows_s = []
            for g in range(len(A_GROUPS)):
                lo = (3 * g + 1) * width
                rows_s.append(zs3[:, :t_real, lo:lo + 2 * width].reshape(bs, t_real, 2, A_HEADS, A_HEAD_DIM))
            a_z_p.append(zp)
            a_rows_s.append(rows_s)
            w_out = w_out_a
        elif kind == 1:
            n_in = w_in_b.shape[2]
            zp, zs = _proj(hp, hs, w_in_b, slot, n_in, "b", tabs_b_p, tabs_b_s, seq // tm_p, tm_p, 1024)
            yp, st_p = _retention(zp, bp, min(B_CHUNK, seq), min(B_CHUNK, seq), gn_b[slot], None)
            ys, st_s = _retention(zs, bs, SAMPLE_PAD, t_real, gn_b[slot], state_b[slot])
            b_states_p.append(st_p)
            b_states_s.append(st_s)
            w_out = w_out_b
        else:
            n_main = 2 * C_HEADS * C_DK + 2 * C_HEADS * C_DV
            w_lr = jnp.pad(w_in_c[slot][:, n_main:], ((0, 0), (0, LANES - C_RANK)))[None]
            zp, zs = _proj(hp, hs, w_in_c, slot, n_main, "c", (), (), 1, tm_p, 1024)
            zp_lr, zs_lr = _proj(hp, hs, w_lr, 0, LANES, "c", (), (), 1, tm_p, LANES)
            yp, st_p = _gla(zp, zp_lr, bp, min(C_CHUNK, seq), min(C_CHUNK, seq),
                            w_gate2_c, b_gate_c, gn_c, slot, None)
            ys, st_s = _gla(zs, zs_lr, bs, SAMPLE_PAD, t_real, w_gate2_c, b_gate_c, gn_c, slot, state_c[slot])
            c_states_p.append(st_p)
            c_states_s.append(st_s)
            w_out = w_out_c
        xp, hp = _outproj(yp, w_out, slot, xp, g_next, h_dtype, 512)
        xs, hs = _outproj(ys, w_out, slot, xs, g_next, h_dtype, ms)

    y_prompt = hp.reshape(bp, seq, d)
    y_sample = hs.reshape(bs, SAMPLE_PAD, d)[:, :t_real]
    stack = lambda rows, g: jnp.stack([r[g] for r in rows])
    kv_p = [_kv_rows(a_z_p, g, bp, seq, min(win, seq)) for g, (win, _) in enumerate(A_GROUPS)]
    return (y_prompt, y_sample,
            kv_p[0], kv_p[1], kv_p[2],
            jnp.stack(b_states_p), jnp.stack(c_states_p),
            stack(a_rows_s, 0), stack(a_rows_s, 1), stack(a_rows_s, 2),
            jnp.stack(b_states_s), jnp.stack(c_states_s))
```

```python
import functools

import jax
import jax.numpy as jnp
from jax import lax
from jax.experimental import pallas as pl
from jax.experimental.pallas import tpu as pltpu

F32 = jnp.float32
BF16 = jnp.bfloat16

PAST_LEN = 16384
NORM_EPS = 1e-6
A_GROUPS = ((128, 1), (512, 4), (2048, 16))
A_HEADS = 16
A_HEAD_DIM = 128
A_ROT = A_HEAD_DIM // 4
A_ROPE_THETA = 500000.0
A_BAND = 128
B_HEADS = 8
B_DK = 256
B_DV = 512
B_CHUNK = 128
B_ROPE_THETA = 10000.0
C_HEADS = 4
C_DK = 256
C_DV = 512
C_RANK = 16
C_TAU = 16.0
C_CHUNK = 64
C_SUB = 8
LAYER_KINDS = (0, 1, 2, 0)
LAYER_SLOTS = (0, 0, 0, 1)

SAMPLE_PAD = 16
LANES = 128
MXU_N = 256
VMEM_LIMIT = 56 * 1024 * 1024


def _params(sem, vmem=VMEM_LIMIT):
    return pltpu.CompilerParams(dimension_semantics=sem, vmem_limit_bytes=vmem)


def _silu(g):
    return g * (1.0 / (1.0 + jnp.exp(-g)))


def _nt(a, b):
    return lax.dot_general(a, b, (((1,), (1,)), ((), ())), preferred_element_type=F32)


def _nn(a, b):
    return jnp.dot(a, b, preferred_element_type=F32)


def _split(x):
    hi = x.astype(BF16)
    lo = (x - hi.astype(F32)).astype(BF16)
    return hi, lo


def _split3(x):
    hi = x.astype(BF16)
    r = x - hi.astype(F32)
    mid = r.astype(BF16)
    lo = (r - mid.astype(F32)).astype(BF16)
    return hi, mid, lo


def _nn_exact_lhs(sel, x):
    hi, mid, lo = _split3(x)
    return _nn(sel, hi) + _nn(sel, mid) + _nn(sel, lo)


def _round_robin(gens):
    gens = list(gens)
    results = [None] * len(gens)
    live = list(range(len(gens)))
    while live:
        for n in list(live):
            try:
                next(gens[n])
            except StopIteration as stop:
                results[n] = stop.value
                live.remove(n)
    return results


def _staggered(gens):
    pending = list(gens)
    active = []
    while pending or active:
        if pending:
            active.append(pending.pop(0))
        for g in list(active):
            try:
                next(g)
            except StopIteration:
                active.remove(g)


def _nn_hi(a, b):
    ah, al = _split(a)
    bh, bl = _split(b)
    return _nn(ah, bh) + _nn(ah, bl) + _nn(al, bh)


def _rmsnorm_body(x_ref, g_ref, o_ref):
    x = x_ref[...]
    ms = jnp.mean(x * x, axis=-1, keepdims=True)
    o_ref[...] = (x * lax.rsqrt(ms + NORM_EPS) * g_ref[...]).astype(o_ref.dtype)


def _rmsnorm(x, g, out_dtype):
    m, d = x.shape
    tm = min(m, 512)
    return pl.pallas_call(
        _rmsnorm_body,
        grid=(m // tm,),
        in_specs=[pl.BlockSpec((tm, d), lambda i: (i, 0)),
                  pl.BlockSpec((1, d), lambda i: (0, 0))],
        out_specs=pl.BlockSpec((tm, d), lambda i: (i, 0)),
        out_shape=jax.ShapeDtypeStruct((m, d), out_dtype),
        compiler_params=_params(("arbitrary",)),
        name="rmsnorm",
    )(x, g.reshape(1, d))


N_ROT_TABLES = {"a": 2, "b": 2, "c": 0}


ROT_ROWS = 128


def _rot_a(src_ref, cs, width, z_ref, tabs, j, tn):
    c_ref, s_ref = tabs
    half = A_ROT // 2
    rows = min(ROT_ROWS, src_ref.shape[0])
    first_half = lax.broadcasted_iota(jnp.int32, (rows, A_HEAD_DIM), 1) < half
    for rb in range(0, src_ref.shape[0], rows):
        rs = slice(rb, rb + rows)
        for o in range(cs, cs + width, A_HEAD_DIM):
            x = src_ref[rs, o:o + A_HEAD_DIM]
            x_hi = pltpu.roll(x, A_HEAD_DIM - half, 1)
            x_lo = pltpu.roll(x, half, 1)
            z_ref[rs, o:o + A_HEAD_DIM] = x * c_ref[rs, :] + jnp.where(first_half, x_hi, x_lo) * s_ref[rs, :]


def _rot_b(src_ref, cs, width, z_ref, tabs, j, tn):
    c_ref, s_ref = tabs
    scale = jnp.where(j * tn >= B_HEADS * B_DK, B_DK ** -0.5, 1.0).astype(F32)
    half = B_DK // 2
    rows = min(ROT_ROWS, src_ref.shape[0])
    for rb in range(0, src_ref.shape[0], rows):
        rs = slice(rb, rb + rows)
        cos, sin = c_ref[rs, :] * scale, s_ref[rs, :] * scale
        for o in range(cs, cs + width, B_DK):
            x1 = src_ref[rs, o:o + half]
            x2 = src_ref[rs, o + half:o + B_DK]
            z_ref[rs, o:o + half] = x1 * cos - x2 * sin
            z_ref[rs, o + half:o + B_DK] = x1 * sin + x2 * cos


def _proj_body(*refs, kind, tn, ni, n_steps):
    n_tab = N_ROT_TABLES[kind]
    h_ref, hs_ref, w_ref = refs[:3]
    tabs, tabs_s = refs[3:3 + n_tab], refs[3 + n_tab:3 + 2 * n_tab]
    z_ref, zs_ref, wb_ref, acc_even, acc_odd = refs[3 + 2 * n_tab:]
    s = pl.program_id(0)
    s_mm = jnp.minimum(s, n_steps - 1)
    j, i = s_mm // ni, s_mm % ni
    j_prev = jnp.maximum(s - 1, 0) // ni
    epilogue = {"a": _rot_a, "b": _rot_b, "c": None}[kind]

    def is_rot(jj):
        if kind == "a":
            seg = (jj * tn) // (A_HEADS * A_HEAD_DIM)
            return jnp.logical_and(seg < 3 * len(A_GROUPS), seg % 3 < 2)
        return (jj * tn) < 2 * B_HEADS * B_DK

    slab = min(tn, MXU_N)

    def finish(src_ref, cs, o_ref, tables, jj, rot):
        if rot:
            epilogue(src_ref, cs, slab, o_ref, tables, jj, tn)
        elif src_ref is not o_ref:
            rows = min(ROT_ROWS, src_ref.shape[0])
            for rb in range(0, src_ref.shape[0], rows):
                o_ref[rb:rb + rows, cs:cs + slab] = src_ref[rb:rb + rows, cs:cs + slab]

    @pl.when(s == 0)
    def _():
        acc_odd[...] = jnp.zeros_like(acc_odd)

    @pl.when(jnp.logical_and(i == 0, s < n_steps))
    def _():
        wb_ref[...] = w_ref[...].astype(BF16)

        def sample(rot):
            for cs in range(0, tn, slab):
                zs_ref[:, cs:cs + slab] = _nn(hs_ref[...], wb_ref[:, cs:cs + slab])
                finish(zs_ref, cs, zs_ref, tabs_s, j, rot)

        if epilogue is None:
            sample(False)
        else:
            pl.when(is_rot(j))(lambda: sample(True))
            pl.when(jnp.logical_not(is_rot(j)))(lambda: sample(False))

    def step(acc_new, acc_old, rot):
        for cs in range(0, tn, slab):
            acc_new[:, cs:cs + slab] = _nn(h_ref[...], wb_ref[:, cs:cs + slab])
            finish(acc_old, cs, z_ref, tabs, j_prev, rot)

    even = s % 2 == 0
    for parity, acc_new, acc_old in ((True, acc_even, acc_odd), (False, acc_odd, acc_even)):
        on = even if parity else jnp.logical_not(even)
        if epilogue is None:
            pl.when(on)(functools.partial(step, acc_new, acc_old, False))
        else:
            rot_prev = is_rot(j_prev)
            pl.when(jnp.logical_and(on, rot_prev))(functools.partial(step, acc_new, acc_old, True))
            pl.when(jnp.logical_and(on, jnp.logical_not(rot_prev)))(functools.partial(step, acc_new, acc_old, False))


def _proj(h, hs, w3, slot, n_out, kind, tabs, tabs_s, tab_blocks, tm, tn):
    m, k = h.shape
    ms = hs.shape[0]
    ni = m // tm
    n_steps = (n_out // tn) * ni
    mm = lambda s: jnp.minimum(s, n_steps - 1)
    prev = lambda s: jnp.maximum(s - 1, 0)
    in_specs = [pl.BlockSpec((tm, k), lambda s: (mm(s) % ni, 0)),
                pl.BlockSpec((ms, k), lambda s: (0, 0)),
                pl.BlockSpec((None, k, tn), lambda s: (slot, 0, mm(s) // ni))]
    in_specs += [pl.BlockSpec((tm, LANES), lambda s: ((prev(s) % ni) % tab_blocks, 0)) for _ in tabs]
    in_specs += [pl.BlockSpec((ms, LANES), lambda s: (0, 0)) for _ in tabs_s]
    return pl.pallas_call(
        functools.partial(_proj_body, kind=kind, tn=tn, ni=ni, n_steps=n_steps),
        grid=(n_steps + 1,),
        in_specs=in_specs,
        out_specs=[pl.BlockSpec((tm, tn), lambda s: (prev(s) % ni, prev(s) // ni)),
                   pl.BlockSpec((ms, tn), lambda s: (0, mm(s) // ni))],
        out_shape=[jax.ShapeDtypeStruct((m, n_out), F32),
                   jax.ShapeDtypeStruct((ms, n_out), F32)],
        scratch_shapes=[pltpu.VMEM((k, tn), BF16), pltpu.VMEM((tm, tn), F32), pltpu.VMEM((tm, tn), F32)],
        compiler_params=_params(("arbitrary",)),
        name="proj_" + kind,
    )(h, hs, w3, *tabs, *tabs_s)


def _cast_body(x_ref, o_ref):
    o_ref[...] = x_ref[...].astype(o_ref.dtype)


def _cast_bf16(w3):
    n, k, d = w3.shape
    rows = n * k
    tr = min(rows, 1024)
    out = pl.pallas_call(
        _cast_body,
        grid=(rows // tr,),
        in_specs=[pl.BlockSpec((tr, d), lambda i: (i, 0))],
        out_specs=pl.BlockSpec((tr, d), lambda i: (i, 0)),
        out_shape=jax.ShapeDtypeStruct((rows, d), BF16),
        compiler_params=_params(("arbitrary",)),
        name="cast_bf16",
    )(w3.reshape(rows, d))
    return out.reshape(n, k, d)


def _outproj_body(y_ref, w_ref, x_ref, g_ref, xo_ref, ho_ref):
    xn = x_ref[...] + _nn(y_ref[...], w_ref[...])
    xo_ref[...] = xn
    ms = jnp.mean(xn * xn, axis=-1, keepdims=True)
    ho_ref[...] = (xn * lax.rsqrt(ms + NORM_EPS) * g_ref[...]).astype(ho_ref.dtype)


def _outproj(y, w3, slot, x, g, h_dtype, tm):
    m, kd = y.shape
    d = x.shape[1]
    return pl.pallas_call(
        _outproj_body,
        grid=(m // tm,),
        in_specs=[pl.BlockSpec((tm, kd), lambda i: (i, 0)),
                  pl.BlockSpec((None, kd, d), lambda i: (slot, 0, 0), pipeline_mode=pl.Buffered(1)),
                  pl.BlockSpec((tm, d), lambda i: (i, 0)),
                  pl.BlockSpec((1, d), lambda i: (0, 0))],
        out_specs=[pl.BlockSpec((tm, d), lambda i: (i, 0)),
                   pl.BlockSpec((tm, d), lambda i: (i, 0))],
        out_shape=[jax.ShapeDtypeStruct((m, d), F32),
                   jax.ShapeDtypeStruct((m, d), h_dtype)],
        compiler_params=_params(("arbitrary",)),
        name="outproj",
    )(y, w3, x, g.reshape(1, d))


LOG2_E = 1.4426950408889634


def _rows(start, size, stride):
    return pl.ds(start, size) if stride == 1 else pl.ds(start, size, stride=stride)


def _attn_prompt_body(*refs, seq):
    qkv_refs, gate_ref, bias_first_ref, bias_band_ref, y_ref = refs[:9], *refs[9:13]
    o_s, d_s, m_s = refs[13:]
    q_scale = (A_HEAD_DIM ** -0.5) * LOG2_E
    ones = jnp.ones((A_BAND, A_HEAD_DIM), BF16)
    for g, (_, dil) in enumerate(A_GROUPS):
        q_ref, k_ref, v_ref = qkv_refs[3 * g:3 * g + 3]

        def load_kv(rows, k_ref=k_ref, v_ref=v_ref):
            return k_ref[rows, :].astype(BF16), jnp.concatenate([v_ref[rows, :].astype(BF16), ones], axis=1)

        def chain(r, n0, n1, g=g, dil=dil, q_ref=q_ref, load_kv=load_kv):
            block_rows = lambda n: _rows(r + dil * A_BAND * n, A_BAND, dil)
            k_prev, v_prev = load_kv(block_rows(n0 - 1)) if n0 > 0 else (None, None)
            for n in range(n0, n1):
                rows = block_rows(n)
                q = (q_ref[rows, :] * q_scale).astype(BF16)
                k_cur, v_cur = load_kv(rows)
                if n == 0:
                    keys, vals, bias_ref = k_cur, v_cur, bias_first_ref
                else:
                    keys = jnp.concatenate([k_prev, k_cur], axis=0)
                    vals = jnp.concatenate([v_prev, v_cur], axis=0)
                    bias_ref = bias_band_ref
                k_prev, v_prev = k_cur, v_cur
                s = _nt(q, keys)
                yield
                s = s + bias_ref[...]
                mx = jnp.max(s, axis=-1, keepdims=True)
                od = _nn(jnp.exp2(s - mx).astype(BF16), vals)
                yield
                o_s[g, rows, :] = od[:, :A_HEAD_DIM]
                d_s[g, rows, :] = od[:, A_HEAD_DIM:]
                m_s[g, rows, :] = jnp.broadcast_to(mx, (A_BAND, A_HEAD_DIM))

        nb = seq // dil // A_BAND
        if dil == 1:
            chains = [chain(0, 0, nb // 2), chain(0, nb // 2, nb)]
        else:
            chains = [chain(r, 0, nb) for r in range(dil)]
        _staggered(chains)
    for tile in range(seq // A_BAND):
        rs = slice(tile * A_BAND, (tile + 1) * A_BAND)
        m1, m2, m3 = m_s[0, rs, :], m_s[1, rs, :], m_s[2, rs, :]
        mx = jnp.maximum(jnp.maximum(m1, m2), m3)
        f1, f2, f3 = jnp.exp2(m1 - mx), jnp.exp2(m2 - mx), jnp.exp2(m3 - mx)
        num = f1 * o_s[0, rs, :] + f2 * o_s[1, rs, :] + f3 * o_s[2, rs, :]
        den = f1 * d_s[0, rs, :] + f2 * d_s[1, rs, :] + f3 * d_s[2, rs, :]
        y_ref[rs, :] = (num / den * _silu(gate_ref[rs, :])).astype(y_ref.dtype)


def _attn_prompt(z, batch, seq):
    n_groups = len(A_GROUPS)
    in_specs = []
    for c in range(3 * n_groups + 1):
        in_specs.append(pl.BlockSpec((seq, A_HEAD_DIM), lambda b, h, c=c: (b, c * A_HEADS + h)))
    row = jnp.arange(A_BAND)[:, None]
    col = jnp.arange(2 * A_BAND)[None, :]
    bias_first = jnp.where(col[:, :A_BAND] <= row, 0.0, -jnp.inf).astype(F32)
    bias_band = jnp.where((col >= row) & (col <= row + A_BAND), 0.0, -jnp.inf).astype(F32)
    in_specs.append(pl.BlockSpec((A_BAND, A_BAND), lambda b, h: (0, 0)))
    in_specs.append(pl.BlockSpec((A_BAND, 2 * A_BAND), lambda b, h: (0, 0)))
    return pl.pallas_call(
        functools.partial(_attn_prompt_body, seq=seq),
        grid=(batch, A_HEADS),
        in_specs=in_specs,
        out_specs=pl.BlockSpec((seq, A_HEAD_DIM), lambda b, h: (b, h)),
        out_shape=jax.ShapeDtypeStruct((batch * seq, A_HEADS * A_HEAD_DIM), BF16),
        scratch_shapes=[pltpu.VMEM((n_groups, seq, A_HEAD_DIM), F32)] * 3,
        compiler_params=_params(("arbitrary", "arbitrary")),
        name="attn_prompt",
    )(*([z] * (3 * n_groups + 1)), bias_first, bias_band)


def _attn_sample_body(z_ref, c1_ref, c2_ref, c3_ref, y_ref):
    t = pl.program_id(1)
    scale = A_HEAD_DIM ** -0.5
    m_idx = lax.broadcasted_iota(jnp.int32, (A_BAND, A_HEADS, 1), 0)
    n_idx = lax.broadcasted_iota(jnp.int32, (SAMPLE_PAD, A_HEADS, 1), 0)
    caches = (c1_ref, c2_ref, c3_ref)
    outs, lses = [], []
    for g, (_, dil) in enumerate(A_GROUPS):
        base = 3 * g * A_HEADS
        q = z_ref[0, t, base:base + A_HEADS, :]
        kc = caches[g][0, :, 0:A_HEADS, :]
        vc = caches[g][0, :, A_HEADS:2 * A_HEADS, :]
        kn = z_ref[0, :, base + A_HEADS:base + 2 * A_HEADS, :]
        vn = z_ref[0, :, base + 2 * A_HEADS:base + 3 * A_HEADS, :]
        s_c = jnp.sum(kc * q[None], axis=-1, keepdims=True) * scale
        s_n = jnp.sum(kn * q[None], axis=-1, keepdims=True) * scale
        if dil == 1:
            s_c = jnp.where(m_idx >= t, s_c, -jnp.inf)
            s_n = jnp.where(n_idx <= t, s_n, -jnp.inf)
        else:
            s_n = jnp.where(n_idx == t, s_n, -jnp.inf)
        mx = jnp.maximum(jnp.max(s_c, axis=0), jnp.max(s_n, axis=0))
        p_c = jnp.exp(s_c - mx[None])
        p_n = jnp.exp(s_n - mx[None])
        den = jnp.sum(p_c, axis=0) + jnp.sum(p_n, axis=0)
        num = jnp.sum(p_c * vc, axis=0) + jnp.sum(p_n * vn, axis=0)
        outs.append(num / den)
        lses.append(mx + jnp.log(den))
    lmax = jnp.maximum(jnp.maximum(lses[0], lses[1]), lses[2])
    es = [jnp.exp(l - lmax) for l in lses]
    tot = es[0] + es[1] + es[2]
    o = (es[0] / tot) * outs[0] + (es[1] / tot) * outs[1] + (es[2] / tot) * outs[2]
    gate = z_ref[0, t, 3 * len(A_GROUPS) * A_HEADS:(3 * len(A_GROUPS) + 1) * A_HEADS, :]
    y_ref[0, 0] = o * _silu(gate)


def _attn_sample(zs, caches, slot, batch, t_real):
    n_in = zs.shape[1]
    z5 = zs.reshape(batch, SAMPLE_PAD, n_in // A_HEAD_DIM, A_HEAD_DIM)
    in_specs = [pl.BlockSpec((1, SAMPLE_PAD, n_in // A_HEAD_DIM, A_HEAD_DIM), lambda b, t: (b, 0, 0, 0))]
    args = [z5]
    for g, (win, dil) in enumerate(A_GROUPS):
        n_a = caches[g].shape[0]
        cv = caches[g].reshape(n_a * batch, win // dil, dil * 2 * A_HEADS, A_HEAD_DIM)
        if dil == 1:
            idx = lambda b, t: (slot * batch + b, 0, 0, 0)
        else:
            idx = lambda b, t: (slot * batch + b, 0, t, 0)
        in_specs.append(pl.BlockSpec((1, A_BAND, 2 * A_HEADS, A_HEAD_DIM), idx))
        args.append(cv)
    y = pl.pallas_call(
        _attn_sample_body,
        grid=(batch, t_real),
        in_specs=in_specs,
        out_specs=pl.BlockSpec((1, 1, A_HEADS, A_HEAD_DIM), lambda b, t: (b, t, 0, 0)),
        out_shape=jax.ShapeDtypeStruct((batch, t_real, A_HEADS, A_HEAD_DIM), F32),
        compiler_params=_params(("arbitrary", "arbitrary")),
        name="attn_sample",
    )(*args)
    return y.reshape(batch, t_real, A_HEADS * A_HEAD_DIM)


def _kv_rows_body(*refs, n_layers):
    out_ref = refs[-1]
    layer = pl.program_id(0)
    for l in range(n_layers):
        k_ref, v_ref = refs[2 * l], refs[2 * l + 1]

        @pl.when(layer == l)
        def _():
            sub = 8
            for base, src in ((0, k_ref), (A_HEADS, v_ref)):
                for c0 in range(0, A_HEADS, sub):
                    cols = [src[:, (c0 + c) * A_HEAD_DIM:(c0 + c + 1) * A_HEAD_DIM] for c in range(sub)]
                    out_ref[:, base + c0:base + c0 + sub, :] = jnp.swapaxes(jnp.stack(cols, axis=0), 0, 1)


def _kv_rows(zs, g, batch, seq, keep):
    n_layers = len(zs)
    width = A_HEADS * A_HEAD_DIM
    tt = min(keep, 256)
    first = (seq - keep) // tt
    in_specs, args = [], []
    for l in range(n_layers):
        for c in (1, 2):
            def idx(s, b, i, l=l, c=c):
                return (jnp.where(s == l, b * (seq // tt) + first + i, 0), 3 * g + c)
            in_specs.append(pl.BlockSpec((tt, width), idx))
            args.append(zs[l])
    out = pl.pallas_call(
        functools.partial(_kv_rows_body, n_layers=n_layers),
        grid=(n_layers, batch, keep // tt),
        in_specs=in_specs,
        out_specs=pl.BlockSpec((None, None, tt, 2 * A_HEADS, A_HEAD_DIM), lambda s, b, i: (s, b, i, 0, 0)),
        out_shape=jax.ShapeDtypeStruct((n_layers, batch, keep, 2 * A_HEADS, A_HEAD_DIM), F32),
        compiler_params=_params(("arbitrary", "arbitrary", "arbitrary")),
        name="kv_rows_g%d" % g,
    )(*args)
    return out.reshape(n_layers, batch, keep, 2, A_HEADS, A_HEAD_DIM)


def _tn(a, b):
    return lax.dot_general(a, b, (((0,), (0,)), ((), ())), preferred_element_type=F32)


def _retention_body(*refs, has_state):
    if has_state:
        q_ref, k_ref, v_ref, gate_ref, dec_ref, xi_ref, kd_ref, sd_ref, gn_ref, s0_ref, y_ref, st_ref = refs
    else:
        q_ref, k_ref, v_ref, gate_ref, dec_ref, xi_ref, kd_ref, sd_ref, gn_ref, y_ref, st_ref = refs
    c = pl.program_id(1)

    @pl.when(c == 0)
    def _():
        if has_state:
            st_ref[...] = s0_ref[...]
        else:
            st_ref[...] = jnp.zeros_like(st_ref)

    def head(h):
        ks = slice(h * B_DK, (h + 1) * B_DK)
        vs = slice(h * B_DV, (h + 1) * B_DV)
        qb = q_ref[:, ks].astype(BF16)
        kf = k_ref[:, ks]
        vb = v_ref[:, vs].astype(BF16)
        state = st_ref[0, h]
        scores = _nt(qb, kf.astype(BF16))
        cross = _nn(qb, state.astype(BF16))
        k_dec = (kf * kd_ref[h]).astype(BF16)
        st_ref[0, h] = sd_ref[h] * state + _tn(k_dec, vb)
        yield
        o = _nn((scores * dec_ref[h]).astype(BF16), vb) + cross * xi_ref[h]
        yield
        mu = jnp.mean(o, axis=-1, keepdims=True)
        var = jnp.mean(jnp.square(o - mu), axis=-1, keepdims=True)
        yv = (o - mu) * lax.rsqrt(var + NORM_EPS) * gn_ref[h]
        y_ref[:, vs] = (yv * _silu(gate_ref[:, vs])).astype(y_ref.dtype)

    _round_robin([head(h) for h in range(B_HEADS)])


def _retention(z, batch, chunk, t_real, gn, state0):
    rows = z.shape[0]
    nc = rows // (batch * chunk)
    lg = jnp.log(1.0 - 2.0 ** (-5.0 - jnp.arange(B_HEADS, dtype=F32)))[:, None, None]
    idx = jnp.arange(chunk, dtype=F32)
    diff = idx[:, None] - idx[None, :]
    dec = jnp.where(diff >= 0, jnp.exp(lg * jnp.maximum(diff, 0.0)), 0.0)
    xi = jnp.exp(lg * (idx[None, :, None] + 1.0))
    live = idx[None, :, None] < t_real
    kd = jnp.where(live, jnp.exp(lg * jnp.where(live, t_real - 1.0 - idx[None, :, None], 0.0)), 0.0)
    sd = jnp.exp(lg * t_real)
    n_k, n_v = B_HEADS * B_DK, B_HEADS * B_DV
    const = lambda b, c: (0, 0, 0)
    in_specs = [pl.BlockSpec((chunk, n_k), lambda b, c: (b * nc + c, 0)),
                pl.BlockSpec((chunk, n_k), lambda b, c: (b * nc + c, 1)),
                pl.BlockSpec((chunk, n_v), lambda b, c: (b * nc + c, 2 * n_k // n_v)),
                pl.BlockSpec((chunk, n_v), lambda b, c: (b * nc + c, 2 * n_k // n_v + 1)),
                pl.BlockSpec((B_HEADS, chunk, chunk), const),
                pl.BlockSpec((B_HEADS, chunk, 1), const),
                pl.BlockSpec((B_HEADS, chunk, 1), const),
                pl.BlockSpec((B_HEADS, 1, 1), const),
                pl.BlockSpec((B_HEADS, 1, B_DV), const)]
    args = [z, z, z, z, dec, xi, kd, sd, gn.reshape(B_HEADS, 1, B_DV)]
    state_spec = pl.BlockSpec((1, B_HEADS, B_DK, B_DV), lambda b, c: (b, 0, 0, 0))
    if state0 is not None:
        in_specs.append(state_spec)
        args.append(state0)
    return pl.pallas_call(
        functools.partial(_retention_body, has_state=state0 is not None),
        grid=(batch, nc),
        in_specs=in_specs,
        out_specs=[pl.BlockSpec((chunk, n_v), lambda b, c: (b * nc + c, 0)), state_spec],
        out_shape=[jax.ShapeDtypeStruct((rows, n_v), BF16),
                   jax.ShapeDtypeStruct((batch, B_HEADS, B_DK, B_DV), F32)],
        compiler_params=_params(("arbitrary", "arbitrary")),
        name="retention",
    )(*args)


def _log_sigmoid(x):
    return jnp.minimum(x, 0.0) - jnp.log(1.0 + jnp.exp(-jnp.abs(x)))


def _gla_head(q, k, v, lr, wg, bg, state, consts, chunk, sub, t_sub):
    nsub = chunk // sub
    tri, pair_live, pair_col, pick_t = consts
    pre = _nn_hi(lr, wg)
    yield
    log_a = _log_sigmoid(pre + bg) * (LOG2_E / C_TAU)
    bl = _nn_exact_lhs(tri, log_a)
    yield

    def block_rows(vals):
        return jnp.concatenate([jnp.broadcast_to(x, (sub, C_DK)) for x in vals], axis=0)

    tot = [bl[j * sub + t_sub - 1:j * sub + t_sub, :] for j in range(nsub)]
    beta = [jnp.zeros((1, C_DK), F32)]
    for j in range(nsub):
        beta.append(beta[-1] + tot[j])
    qs = q * (C_DK ** -0.5)
    q_in = qs * jnp.exp2(bl)
    k_out = k * jnp.exp2(block_rows(tot) - bl)
    if t_sub < sub:
        k_out = jnp.where(lax.broadcasted_iota(jnp.int32, (chunk, 1), 0) % sub < t_sub, k_out, 0.0)
    vb = v.astype(BF16)

    o = _nn((q_in * block_rows([jnp.exp2(x) for x in beta[:nsub]])).astype(BF16), state.astype(BF16))
    yield

    def rep_rows(x):
        return jnp.concatenate([jnp.broadcast_to(x[t:t + 1, :], (sub, C_DK)) for t in range(sub)], axis=0)

    def tile_rows(x):
        return jnp.concatenate([x] * sub, axis=0)

    a_rows = []
    for i in range(nsub):
        r = slice(i * sub, (i + 1) * sub)
        pair = rep_rows(qs[r]) * tile_rows(k[r]) * jnp.exp2(rep_rows(bl[r]) - tile_rows(bl[r]))
        att = jnp.where(pair_live, jnp.sum(pair, axis=1, keepdims=True), 0.0)
        placed = jnp.where(pair_col == i * sub, att, 0.0).astype(BF16)
        a_i = _nn(pick_t, placed)
        if i > 0:
            between = [jnp.broadcast_to(jnp.exp2(beta[i] - beta[j + 1]), (sub, C_DK)) for j in range(i)]
            between.append(jnp.zeros(((nsub - i) * sub, C_DK), F32))
            a_i += _nt(q_in[r].astype(BF16), (k_out * jnp.concatenate(between, axis=0)).astype(BF16))
        a_rows.append(a_i)
        if i % 2 == 1:
            yield
    a = a_rows[0] if nsub == 1 else jnp.concatenate(a_rows, axis=0)
    o += _nn(a.astype(BF16), vb)
    yield
    k_end = k_out * block_rows([jnp.exp2(beta[nsub] - beta[j + 1]) for j in range(nsub)])
    d_col = jnp.broadcast_to(jnp.exp2(beta[nsub]), (8, C_DK)).T[:, 0:1]
    return o, d_col * state + _tn(k_end.astype(BF16), vb)


def _gla_consts(chunk, sub):
    ri = lax.broadcasted_iota(jnp.int32, (chunk, chunk), 0)
    ci = lax.broadcasted_iota(jnp.int32, (chunk, chunk), 1)
    tri = jnp.logical_and(ri // sub == ci // sub, ci <= ri).astype(BF16)
    pj = lax.broadcasted_iota(jnp.int32, (sub * sub, 1), 0)
    pair_live = pj // sub >= pj % sub
    pc = lax.broadcasted_iota(jnp.int32, (sub * sub, chunk), 1)
    ps = lax.broadcasted_iota(jnp.int32, (sub * sub, chunk), 0) % sub
    et = lax.broadcasted_iota(jnp.int32, (sub, sub * sub), 0)
    ej = lax.broadcasted_iota(jnp.int32, (sub, sub * sub), 1)
    pick_t = (ej // sub == et).astype(BF16)
    return tri, pair_live, pc - ps, pick_t


def _gla_body(*refs, has_state, chunk, sub, t_sub):
    consts = _gla_consts(chunk, sub)
    if has_state:
        q_ref, k_ref, v_ref, gate_ref, lr_ref, wg_ref, bg_ref, gn_ref, s0_ref, y_ref, st_ref = refs
    else:
        q_ref, k_ref, v_ref, gate_ref, lr_ref, wg_ref, bg_ref, gn_ref, y_ref, st_ref = refs
    c = pl.program_id(1)

    @pl.when(c == 0)
    def _():
        if has_state:
            st_ref[...] = s0_ref[...]
        else:
            st_ref[...] = jnp.zeros_like(st_ref)

    lr = lr_ref[:, 0:C_RANK]
    heads = []
    for h in range(C_HEADS):
        ks = slice(h * C_DK, (h + 1) * C_DK)
        vs = slice(h * C_DV, (h + 1) * C_DV)
        heads.append(_gla_head(q_ref[:, ks], k_ref[:, ks], v_ref[:, vs], lr, wg_ref[:, ks], bg_ref[:, ks],
                               st_ref[0, h], consts, chunk, sub, t_sub))
    for h, (o, new_state) in enumerate(_round_robin(heads)):
        vs = slice(h * C_DV, (h + 1) * C_DV)
        st_ref[0, h] = new_state
        yv = o * lax.rsqrt(jnp.mean(o * o, axis=-1, keepdims=True) + NORM_EPS) * gn_ref[:, vs]
        y_ref[:, vs] = (yv * _silu(gate_ref[:, vs])).astype(y_ref.dtype)


def _gla(z, z_lr, batch, chunk, t_real, wg3, bg3, gn3, slot, state0):
    rows = z.shape[0]
    nc = rows // (batch * chunk)
    sub, t_sub = (C_SUB, C_SUB) if t_real == chunk else (chunk, t_real)
    n_k, n_v = C_HEADS * C_DK, C_HEADS * C_DV
    n_slots = wg3.shape[0]
    in_specs = [pl.BlockSpec((chunk, n_k), lambda b, c: (b * nc + c, 0)),
                pl.BlockSpec((chunk, n_k), lambda b, c: (b * nc + c, 1)),
                pl.BlockSpec((chunk, n_v), lambda b, c: (b * nc + c, 2 * n_k // n_v)),
                pl.BlockSpec((chunk, n_v), lambda b, c: (b * nc + c, 2 * n_k // n_v + 1)),
                pl.BlockSpec((chunk, LANES), lambda b, c: (b * nc + c, 0)),
                pl.BlockSpec((None, C_RANK, n_k), lambda b, c: (slot, 0, 0)),
                pl.BlockSpec((None, 1, n_k), lambda b, c: (slot, 0, 0)),
                pl.BlockSpec((None, 1, n_v), lambda b, c: (slot, 0, 0))]
    args = [z, z, z, z, z_lr, wg3, bg3.reshape(n_slots, 1, n_k), gn3.reshape(n_slots, 1, n_v)]
    state_spec = pl.BlockSpec((1, C_HEADS, C_DK, C_DV), lambda b, c: (b, 0, 0, 0))
    if state0 is not None:
        in_specs.append(state_spec)
        args.append(state0)
    return pl.pallas_call(
        functools.partial(_gla_body, has_state=state0 is not None, chunk=chunk, sub=sub, t_sub=t_sub),
        grid=(batch, nc),
        in_specs=in_specs,
        out_specs=[pl.BlockSpec((chunk, n_v), lambda b, c: (b * nc + c, 0)), state_spec],
        out_shape=[jax.ShapeDtypeStruct((rows, n_v), BF16),
                   jax.ShapeDtypeStruct((batch, C_HEADS, C_DK, C_DV), F32)],
        compiler_params=_params(("arbitrary", "arbitrary")),
        name="gla",
    )(*args)


def _rot_tables_a(pos):
    half = A_ROT // 2
    inv_freq = A_ROPE_THETA ** (-jnp.arange(half, dtype=F32) / half)
    ang = pos.astype(F32)[:, None] * inv_freq[None, :]
    cos, sin = jnp.cos(ang), jnp.sin(ang)
    n = pos.shape[0]
    rest = A_HEAD_DIM - A_ROT
    c = jnp.concatenate([cos, cos, jnp.ones((n, rest), F32)], axis=1)
    s = jnp.concatenate([-sin, sin, jnp.zeros((n, rest), F32)], axis=1)
    return c, s


def _rot_tables_b(pos):
    half = B_DK // 2
    inv_freq = B_ROPE_THETA ** (-jnp.arange(half, dtype=F32) / half)
    ang = pos.astype(F32)[:, None] * inv_freq[None, :]
    return jnp.cos(ang), jnp.sin(ang)


def kernel(x_prompt, x_sample, cache_a_kv1, cache_a_kv2, cache_a_kv3, state_b, state_c, norm_g, final_g,
           w_in_a, w_out_a, w_in_b, gn_b, w_out_b, w_in_c, w_gate2_c, b_gate_c, gn_c, w_out_c):
    bp, seq, d = x_prompt.shape
    bs, t_real, _ = x_sample.shape
    caches_a = (cache_a_kv1, cache_a_kv2, cache_a_kv3)
    width = A_HEADS * A_HEAD_DIM

    xp = x_prompt.reshape(bp * seq, d)
    xs = jnp.pad(x_sample, ((0, 0), (0, SAMPLE_PAD - t_real), (0, 0))).reshape(bs * SAMPLE_PAD, d)
    ms = xs.shape[0]
    tm_p = 1024

    pos_p = jnp.arange(seq)
    pos_s = jnp.tile(PAST_LEN + jnp.arange(SAMPLE_PAD), bs)
    tabs_a_p, tabs_a_s = _rot_tables_a(pos_p), _rot_tables_a(pos_s)
    tabs_b_p, tabs_b_s = _rot_tables_b(pos_p), _rot_tables_b(pos_s)

    hp = _rmsnorm(xp, norm_g[0], BF16)
    hs = _rmsnorm(xs, norm_g[0], BF16)
    w_out_a, w_out_b, w_out_c = _cast_bf16(w_out_a), _cast_bf16(w_out_b), _cast_bf16(w_out_c)

    a_z_p, a_rows_s = [], []
    b_states_p, b_states_s, c_states_p, c_states_s = [], [], [], []
    n_layers = len(LAYER_KINDS)
    for i in range(n_layers):
        kind, slot = LAYER_KINDS[i], LAYER_SLOTS[i]
        last = i == n_layers - 1
        g_next = final_g if last else norm_g[i + 1]
        h_dtype = F32 if last else BF16
        if kind == 0:
            n_in = w_in_a.shape[2]
            zp, zs = _proj(hp, hs, w_in_a, slot, n_in, "a", tabs_a_p, tabs_a_s, seq // tm_p, tm_p, 1024)
            yp = _attn_prompt(zp, bp, seq)
            ys = _attn_sample(zs, caches_a, slot, bs, t_real)
            ys = jnp.pad(ys, ((0, 0), (0, SAMPLE_PAD - t_real), (0, 0))).reshape(ms, width).astype(BF16)
            zs3 = zs.reshape(bs, SAMPLE_PAD, n_in)
            rows_s = []
            for g in range(len(A_GROUPS)):
                lo = (3 * g + 1) * width
                rows_s.append(zs3[:, :t_real, lo:lo + 2 * width].reshape(bs, t_real, 2, A_HEADS, A_HEAD_DIM))
            a_z_p.append(zp)
            a_rows_s.append(rows_s)
            w_out = w_out_a
        elif kind == 1:
            n_in = w_in_b.shape[2]
            zp, zs = _proj(hp, hs, w_in_b, slot, n_in, "b", tabs_b_p, tabs_b_s, seq // tm_p, tm_p, 1024)
            yp, st_p = _retention(zp, bp, min(B_CHUNK, seq), min(B_CHUNK, seq), gn_b[slot], None)
            ys, st_s = _retention(zs, bs, SAMPLE_PAD, t_real, gn_b[slot], state_b[slot])
            b_states_p.append(st_p)
            b_states_s.append(st_s)
            w_out = w_out_b
        else:
            n_main = 2 * C_HEADS * C_DK + 2 * C_HEADS * C_DV
            w_lr = jnp.pad(w_in_c[slot][:, n_main:], ((0, 0), (0, LANES - C_RANK)))[None]
            zp, zs = _proj(hp, hs, w_in_c, slot, n_main, "c", (), (), 1, tm_p, 1024)
            zp_lr, zs_lr = _proj(hp, hs, w_lr, 0, LANES, "c", (), (), 1, tm_p, LANES)
            yp, st_p = _gla(zp, zp_lr, bp, min(C_CHUNK, seq), min(C_CHUNK, seq),
                            w_gate2_c, b_gate_c, gn_c, slot, None)
            ys, st_s = _gla(zs, zs_lr, bs, SAMPLE_PAD, t_real, w_gate2_c, b_gate_c, gn_c, slot, state_c[slot])
            c_states_p.append(st_p)
            c_states_s.append(st_s)
            w_out = w_out_c
        xp, hp = _outproj(yp, w_out, slot, xp, g_next, h_dtype, 512)
        xs, hs = _outproj(ys, w_out, slot, xs, g_next, h_dtype, ms)

    y_prompt = hp.reshape(bp, seq, d)
    y_sample = hs.reshape(bs, SAMPLE_PAD, d)[:, :t_real]
    stack = lambda rows, g: jnp.stack([r[g] for r in rows])
    kv_p = [_kv_rows(a_z_p, g, bp, seq, min(win, seq)) for g, (win, _) in enumerate(A_GROUPS)]
    return (y_prompt, y_sample,
            kv_p[0], kv_p[1], kv_p[2],
            jnp.stack(b_states_p), jnp.stack(c_states_p),
            stack(a_rows_s, 0), stack(a_rows_s, 1), stack(a_rows_s, 2),
            jnp.stack(b_states_s), jnp.stack(c_states_s))
```

```python
import functools

import jax
import jax.numpy as jnp
from jax import lax
from jax.experimental import pallas as pl
from jax.experimental.pallas import tpu as pltpu

F32 = jnp.float32
BF16 = jnp.bfloat16

PAST_LEN = 16384
NORM_EPS = 1e-6
A_GROUPS = ((128, 1), (512, 4), (2048, 16))
A_HEADS = 16
A_HEAD_DIM = 128
A_ROT = A_HEAD_DIM // 4
A_ROPE_THETA = 500000.0
A_BAND = 128
B_HEADS = 8
B_DK = 256
B_DV = 512
B_CHUNK = 128
B_ROPE_THETA = 10000.0
C_HEADS = 4
C_DK = 256
C_DV = 512
C_RANK = 16
C_TAU = 16.0
C_CHUNK = 64
C_SUB = 8
LAYER_KINDS = (0, 1, 2, 0)
LAYER_SLOTS = (0, 0, 0, 1)

SAMPLE_PAD = 16
LANES = 128
MXU_N = 256
VMEM_LIMIT = 56 * 1024 * 1024


def _params(sem, vmem=VMEM_LIMIT):
    return pltpu.CompilerParams(dimension_semantics=sem, vmem_limit_bytes=vmem)


def _silu(g):
    return g * (1.0 / (1.0 + jnp.exp(-g)))


def _nt(a, b):
    return lax.dot_general(a, b, (((1,), (1,)), ((), ())), preferred_element_type=F32)


def _nn(a, b):
    return jnp.dot(a, b, preferred_element_type=F32)


def _split(x):
    hi = x.astype(BF16)
    lo = (x - hi.astype(F32)).astype(BF16)
    return hi, lo


def _split3(x):
    hi = x.astype(BF16)
    r = x - hi.astype(F32)
    mid = r.astype(BF16)
    lo = (r - mid.astype(F32)).astype(BF16)
    return hi, mid, lo


def _nn_exact_lhs(sel, x):
    hi, mid, lo = _split3(x)
    return _nn(sel, hi) + _nn(sel, mid) + _nn(sel, lo)


def _round_robin(gens):
    gens = list(gens)
    results = [None] * len(gens)
    live = list(range(len(gens)))
    while live:
        for n in list(live):
            try:
                next(gens[n])
            except StopIteration as stop:
                results[n] = stop.value
                live.remove(n)
    return results


def _staggered(gens):
    pending = list(gens)
    active = []
    while pending or active:
        if pending:
            active.append(pending.pop(0))
        for g in list(active):
            try:
                next(g)
            except StopIteration:
                active.remove(g)


def _nn_hi(a, b):
    ah, al = _split(a)
    bh, bl = _split(b)
    return _nn(ah, bh) + _nn(ah, bl) + _nn(al, bh)


def _rmsnorm_body(x_ref, g_ref, o_ref):
    x = x_ref[...]
    ms = jnp.mean(x * x, axis=-1, keepdims=True)
    o_ref[...] = (x * lax.rsqrt(ms + NORM_EPS) * g_ref[...]).astype(o_ref.dtype)


def _rmsnorm(x, g, out_dtype):
    m, d = x.shape
    tm = min(m, 512)
    return pl.pallas_call(
        _rmsnorm_body,
        grid=(m // tm,),
        in_specs=[pl.BlockSpec((tm, d), lambda i: (i, 0)),
                  pl.BlockSpec((1, d), lambda i: (0, 0))],
        out_specs=pl.BlockSpec((tm, d), lambda i: (i, 0)),
        out_shape=jax.ShapeDtypeStruct((m, d), out_dtype),
        compiler_params=_params(("arbitrary",)),
        name="rmsnorm",
    )(x, g.reshape(1, d))


N_ROT_TABLES = {"a": 2, "b": 2, "c": 0}


ROT_ROWS = 128


def _rot_a(src_ref, cs, width, z_ref, tabs, j, tn):
    c_ref, s_ref = tabs
    half = A_ROT // 2
    rows = min(ROT_ROWS, src_ref.shape[0])
    first_half = lax.broadcasted_iota(jnp.int32, (rows, A_HEAD_DIM), 1) < half
    for rb in range(0, src_ref.shape[0], rows):
        rs = slice(rb, rb + rows)
        for o in range(cs, cs + width, A_HEAD_DIM):
            x = src_ref[rs, o:o + A_HEAD_DIM]
            x_hi = pltpu.roll(x, A_HEAD_DIM - half, 1)
            x_lo = pltpu.roll(x, half, 1)
            z_ref[rs, o:o + A_HEAD_DIM] = x * c_ref[rs, :] + jnp.where(first_half, x_hi, x_lo) * s_ref[rs, :]


def _rot_b(src_ref, cs, width, z_ref, tabs, j, tn):
    c_ref, s_ref = tabs
    scale = jnp.where(j * tn >= B_HEADS * B_DK, B_DK ** -0.5, 1.0).astype(F32)
    half = B_DK // 2
    rows = min(ROT_ROWS, src_ref.shape[0])
    for rb in range(0, src_ref.shape[0], rows):
        rs = slice(rb, rb + rows)
        cos, sin = c_ref[rs, :] * scale, s_ref[rs, :] * scale
        for o in range(cs, cs + width, B_DK):
            x1 = src_ref[rs, o:o + half]
            x2 = src_ref[rs, o + half:o + B_DK]
            z_ref[rs, o:o + half] = x1 * cos - x2 * sin
            z_ref[rs, o + half:o + B_DK] = x1 * sin + x2 * cos


def _proj_body(*refs, kind, tn, ni, n_steps, lagged):
    n_tab = N_ROT_TABLES[kind]
    h_ref, hs_ref, w_ref = refs[:3]
    tabs, tabs_s = refs[3:3 + n_tab], refs[3 + n_tab:3 + 2 * n_tab]
    z_ref, zs_ref, wb_ref = refs[3 + 2 * n_tab:6 + 2 * n_tab]
    s = pl.program_id(0)
    s_mm = jnp.minimum(s, n_steps - 1)
    j, i = s_mm // ni, s_mm % ni
    if lagged:
        acc_even, acc_odd = refs[6 + 2 * n_tab:]
        j_prev = jnp.maximum(s - 1, 0) // ni
    else:
        j_prev = j
    epilogue = {"a": _rot_a, "b": _rot_b, "c": None}[kind]

    def is_rot(jj):
        if kind == "a":
            seg = (jj * tn) // (A_HEADS * A_HEAD_DIM)
            return jnp.logical_and(seg < 3 * len(A_GROUPS), seg % 3 < 2)
        return (jj * tn) < 2 * B_HEADS * B_DK

    slab = min(tn, MXU_N)

    def finish(src_ref, cs, o_ref, tables, jj, rot):
        if rot:
            epilogue(src_ref, cs, slab, o_ref, tables, jj, tn)
        elif src_ref is not o_ref:
            rows = min(ROT_ROWS, src_ref.shape[0])
            for rb in range(0, src_ref.shape[0], rows):
                o_ref[rb:rb + rows, cs:cs + slab] = src_ref[rb:rb + rows, cs:cs + slab]

    if lagged:
        @pl.when(s == 0)
        def _():
            acc_odd[...] = jnp.zeros_like(acc_odd)

    @pl.when(jnp.logical_and(i == 0, s < n_steps))
    def _():
        wb_ref[...] = w_ref[...].astype(BF16)

        def sample(rot):
            for cs in range(0, tn, slab):
                zs_ref[:, cs:cs + slab] = _nn(hs_ref[...], wb_ref[:, cs:cs + slab])
                finish(zs_ref, cs, zs_ref, tabs_s, j, rot)

        if epilogue is None:
            sample(False)
        else:
            pl.when(is_rot(j))(lambda: sample(True))
            pl.when(jnp.logical_not(is_rot(j)))(lambda: sample(False))

    def step(acc_new, acc_old, rot):
        for cs in range(0, tn, slab):
            acc_new[:, cs:cs + slab] = _nn(h_ref[...], wb_ref[:, cs:cs + slab])
            finish(acc_old, cs, z_ref, tabs, j_prev, rot)

    if lagged:
        even = s % 2 == 0
        variants = ((even, acc_even, acc_odd), (jnp.logical_not(even), acc_odd, acc_even))
    else:
        variants = ((None, z_ref, z_ref),)
    for on, acc_new, acc_old in variants:
        if epilogue is None:
            plain = functools.partial(step, acc_new, acc_old, False)
            plain() if on is None else pl.when(on)(plain)
        else:
            rot_prev = is_rot(j_prev)
            both = lambda a, b: b if a is None else jnp.logical_and(a, b)
            pl.when(both(on, rot_prev))(functools.partial(step, acc_new, acc_old, True))
            pl.when(both(on, jnp.logical_not(rot_prev)))(functools.partial(step, acc_new, acc_old, False))


def _proj(h, hs, w3, slot, n_out, kind, tabs, tabs_s, tab_blocks, tm, tn):
    m, k = h.shape
    ms = hs.shape[0]
    ni = m // tm
    n_steps = (n_out // tn) * ni
    lagged = kind == "a"
    mm = lambda s: jnp.minimum(s, n_steps - 1)
    prev = (lambda s: jnp.maximum(s - 1, 0)) if lagged else (lambda s: s)
    in_specs = [pl.BlockSpec((tm, k), lambda s: (mm(s) % ni, 0)),
                pl.BlockSpec((ms, k), lambda s: (0, 0)),
                pl.BlockSpec((None, k, tn), lambda s: (slot, 0, mm(s) // ni))]
    in_specs += [pl.BlockSpec((tm, LANES), lambda s: ((prev(s) % ni) % tab_blocks, 0)) for _ in tabs]
    in_specs += [pl.BlockSpec((ms, LANES), lambda s: (0, 0)) for _ in tabs_s]
    scratch = [pltpu.VMEM((k, tn), BF16)]
    if lagged:
        scratch += [pltpu.VMEM((tm, tn), F32), pltpu.VMEM((tm, tn), F32)]
    return pl.pallas_call(
        functools.partial(_proj_body, kind=kind, tn=tn, ni=ni, n_steps=n_steps, lagged=lagged),
        grid=(n_steps + 1 if lagged else n_steps,),
        in_specs=in_specs,
        out_specs=[pl.BlockSpec((tm, tn), lambda s: (prev(s) % ni, prev(s) // ni)),
                   pl.BlockSpec((ms, tn), lambda s: (0, mm(s) // ni))],
        out_shape=[jax.ShapeDtypeStruct((m, n_out), F32),
                   jax.ShapeDtypeStruct((ms, n_out), F32)],
        scratch_shapes=scratch,
        compiler_params=_params(("arbitrary",)),
        name="proj_" + kind,
    )(h, hs, w3, *tabs, *tabs_s)


def _cast_body(x_ref, o_ref):
    o_ref[...] = x_ref[...].astype(o_ref.dtype)


def _cast_bf16(w3):
    n, k, d = w3.shape
    rows = n * k
    tr = min(rows, 1024)
    out = pl.pallas_call(
        _cast_body,
        grid=(rows // tr,),
        in_specs=[pl.BlockSpec((tr, d), lambda i: (i, 0))],
        out_specs=pl.BlockSpec((tr, d), lambda i: (i, 0)),
        out_shape=jax.ShapeDtypeStruct((rows, d), BF16),
        compiler_params=_params(("arbitrary",)),
        name="cast_bf16",
    )(w3.reshape(rows, d))
    return out.reshape(n, k, d)


def _outproj_body(y_ref, w_ref, x_ref, g_ref, *out_refs):
    xn = x_ref[...] + _nn(y_ref[...], w_ref[...])
    if len(out_refs) == 2:
        out_refs[0][...] = xn
    ho_ref = out_refs[-1]
    ms = jnp.mean(xn * xn, axis=-1, keepdims=True)
    ho_ref[...] = (xn * lax.rsqrt(ms + NORM_EPS) * g_ref[...]).astype(ho_ref.dtype)


def _outproj(y, w3, slot, x, g, h_dtype, tm, want_x):
    m, kd = y.shape
    d = x.shape[1]
    row_spec = pl.BlockSpec((tm, d), lambda i: (i, 0))
    outs = pl.pallas_call(
        _outproj_body,
        grid=(m // tm,),
        in_specs=[pl.BlockSpec((tm, kd), lambda i: (i, 0)),
                  pl.BlockSpec((None, kd, d), lambda i: (slot, 0, 0), pipeline_mode=pl.Buffered(1)),
                  row_spec,
                  pl.BlockSpec((1, d), lambda i: (0, 0))],
        out_specs=[row_spec] * (2 if want_x else 1),
        out_shape=([jax.ShapeDtypeStruct((m, d), F32)] if want_x else []) + [jax.ShapeDtypeStruct((m, d), h_dtype)],
        compiler_params=_params(("arbitrary",)),
        name="outproj",
    )(y, w3, x, g.reshape(1, d))
    return (outs[0], outs[1]) if want_x else (None, outs[0])


LOG2_E = 1.4426950408889634


def _rows(start, size, stride):
    return pl.ds(start, size) if stride == 1 else pl.ds(start, size, stride=stride)


def _attn_prompt_body(*refs, seq):
    qkv_refs, gate_ref, bias_first_ref, bias_band_ref, y_ref = refs[:9], *refs[9:13]
    o_s, d_s, m_s = refs[13:]
    q_scale = (A_HEAD_DIM ** -0.5) * LOG2_E
    ones = jnp.ones((A_BAND, A_HEAD_DIM), BF16)
    for g, (_, dil) in enumerate(A_GROUPS):
        q_ref, k_ref, v_ref = qkv_refs[3 * g:3 * g + 3]

        def load_kv(rows, k_ref=k_ref, v_ref=v_ref):
            return k_ref[rows, :].astype(BF16), jnp.concatenate([v_ref[rows, :].astype(BF16), ones], axis=1)

        def chain(r, n0, n1, g=g, dil=dil, q_ref=q_ref, load_kv=load_kv):
            block_rows = lambda n: _rows(r + dil * A_BAND * n, A_BAND, dil)
            k_prev, v_prev = load_kv(block_rows(n0 - 1)) if n0 > 0 else (None, None)
            for n in range(n0, n1):
                rows = block_rows(n)
                q = (q_ref[rows, :] * q_scale).astype(BF16)
                k_cur, v_cur = load_kv(rows)
                if n == 0:
                    keys, vals, bias_ref = k_cur, v_cur, bias_first_ref
                else:
                    keys = jnp.concatenate([k_prev, k_cur], axis=0)
                    vals = jnp.concatenate([v_prev, v_cur], axis=0)
                    bias_ref = bias_band_ref
                k_prev, v_prev = k_cur, v_cur
                s = _nt(q, keys)
                yield
                s = s + bias_ref[...]
                mx = jnp.max(s, axis=-1, keepdims=True)
                od = _nn(jnp.exp2(s - mx).astype(BF16), vals)
                yield
                o_s[g, rows, :] = od[:, :A_HEAD_DIM]
                d_s[g, rows, :] = od[:, A_HEAD_DIM:]
                m_s[g, rows, :] = jnp.broadcast_to(mx, (A_BAND, A_HEAD_DIM))

        nb = seq // dil // A_BAND
        if dil == 1:
            chains = [chain(0, 0, nb // 2), chain(0, nb // 2, nb)]
        else:
            chains = [chain(r, 0, nb) for r in range(dil)]
        _staggered(chains)
    for tile in range(seq // A_BAND):
        rs = slice(tile * A_BAND, (tile + 1) * A_BAND)
        m1, m2, m3 = m_s[0, rs, :], m_s[1, rs, :], m_s[2, rs, :]
        mx = jnp.maximum(jnp.maximum(m1, m2), m3)
        f1, f2, f3 = jnp.exp2(m1 - mx), jnp.exp2(m2 - mx), jnp.exp2(m3 - mx)
        num = f1 * o_s[0, rs, :] + f2 * o_s[1, rs, :] + f3 * o_s[2, rs, :]
        den = f1 * d_s[0, rs, :] + f2 * d_s[1, rs, :] + f3 * d_s[2, rs, :]
        y_ref[rs, :] = (num / den * _silu(gate_ref[rs, :])).astype(y_ref.dtype)


def _attn_prompt(z, batch, seq):
    n_groups = len(A_GROUPS)
    in_specs = []
    for c in range(3 * n_groups + 1):
        in_specs.append(pl.BlockSpec((seq, A_HEAD_DIM), lambda b, h, c=c: (b, c * A_HEADS + h)))
    row = jnp.arange(A_BAND)[:, None]
    col = jnp.arange(2 * A_BAND)[None, :]
    bias_first = jnp.where(col[:, :A_BAND] <= row, 0.0, -jnp.inf).astype(F32)
    bias_band = jnp.where((col >= row) & (col <= row + A_BAND), 0.0, -jnp.inf).astype(F32)
    in_specs.append(pl.BlockSpec((A_BAND, A_BAND), lambda b, h: (0, 0)))
    in_specs.append(pl.BlockSpec((A_BAND, 2 * A_BAND), lambda b, h: (0, 0)))
    return pl.pallas_call(
        functools.partial(_attn_prompt_body, seq=seq),
        grid=(batch, A_HEADS),
        in_specs=in_specs,
        out_specs=pl.BlockSpec((seq, A_HEAD_DIM), lambda b, h: (b, h)),
        out_shape=jax.ShapeDtypeStruct((batch * seq, A_HEADS * A_HEAD_DIM), BF16),
        scratch_shapes=[pltpu.VMEM((n_groups, seq, A_HEAD_DIM), F32)] * 3,
        compiler_params=_params(("arbitrary", "arbitrary")),
        name="attn_prompt",
    )(*([z] * (3 * n_groups + 1)), bias_first, bias_band)


def _attn_sample_body(z_ref, c1_ref, c2_ref, c3_ref, y_ref):
    t = pl.program_id(1)
    scale = A_HEAD_DIM ** -0.5
    m_idx = lax.broadcasted_iota(jnp.int32, (A_BAND, A_HEADS, 1), 0)
    n_idx = lax.broadcasted_iota(jnp.int32, (SAMPLE_PAD, A_HEADS, 1), 0)
    caches = (c1_ref, c2_ref, c3_ref)
    outs, lses = [], []
    for g, (_, dil) in enumerate(A_GROUPS):
        base = 3 * g * A_HEADS
        q = z_ref[0, t, base:base + A_HEADS, :]
        kc = caches[g][0, :, 0:A_HEADS, :]
        vc = caches[g][0, :, A_HEADS:2 * A_HEADS, :]
        kn = z_ref[0, :, base + A_HEADS:base + 2 * A_HEADS, :]
        vn = z_ref[0, :, base + 2 * A_HEADS:base + 3 * A_HEADS, :]
        s_c = jnp.sum(kc * q[None], axis=-1, keepdims=True) * scale
        s_n = jnp.sum(kn * q[None], axis=-1, keepdims=True) * scale
        if dil == 1:
            s_c = jnp.where(m_idx >= t, s_c, -jnp.inf)
            s_n = jnp.where(n_idx <= t, s_n, -jnp.inf)
        else:
            s_n = jnp.where(n_idx == t, s_n, -jnp.inf)
        mx = jnp.maximum(jnp.max(s_c, axis=0), jnp.max(s_n, axis=0))
        p_c = jnp.exp(s_c - mx[None])
        p_n = jnp.exp(s_n - mx[None])
        den = jnp.sum(p_c, axis=0) + jnp.sum(p_n, axis=0)
        num = jnp.sum(p_c * vc, axis=0) + jnp.sum(p_n * vn, axis=0)
        outs.append(num / den)
        lses.append(mx + jnp.log(den))
    lmax = jnp.maximum(jnp.maximum(lses[0], lses[1]), lses[2])
    es = [jnp.exp(l - lmax) for l in lses]
    tot = es[0] + es[1] + es[2]
    o = (es[0] / tot) * outs[0] + (es[1] / tot) * outs[1] + (es[2] / tot) * outs[2]
    gate = z_ref[0, t, 3 * len(A_GROUPS) * A_HEADS:(3 * len(A_GROUPS) + 1) * A_HEADS, :]
    y_ref[0, 0] = o * _silu(gate)


def _attn_sample(zs, caches, slot, batch, t_real):
    n_in = zs.shape[1]
    z5 = zs.reshape(batch, SAMPLE_PAD, n_in // A_HEAD_DIM, A_HEAD_DIM)
    in_specs = [pl.BlockSpec((1, SAMPLE_PAD, n_in // A_HEAD_DIM, A_HEAD_DIM), lambda b, t: (b, 0, 0, 0))]
    args = [z5]
    for g, (win, dil) in enumerate(A_GROUPS):
        n_a = caches[g].shape[0]
        cv = caches[g].reshape(n_a * batch, win // dil, dil * 2 * A_HEADS, A_HEAD_DIM)
        if dil == 1:
            idx = lambda b, t: (slot * batch + b, 0, 0, 0)
        else:
            idx = lambda b, t: (slot * batch + b, 0, t, 0)
        in_specs.append(pl.BlockSpec((1, A_BAND, 2 * A_HEADS, A_HEAD_DIM), idx))
        args.append(cv)
    y = pl.pallas_call(
        _attn_sample_body,
        grid=(batch, t_real),
        in_specs=in_specs,
        out_specs=pl.BlockSpec((1, 1, A_HEADS, A_HEAD_DIM), lambda b, t: (b, t, 0, 0)),
        out_shape=jax.ShapeDtypeStruct((batch, t_real, A_HEADS, A_HEAD_DIM), F32),
        compiler_params=_params(("arbitrary", "arbitrary")),
        name="attn_sample",
    )(*args)
    return y.reshape(batch, t_real, A_HEADS * A_HEAD_DIM)


def _kv_rows_body(*refs, n_layers):
    out_ref = refs[-1]
    layer = pl.program_id(0)
    for l in range(n_layers):
        k_ref, v_ref = refs[2 * l], refs[2 * l + 1]

        @pl.when(layer == l)
        def _():
            sub = 8
            for base, src in ((0, k_ref), (A_HEADS, v_ref)):
                for c0 in range(0, A_HEADS, sub):
                    cols = [src[:, (c0 + c) * A_HEAD_DIM:(c0 + c + 1) * A_HEAD_DIM] for c in range(sub)]
                    out_ref[:, base + c0:base + c0 + sub, :] = jnp.swapaxes(jnp.stack(cols, axis=0), 0, 1)


def _kv_rows(zs, g, batch, seq, keep):
    n_layers = len(zs)
    width = A_HEADS * A_HEAD_DIM
    tt = min(keep, 256)
    first = (seq - keep) // tt
    in_specs, args = [], []
    for l in range(n_layers):
        for c in (1, 2):
            def idx(s, b, i, l=l, c=c):
                return (jnp.where(s == l, b * (seq // tt) + first + i, 0), 3 * g + c)
            in_specs.append(pl.BlockSpec((tt, width), idx))
            args.append(zs[l])
    out = pl.pallas_call(
        functools.partial(_kv_rows_body, n_layers=n_layers),
        grid=(n_layers, batch, keep // tt),
        in_specs=in_specs,
        out_specs=pl.BlockSpec((None, None, tt, 2 * A_HEADS, A_HEAD_DIM), lambda s, b, i: (s, b, i, 0, 0)),
        out_shape=jax.ShapeDtypeStruct((n_layers, batch, keep, 2 * A_HEADS, A_HEAD_DIM), F32),
        compiler_params=_params(("arbitrary", "arbitrary", "arbitrary")),
        name="kv_rows_g%d" % g,
    )(*args)
    return out.reshape(n_layers, batch, keep, 2, A_HEADS, A_HEAD_DIM)


def _tn(a, b):
    return lax.dot_general(a, b, (((0,), (0,)), ((), ())), preferred_element_type=F32)


def _retention_body(*refs, has_state):
    if has_state:
        q_ref, k_ref, v_ref, gate_ref, dec_ref, xi_ref, kd_ref, sd_ref, gn_ref, s0_ref, y_ref, st_ref = refs
    else:
        q_ref, k_ref, v_ref, gate_ref, dec_ref, xi_ref, kd_ref, sd_ref, gn_ref, y_ref, st_ref = refs
    c = pl.program_id(1)

    @pl.when(c == 0)
    def _():
        if has_state:
            st_ref[...] = s0_ref[...]
        else:
            st_ref[...] = jnp.zeros_like(st_ref)

    def head(h):
        ks = slice(h * B_DK, (h + 1) * B_DK)
        vs = slice(h * B_DV, (h + 1) * B_DV)
        qb = q_ref[:, ks].astype(BF16)
        kf = k_ref[:, ks]
        vb = v_ref[:, vs].astype(BF16)
        state = st_ref[0, h]
        scores = _nt(qb, kf.astype(BF16))
        cross = _nn(qb, state.astype(BF16))
        k_dec = (kf * kd_ref[h]).astype(BF16)
        st_ref[0, h] = sd_ref[h] * state + _tn(k_dec, vb)
        yield
        o = _nn((scores * dec_ref[h]).astype(BF16), vb) + cross * xi_ref[h]
        yield
        mu = jnp.mean(o, axis=-1, keepdims=True)
        var = jnp.mean(jnp.square(o - mu), axis=-1, keepdims=True)
        yv = (o - mu) * lax.rsqrt(var + NORM_EPS) * gn_ref[h]
        y_ref[:, vs] = (yv * _silu(gate_ref[:, vs])).astype(y_ref.dtype)

    _round_robin([head(h) for h in range(B_HEADS)])


def _retention(z, batch, chunk, t_real, gn, state0):
    rows = z.shape[0]
    nc = rows // (batch * chunk)
    lg = jnp.log(1.0 - 2.0 ** (-5.0 - jnp.arange(B_HEADS, dtype=F32)))[:, None, None]
    idx = jnp.arange(chunk, dtype=F32)
    diff = idx[:, None] - idx[None, :]
    dec = jnp.where(diff >= 0, jnp.exp(lg * jnp.maximum(diff, 0.0)), 0.0)
    xi = jnp.exp(lg * (idx[None, :, None] + 1.0))
    live = idx[None, :, None] < t_real
    kd = jnp.where(live, jnp.exp(lg * jnp.where(live, t_real - 1.0 - idx[None, :, None], 0.0)), 0.0)
    sd = jnp.exp(lg * t_real)
    n_k, n_v = B_HEADS * B_DK, B_HEADS * B_DV
    const = lambda b, c: (0, 0, 0)
    in_specs = [pl.BlockSpec((chunk, n_k), lambda b, c: (b * nc + c, 0)),
                pl.BlockSpec((chunk, n_k), lambda b, c: (b * nc + c, 1)),
                pl.BlockSpec((chunk, n_v), lambda b, c: (b * nc + c, 2 * n_k // n_v)),
                pl.BlockSpec((chunk, n_v), lambda b, c: (b * nc + c, 2 * n_k // n_v + 1)),
                pl.BlockSpec((B_HEADS, chunk, chunk), const),
                pl.BlockSpec((B_HEADS, chunk, 1), const),
                pl.BlockSpec((B_HEADS, chunk, 1), const),
                pl.BlockSpec((B_HEADS, 1, 1), const),
                pl.BlockSpec((B_HEADS, 1, B_DV), const)]
    args = [z, z, z, z, dec, xi, kd, sd, gn.reshape(B_HEADS, 1, B_DV)]
    state_spec = pl.BlockSpec((1, B_HEADS, B_DK, B_DV), lambda b, c: (b, 0, 0, 0))
    if state0 is not None:
        in_specs.append(state_spec)
        args.append(state0)
    return pl.pallas_call(
        functools.partial(_retention_body, has_state=state0 is not None),
        grid=(batch, nc),
        in_specs=in_specs,
        out_specs=[pl.BlockSpec((chunk, n_v), lambda b, c: (b * nc + c, 0)), state_spec],
        out_shape=[jax.ShapeDtypeStruct((rows, n_v), BF16),
                   jax.ShapeDtypeStruct((batch, B_HEADS, B_DK, B_DV), F32)],
        compiler_params=_params(("arbitrary", "arbitrary")),
        name="retention",
    )(*args)


def _log_sigmoid(x):
    return jnp.minimum(x, 0.0) - jnp.log(1.0 + jnp.exp(-jnp.abs(x)))


def _gla_head(q, k, v, lr, wg, bg, state, consts, chunk, sub, t_sub):
    nsub = chunk // sub
    tri, pair_live, pair_col, pick_t = consts
    pre = _nn_hi(lr, wg)
    yield
    log_a = _log_sigmoid(pre + bg) * (LOG2_E / C_TAU)
    bl = _nn_exact_lhs(tri, log_a)
    yield

    def block_rows(vals):
        return jnp.concatenate([jnp.broadcast_to(x, (sub, C_DK)) for x in vals], axis=0)

    tot = [bl[j * sub + t_sub - 1:j * sub + t_sub, :] for j in range(nsub)]
    beta = [jnp.zeros((1, C_DK), F32)]
    for j in range(nsub):
        beta.append(beta[-1] + tot[j])
    qs = q * (C_DK ** -0.5)
    q_in = qs * jnp.exp2(bl)
    k_out = k * jnp.exp2(block_rows(tot) - bl)
    if t_sub < sub:
        k_out = jnp.where(lax.broadcasted_iota(jnp.int32, (chunk, 1), 0) % sub < t_sub, k_out, 0.0)
    vb = v.astype(BF16)

    o = _nn((q_in * block_rows([jnp.exp2(x) for x in beta[:nsub]])).astype(BF16), state.astype(BF16))
    yield

    def rep_rows(x):
        return jnp.concatenate([jnp.broadcast_to(x[t:t + 1, :], (sub, C_DK)) for t in range(sub)], axis=0)

    def tile_rows(x):
        return jnp.concatenate([x] * sub, axis=0)

    a_rows = []
    for i in range(nsub):
        r = slice(i * sub, (i + 1) * sub)
        pair = rep_rows(qs[r]) * tile_rows(k[r]) * jnp.exp2(rep_rows(bl[r]) - tile_rows(bl[r]))
        att = jnp.where(pair_live, jnp.sum(pair, axis=1, keepdims=True), 0.0)
        placed = jnp.where(pair_col == i * sub, att, 0.0).astype(BF16)
        a_i = _nn(pick_t, placed)
        if i > 0:
            between = [jnp.broadcast_to(jnp.exp2(beta[i] - beta[j + 1]), (sub, C_DK)) for j in range(i)]
            between.append(jnp.zeros(((nsub - i) * sub, C_DK), F32))
            a_i += _nt(q_in[r].astype(BF16), (k_out * jnp.concatenate(between, axis=0)).astype(BF16))
        a_rows.append(a_i)
        if i % 2 == 1:
            yield
    a = a_rows[0] if nsub == 1 else jnp.concatenate(a_rows, axis=0)
    o += _nn(a.astype(BF16), vb)
    yield
    k_end = k_out * block_rows([jnp.exp2(beta[nsub] - beta[j + 1]) for j in range(nsub)])
    d_col = jnp.broadcast_to(jnp.exp2(beta[nsub]), (8, C_DK)).T[:, 0:1]
    return o, d_col * state + _tn(k_end.astype(BF16), vb)


def _gla_consts(chunk, sub):
    ri = lax.broadcasted_iota(jnp.int32, (chunk, chunk), 0)
    ci = lax.broadcasted_iota(jnp.int32, (chunk, chunk), 1)
    tri = jnp.logical_and(ri // sub == ci // sub, ci <= ri).astype(BF16)
    pj = lax.broadcasted_iota(jnp.int32, (sub * sub, 1), 0)
    pair_live = pj // sub >= pj % sub
    pc = lax.broadcasted_iota(jnp.int32, (sub * sub, chunk), 1)
    ps = lax.broadcasted_iota(jnp.int32, (sub * sub, chunk), 0) % sub
    et = lax.broadcasted_iota(jnp.int32, (sub, sub * sub), 0)
    ej = lax.broadcasted_iota(jnp.int32, (sub, sub * sub), 1)
    pick_t = (ej // sub == et).astype(BF16)
    return tri, pair_live, pc - ps, pick_t


def _gla_body(*refs, has_state, chunk, sub, t_sub):
    consts = _gla_consts(chunk, sub)
    if has_state:
        q_ref, k_ref, v_ref, gate_ref, lr_ref, wg_ref, bg_ref, gn_ref, s0_ref, y_ref, st_ref = refs
    else:
        q_ref, k_ref, v_ref, gate_ref, lr_ref, wg_ref, bg_ref, gn_ref, y_ref, st_ref = refs
    c = pl.program_id(1)

    @pl.when(c == 0)
    def _():
        if has_state:
            st_ref[...] = s0_ref[...]
        else:
            st_ref[...] = jnp.zeros_like(st_ref)

    lr = lr_ref[:, 0:C_RANK]
    heads = []
    for h in range(C_HEADS):
        ks = slice(h * C_DK, (h + 1) * C_DK)
        vs = slice(h * C_DV, (h + 1) * C_DV)
        heads.append(_gla_head(q_ref[:, ks], k_ref[:, ks], v_ref[:, vs], lr, wg_ref[:, ks], bg_ref[:, ks],
                               st_ref[0, h], consts, chunk, sub, t_sub))
    for h, (o, new_state) in enumerate(_round_robin(heads)):
        vs = slice(h * C_DV, (h + 1) * C_DV)
        st_ref[0, h] = new_state
        yv = o * lax.rsqrt(jnp.mean(o * o, axis=-1, keepdims=True) + NORM_EPS) * gn_ref[:, vs]
        y_ref[:, vs] = (yv * _silu(gate_ref[:, vs])).astype(y_ref.dtype)


def _gla(z, z_lr, batch, chunk, t_real, wg3, bg3, gn3, slot, state0):
    rows = z.shape[0]
    nc = rows // (batch * chunk)
    sub, t_sub = (C_SUB, C_SUB) if t_real == chunk else (chunk, t_real)
    n_k, n_v = C_HEADS * C_DK, C_HEADS * C_DV
    n_slots = wg3.shape[0]
    in_specs = [pl.BlockSpec((chunk, n_k), lambda b, c: (b * nc + c, 0)),
                pl.BlockSpec((chunk, n_k), lambda b, c: (b * nc + c, 1)),
                pl.BlockSpec((chunk, n_v), lambda b, c: (b * nc + c, 2 * n_k // n_v)),
                pl.BlockSpec((chunk, n_v), lambda b, c: (b * nc + c, 2 * n_k // n_v + 1)),
                pl.BlockSpec((chunk, LANES), lambda b, c: (b * nc + c, 0)),
                pl.BlockSpec((None, C_RANK, n_k), lambda b, c: (slot, 0, 0)),
                pl.BlockSpec((None, 1, n_k), lambda b, c: (slot, 0, 0)),
                pl.BlockSpec((None, 1, n_v), lambda b, c: (slot, 0, 0))]
    args = [z, z, z, z, z_lr, wg3, bg3.reshape(n_slots, 1, n_k), gn3.reshape(n_slots, 1, n_v)]
    state_spec = pl.BlockSpec((1, C_HEADS, C_DK, C_DV), lambda b, c: (b, 0, 0, 0))
    if state0 is not None:
        in_specs.append(state_spec)
        args.append(state0)
    return pl.pallas_call(
        functools.partial(_gla_body, has_state=state0 is not None, chunk=chunk, sub=sub, t_sub=t_sub),
        grid=(batch, nc),
        in_specs=in_specs,
        out_specs=[pl.BlockSpec((chunk, n_v), lambda b, c: (b * nc + c, 0)), state_spec],
        out_shape=[jax.ShapeDtypeStruct((rows, n_v), BF16),
                   jax.ShapeDtypeStruct((batch, C_HEADS, C_DK, C_DV), F32)],
        compiler_params=_params(("arbitrary", "arbitrary")),
        name="gla",
    )(*args)


def _rot_tables_a(pos):
    half = A_ROT // 2
    inv_freq = A_ROPE_THETA ** (-jnp.arange(half, dtype=F32) / half)
    ang = pos.astype(F32)[:, None] * inv_freq[None, :]
    cos, sin = jnp.cos(ang), jnp.sin(ang)
    n = pos.shape[0]
    rest = A_HEAD_DIM - A_ROT
    c = jnp.concatenate([cos, cos, jnp.ones((n, rest), F32)], axis=1)
    s = jnp.concatenate([-sin, sin, jnp.zeros((n, rest), F32)], axis=1)
    return c, s


def _rot_tables_b(pos):
    half = B_DK // 2
    inv_freq = B_ROPE_THETA ** (-jnp.arange(half, dtype=F32) / half)
    ang = pos.astype(F32)[:, None] * inv_freq[None, :]
    return jnp.cos(ang), jnp.sin(ang)


def kernel(x_prompt, x_sample, cache_a_kv1, cache_a_kv2, cache_a_kv3, state_b, state_c, norm_g, final_g,
           w_in_a, w_out_a, w_in_b, gn_b, w_out_b, w_in_c, w_gate2_c, b_gate_c, gn_c, w_out_c):
    bp, seq, d = x_prompt.shape
    bs, t_real, _ = x_sample.shape
    caches_a = (cache_a_kv1, cache_a_kv2, cache_a_kv3)
    width = A_HEADS * A_HEAD_DIM

    xp = x_prompt.reshape(bp * seq, d)
    xs = jnp.pad(x_sample, ((0, 0), (0, SAMPLE_PAD - t_real), (0, 0))).reshape(bs * SAMPLE_PAD, d)
    ms = xs.shape[0]
    tm_p = 1024

    pos_p = jnp.arange(seq)
    pos_s = jnp.tile(PAST_LEN + jnp.arange(SAMPLE_PAD), bs)
    tabs_a_p, tabs_a_s = _rot_tables_a(pos_p), _rot_tables_a(pos_s)
    tabs_b_p, tabs_b_s = _rot_tables_b(pos_p), _rot_tables_b(pos_s)

    hp = _rmsnorm(xp, norm_g[0], BF16)
    hs = _rmsnorm(xs, norm_g[0], BF16)
    w_out_a, w_out_b, w_out_c = _cast_bf16(w_out_a), _cast_bf16(w_out_b), _cast_bf16(w_out_c)

    a_z_p, a_rows_s = [], []
    b_states_p, b_states_s, c_states_p, c_states_s = [], [], [], []
    n_layers = len(LAYER_KINDS)
    for i in range(n_layers):
        kind, slot = LAYER_KINDS[i], LAYER_SLOTS[i]
        last = i == n_layers - 1
        g_next = final_g if last else norm_g[i + 1]
        h_dtype = F32 if last else BF16
        if kind == 0:
            n_in = w_in_a.shape[2]
            zp, zs = _proj(hp, hs, w_in_a, slot, n_in, "a", tabs_a_p, tabs_a_s, seq // tm_p, tm_p, 1024)
            yp = _attn_prompt(zp, bp, seq)
            ys = _attn_sample(zs, caches_a, slot, bs, t_real)
            ys = jnp.pad(ys, ((0, 0), (0, SAMPLE_PAD - t_real), (0, 0))).reshape(ms, width).astype(BF16)
            zs3 = zs.reshape(bs, SAMPLE_PAD, n_in)
            rows_s = []
            for g in range(len(A_GROUPS)):
                lo = (3 * g + 1) * width
                rows_s.append(zs3[:, :t_real, lo:lo + 2 * width].reshape(bs, t_real, 2, A_HEADS, A_HEAD_DIM))
            a_z_p.append(zp)
            a_rows_s.append(rows_s)
            w_out = w_out_a
        elif kind == 1:
            n_in = w_in_b.shape[2]
            zp, zs = _proj(hp, hs, w_in_b, slot, n_in, "b", tabs_b_p, tabs_b_s, seq // tm_p, tm_p, 1024)
            yp, st_p = _retention(zp, bp, min(B_CHUNK, seq), min(B_CHUNK, seq), gn_b[slot], None)
            ys, st_s = _retention(zs, bs, SAMPLE_PAD, t_real, gn_b[slot], state_b[slot])
            b_states_p.append(st_p)
            b_states_s.append(st_s)
            w_out = w_out_b
        else:
            n_main = 2 * C_HEADS * C_DK + 2 * C_HEADS * C_DV
            w_lr = jnp.pad(w_in_c[slot][:, n_main:], ((0, 0), (0, LANES - C_RANK)))[None]
            zp, zs = _proj(hp, hs, w_in_c, slot, n_main, "c", (), (), 1, tm_p, 1024)
            zp_lr, zs_lr = _proj(hp, hs, w_lr, 0, LANES, "c", (), (), 1, tm_p, LANES)
            yp, st_p = _gla(zp, zp_lr, bp, min(C_CHUNK, seq), min(C_CHUNK, seq),
                            w_gate2_c, b_gate_c, gn_c, slot, None)
            ys, st_s = _gla(zs, zs_lr, bs, SAMPLE_PAD, t_real, w_gate2_c, b_gate_c, gn_c, slot, state_c[slot])
            c_states_p.append(st_p)
            c_states_s.append(st_s)
            w_out = w_out_c
        xp, hp = _outproj(yp, w_out, slot, xp, g_next, h_dtype, 512, not last)
        xs, hs = _outproj(ys, w_out, slot, xs, g_next, h_dtype, ms, not last)

    y_prompt = hp.reshape(bp, seq, d)
    y_sample = hs.reshape(bs, SAMPLE_PAD, d)[:, :t_real]
    stack = lambda rows, g: jnp.stack([r[g] for r in rows])
    kv_p = [_kv_rows(a_z_p, g, bp, seq, min(win, seq)) for g, (win, _) in enumerate(A_GROUPS)]
    return (y_prompt, y_sample,
            kv_p[0], kv_p[1], kv_p[2],
            jnp.stack(b_states_p), jnp.stack(c_states_p),
            stack(a_rows_s, 0), stack(a_rows_s, 1), stack(a_rows_s, 2),
            jnp.stack(b_states_s), jnp.stack(c_states_s))
```

```python
import functools

import jax
import jax.numpy as jnp
from jax import lax
from jax.experimental import pallas as pl
from jax.experimental.pallas import tpu as pltpu

F32 = jnp.float32
BF16 = jnp.bfloat16

PAST_LEN = 16384
NORM_EPS = 1e-6
A_GROUPS = ((128, 1), (512, 4), (2048, 16))
A_HEADS = 16
A_HEAD_DIM = 128
A_ROT = A_HEAD_DIM // 4
A_ROPE_THETA = 500000.0
A_BAND = 128
B_HEADS = 8
B_DK = 256
B_DV = 512
B_CHUNK = 128
B_ROPE_THETA = 10000.0
C_HEADS = 4
C_DK = 256
C_DV = 512
C_RANK = 16
C_TAU = 16.0
C_CHUNK = 64
C_SUB = 8
LAYER_KINDS = (0, 1, 2, 0)
LAYER_SLOTS = (0, 0, 0, 1)

SAMPLE_PAD = 16
LANES = 128
MXU_N = 256
OUTPROJ_CAST_LIMIT = 32 * 1024 * 1024
VMEM_LIMIT = 56 * 1024 * 1024


def _params(sem, vmem=VMEM_LIMIT):
    return pltpu.CompilerParams(dimension_semantics=sem, vmem_limit_bytes=vmem)


def _silu(g):
    return g * (1.0 / (1.0 + jnp.exp(-g)))


def _nt(a, b):
    return lax.dot_general(a, b, (((1,), (1,)), ((), ())), preferred_element_type=F32)


def _nn(a, b):
    return jnp.dot(a, b, preferred_element_type=F32)


def _split(x):
    hi = x.astype(BF16)
    lo = (x - hi.astype(F32)).astype(BF16)
    return hi, lo


def _split3(x):
    hi = x.astype(BF16)
    r = x - hi.astype(F32)
    mid = r.astype(BF16)
    lo = (r - mid.astype(F32)).astype(BF16)
    return hi, mid, lo


def _nn_exact_lhs(sel, x):
    hi, mid, lo = _split3(x)
    return _nn(sel, hi) + _nn(sel, mid) + _nn(sel, lo)


def _round_robin(gens):
    gens = list(gens)
    results = [None] * len(gens)
    live = list(range(len(gens)))
    while live:
        for n in list(live):
            try:
                next(gens[n])
            except StopIteration as stop:
                results[n] = stop.value
                live.remove(n)
    return results


def _staggered(gens):
    pending = list(gens)
    active = []
    while pending or active:
        if pending:
            active.append(pending.pop(0))
        for g in list(active):
            try:
                next(g)
            except StopIteration:
                active.remove(g)


def _nn_hi(a, b):
    ah, al = _split(a)
    bh, bl = _split(b)
    return _nn(ah, bh) + _nn(ah, bl) + _nn(al, bh)


def _rmsnorm_body(x_ref, g_ref, o_ref):
    x = x_ref[...]
    ms = jnp.mean(x * x, axis=-1, keepdims=True)
    o_ref[...] = (x * lax.rsqrt(ms + NORM_EPS) * g_ref[...]).astype(o_ref.dtype)


def _rmsnorm(x, g, out_dtype):
    m, d = x.shape
    tm = min(m, 512)
    return pl.pallas_call(
        _rmsnorm_body,
        grid=(m // tm,),
        in_specs=[pl.BlockSpec((tm, d), lambda i: (i, 0)),
                  pl.BlockSpec((1, d), lambda i: (0, 0))],
        out_specs=pl.BlockSpec((tm, d), lambda i: (i, 0)),
        out_shape=jax.ShapeDtypeStruct((m, d), out_dtype),
        compiler_params=_params(("arbitrary",)),
        name="rmsnorm",
    )(x, g.reshape(1, d))


N_ROT_TABLES = {"a": 2, "b": 2, "c": 0}


ROT_ROWS = 128


def _rot_a(src_ref, cs, width, z_ref, tabs, j, tn):
    c_ref, s_ref = tabs
    half = A_ROT // 2
    rows = min(ROT_ROWS, src_ref.shape[0])
    first_half = lax.broadcasted_iota(jnp.int32, (rows, A_HEAD_DIM), 1) < half
    for rb in range(0, src_ref.shape[0], rows):
        rs = slice(rb, rb + rows)
        for o in range(cs, cs + width, A_HEAD_DIM):
            x = src_ref[rs, o:o + A_HEAD_DIM]
            x_hi = pltpu.roll(x, A_HEAD_DIM - half, 1)
            x_lo = pltpu.roll(x, half, 1)
            z_ref[rs, o:o + A_HEAD_DIM] = x * c_ref[rs, :] + jnp.where(first_half, x_hi, x_lo) * s_ref[rs, :]


def _rot_b(src_ref, cs, width, z_ref, tabs, j, tn):
    c_ref, s_ref = tabs
    scale = jnp.where(j * tn >= B_HEADS * B_DK, B_DK ** -0.5, 1.0).astype(F32)
    half = B_DK // 2
    rows = min(ROT_ROWS, src_ref.shape[0])
    for rb in range(0, src_ref.shape[0], rows):
        rs = slice(rb, rb + rows)
        cos, sin = c_ref[rs, :] * scale, s_ref[rs, :] * scale
        for o in range(cs, cs + width, B_DK):
            x1 = src_ref[rs, o:o + half]
            x2 = src_ref[rs, o + half:o + B_DK]
            z_ref[rs, o:o + half] = x1 * cos - x2 * sin
            z_ref[rs, o + half:o + B_DK] = x1 * sin + x2 * cos


def _proj_body(*refs, kind, tn, ni, n_steps, lagged, side):
    n_tab = N_ROT_TABLES[kind]
    refs = list(refs)
    h_ref, hs_ref, w_ref = refs[:3]
    del refs[:3]
    w_side_ref = refs.pop(0) if side else None
    tabs, tabs_s = refs[:n_tab], refs[n_tab:2 * n_tab]
    del refs[:2 * n_tab]
    z_ref, zs_ref = refs[:2]
    del refs[:2]
    if side:
        z_side_ref, zs_side_ref = refs[:2]
        del refs[:2]
    wb_ref = refs.pop(0)
    s = pl.program_id(0)
    s_mm = jnp.minimum(s, n_steps - 1)
    j, i = s_mm // ni, s_mm % ni
    if lagged:
        acc_even, acc_odd = refs
        j_prev = jnp.maximum(s - 1, 0) // ni
    else:
        j_prev = j
    if side:
        @pl.when(j == 0)
        def _():
            w_side = w_side_ref[...].astype(BF16)
            z_side_ref[...] = _nn(h_ref[...], w_side)

            @pl.when(i == 0)
            def _():
                zs_side_ref[...] = _nn(hs_ref[...], w_side)
    epilogue = {"a": _rot_a, "b": _rot_b, "c": None}[kind]

    def is_rot(jj):
        if kind == "a":
            seg = (jj * tn) // (A_HEADS * A_HEAD_DIM)
            return jnp.logical_and(seg < 3 * len(A_GROUPS), seg % 3 < 2)
        return (jj * tn) < 2 * B_HEADS * B_DK

    slab = min(tn, MXU_N)

    def finish(src_ref, cs, o_ref, tables, jj, rot):
        if rot:
            epilogue(src_ref, cs, slab, o_ref, tables, jj, tn)
        elif src_ref is not o_ref:
            rows = min(ROT_ROWS, src_ref.shape[0])
            for rb in range(0, src_ref.shape[0], rows):
                o_ref[rb:rb + rows, cs:cs + slab] = src_ref[rb:rb + rows, cs:cs + slab]

    if lagged:
        @pl.when(s == 0)
        def _():
            acc_odd[...] = jnp.zeros_like(acc_odd)

    @pl.when(jnp.logical_and(i == 0, s < n_steps))
    def _():
        wb_ref[...] = w_ref[...].astype(BF16)

        def sample(rot):
            for cs in range(0, tn, slab):
                zs_ref[:, cs:cs + slab] = _nn(hs_ref[...], wb_ref[:, cs:cs + slab])
                finish(zs_ref, cs, zs_ref, tabs_s, j, rot)

        if epilogue is None:
            sample(False)
        else:
            pl.when(is_rot(j))(lambda: sample(True))
            pl.when(jnp.logical_not(is_rot(j)))(lambda: sample(False))

    def step(acc_new, acc_old, rot):
        for cs in range(0, tn, slab):
            acc_new[:, cs:cs + slab] = _nn(h_ref[...], wb_ref[:, cs:cs + slab])
            finish(acc_old, cs, z_ref, tabs, j_prev, rot)

    if lagged:
        even = s % 2 == 0
        variants = ((even, acc_even, acc_odd), (jnp.logical_not(even), acc_odd, acc_even))
    else:
        variants = ((None, z_ref, z_ref),)
    for on, acc_new, acc_old in variants:
        if epilogue is None:
            plain = functools.partial(step, acc_new, acc_old, False)
            plain() if on is None else pl.when(on)(plain)
        else:
            rot_prev = is_rot(j_prev)
            both = lambda a, b: b if a is None else jnp.logical_and(a, b)
            pl.when(both(on, rot_prev))(functools.partial(step, acc_new, acc_old, True))
            pl.when(both(on, jnp.logical_not(rot_prev)))(functools.partial(step, acc_new, acc_old, False))


def _proj(h, hs, w3, slot, n_out, kind, tabs, tabs_s, tab_blocks, tm, tn, w_side=None):
    m, k = h.shape
    ms = hs.shape[0]
    ni = m // tm
    n_steps = (n_out // tn) * ni
    lagged = kind == "a"
    side = w_side is not None
    assert not (side and lagged)
    mm = lambda s: jnp.minimum(s, n_steps - 1)
    prev = (lambda s: jnp.maximum(s - 1, 0)) if lagged else (lambda s: s)
    const = lambda s: (0, 0)
    in_specs = [pl.BlockSpec((tm, k), lambda s: (mm(s) % ni, 0)),
                pl.BlockSpec((ms, k), const),
                pl.BlockSpec((None, k, tn), lambda s: (slot, 0, mm(s) // ni))]
    args = [h, hs, w3]
    if side:
        in_specs.append(pl.BlockSpec((k, LANES), const))
        args.append(w_side)
    in_specs += [pl.BlockSpec((tm, LANES), lambda s: ((prev(s) % ni) % tab_blocks, 0)) for _ in tabs]
    in_specs += [pl.BlockSpec((ms, LANES), const) for _ in tabs_s]
    out_specs = [pl.BlockSpec((tm, tn), lambda s: (prev(s) % ni, prev(s) // ni)),
                 pl.BlockSpec((ms, tn), lambda s: (0, mm(s) // ni))]
    out_shape = [jax.ShapeDtypeStruct((m, n_out), F32), jax.ShapeDtypeStruct((ms, n_out), F32)]
    if side:
        out_specs += [pl.BlockSpec((tm, LANES), lambda s: (jnp.minimum(s, ni - 1), 0)),
                      pl.BlockSpec((ms, LANES), const)]
        out_shape += [jax.ShapeDtypeStruct((m, LANES), F32), jax.ShapeDtypeStruct((ms, LANES), F32)]
    scratch = [pltpu.VMEM((k, tn), BF16)]
    if lagged:
        scratch += [pltpu.VMEM((tm, tn), F32), pltpu.VMEM((tm, tn), F32)]
    return pl.pallas_call(
        functools.partial(_proj_body, kind=kind, tn=tn, ni=ni, n_steps=n_steps, lagged=lagged, side=side),
        grid=(n_steps + 1 if lagged else n_steps,),
        in_specs=in_specs,
        out_specs=out_specs,
        out_shape=out_shape,
        scratch_shapes=scratch,
        compiler_params=_params(("arbitrary",)),
        name="proj_" + kind,
    )(*args, *tabs, *tabs_s)


def _cast_body(x_ref, o_ref):
    o_ref[...] = x_ref[...].astype(o_ref.dtype)


def _cast_bf16(w3):
    n, k, d = w3.shape
    rows = n * k
    tr = min(rows, 1024)
    out = pl.pallas_call(
        _cast_body,
        grid=(rows // tr,),
        in_specs=[pl.BlockSpec((tr, d), lambda i: (i, 0))],
        out_specs=pl.BlockSpec((tr, d), lambda i: (i, 0)),
        out_shape=jax.ShapeDtypeStruct((rows, d), BF16),
        compiler_params=_params(("arbitrary",)),
        name="cast_bf16",
    )(w3.reshape(rows, d))
    return out.reshape(n, k, d)


def _outproj_body(y_ref, w_ref, x_ref, g_ref, *out_refs, cast_w):
    if cast_w:
        *out_refs, wb_ref = out_refs

        @pl.when(pl.program_id(0) == 0)
        def _():
            wb_ref[...] = w_ref[...].astype(BF16)
    else:
        wb_ref = w_ref
    xn = x_ref[...] + _nn(y_ref[...], wb_ref[...])
    if len(out_refs) == 2:
        out_refs[0][...] = xn
    ho_ref = out_refs[-1]
    ms = jnp.mean(xn * xn, axis=-1, keepdims=True)
    ho_ref[...] = (xn * lax.rsqrt(ms + NORM_EPS) * g_ref[...]).astype(ho_ref.dtype)


def _outproj(y, w3, slot, x, g, h_dtype, tm, want_x):
    m, kd = y.shape
    d = x.shape[1]
    cast_w = w3.dtype == F32
    row_spec = pl.BlockSpec((tm, d), lambda i: (i, 0))
    outs = pl.pallas_call(
        functools.partial(_outproj_body, cast_w=cast_w),
        scratch_shapes=[pltpu.VMEM((kd, d), BF16)] if cast_w else [],
        grid=(m // tm,),
        in_specs=[pl.BlockSpec((tm, kd), lambda i: (i, 0)),
                  pl.BlockSpec((None, kd, d), lambda i: (slot, 0, 0), pipeline_mode=pl.Buffered(1)),
                  row_spec,
                  pl.BlockSpec((1, d), lambda i: (0, 0))],
        out_specs=[row_spec] * (2 if want_x else 1),
        out_shape=([jax.ShapeDtypeStruct((m, d), F32)] if want_x else []) + [jax.ShapeDtypeStruct((m, d), h_dtype)],
        compiler_params=_params(("arbitrary",)),
        name="outproj",
    )(y, w3, x, g.reshape(1, d))
    return (outs[0], outs[1]) if want_x else (None, outs[0])


LOG2_E = 1.4426950408889634


def _rows(start, size, stride):
    return pl.ds(start, size) if stride == 1 else pl.ds(start, size, stride=stride)


def _attn_prompt_body(*refs, seq):
    qkv_refs, gate_ref, bias_first_ref, bias_band_ref, y_ref = refs[:9], *refs[9:13]
    o_s, d_s, m_s = refs[13:]
    q_scale = (A_HEAD_DIM ** -0.5) * LOG2_E
    ones = jnp.ones((A_BAND, A_HEAD_DIM), BF16)
    for g, (_, dil) in enumerate(A_GROUPS):
        q_ref, k_ref, v_ref = qkv_refs[3 * g:3 * g + 3]

        def load_kv(rows, k_ref=k_ref, v_ref=v_ref):
            return k_ref[rows, :].astype(BF16), jnp.concatenate([v_ref[rows, :].astype(BF16), ones], axis=1)

        def chain(r, n0, n1, g=g, dil=dil, q_ref=q_ref, load_kv=load_kv):
            block_rows = lambda n: _rows(r + dil * A_BAND * n, A_BAND, dil)
            k_prev, v_prev = load_kv(block_rows(n0 - 1)) if n0 > 0 else (None, None)
            for n in range(n0, n1):
                rows = block_rows(n)
                q = (q_ref[rows, :] * q_scale).astype(BF16)
                k_cur, v_cur = load_kv(rows)
                if n == 0:
                    keys, vals, bias_ref = k_cur, v_cur, bias_first_ref
                else:
                    keys = jnp.concatenate([k_prev, k_cur], axis=0)
                    vals = jnp.concatenate([v_prev, v_cur], axis=0)
                    bias_ref = bias_band_ref
                k_prev, v_prev = k_cur, v_cur
                s = _nt(q, keys)
                yield
                s = s + bias_ref[...]
                mx = jnp.max(s, axis=-1, keepdims=True)
                od = _nn(jnp.exp2(s - mx).astype(BF16), vals)
                yield
                o_s[g, rows, :] = od[:, :A_HEAD_DIM]
                d_s[g, rows, :] = od[:, A_HEAD_DIM:]
                m_s[g, rows, :] = jnp.broadcast_to(mx, (A_BAND, A_HEAD_DIM))

        nb = seq // dil // A_BAND
        if dil == 1:
            chains = [chain(0, 0, nb // 2), chain(0, nb // 2, nb)]
        else:
            chains = [chain(r, 0, nb) for r in range(dil)]
        _staggered(chains)
    for tile in range(seq // A_BAND):
        rs = slice(tile * A_BAND, (tile + 1) * A_BAND)
        m1, m2, m3 = m_s[0, rs, :], m_s[1, rs, :], m_s[2, rs, :]
        mx = jnp.maximum(jnp.maximum(m1, m2), m3)
        f1, f2, f3 = jnp.exp2(m1 - mx), jnp.exp2(m2 - mx), jnp.exp2(m3 - mx)
        num = f1 * o_s[0, rs, :] + f2 * o_s[1, rs, :] + f3 * o_s[2, rs, :]
        den = f1 * d_s[0, rs, :] + f2 * d_s[1, rs, :] + f3 * d_s[2, rs, :]
        y_ref[rs, :] = (num / den * _silu(gate_ref[rs, :])).astype(y_ref.dtype)


def _attn_prompt(z, batch, seq):
    n_groups = len(A_GROUPS)
    in_specs = []
    for c in range(3 * n_groups + 1):
        in_specs.append(pl.BlockSpec((seq, A_HEAD_DIM), lambda b, h, c=c: (b, c * A_HEADS + h)))
    row = jnp.arange(A_BAND)[:, None]
    col = jnp.arange(2 * A_BAND)[None, :]
    bias_first = jnp.where(col[:, :A_BAND] <= row, 0.0, -jnp.inf).astype(F32)
    bias_band = jnp.where((col >= row) & (col <= row + A_BAND), 0.0, -jnp.inf).astype(F32)
    in_specs.append(pl.BlockSpec((A_BAND, A_BAND), lambda b, h: (0, 0)))
    in_specs.append(pl.BlockSpec((A_BAND, 2 * A_BAND), lambda b, h: (0, 0)))
    return pl.pallas_call(
        functools.partial(_attn_prompt_body, seq=seq),
        grid=(batch, A_HEADS),
        in_specs=in_specs,
        out_specs=pl.BlockSpec((seq, A_HEAD_DIM), lambda b, h: (b, h)),
        out_shape=jax.ShapeDtypeStruct((batch * seq, A_HEADS * A_HEAD_DIM), BF16),
        scratch_shapes=[pltpu.VMEM((n_groups, seq, A_HEAD_DIM), F32)] * 3,
        compiler_params=_params(("arbitrary", "arbitrary")),
        name="attn_prompt",
    )(*([z] * (3 * n_groups + 1)), bias_first, bias_band)


def _attn_sample_body(z_ref, c1_ref, c2_ref, c3_ref, y_ref):
    t = pl.program_id(1)
    scale = A_HEAD_DIM ** -0.5
    m_idx = lax.broadcasted_iota(jnp.int32, (A_BAND, A_HEADS, 1), 0)
    n_idx = lax.broadcasted_iota(jnp.int32, (SAMPLE_PAD, A_HEADS, 1), 0)
    caches = (c1_ref, c2_ref, c3_ref)
    outs, lses = [], []
    for g, (_, dil) in enumerate(A_GROUPS):
        base = 3 * g * A_HEADS
        q = z_ref[0, t, base:base + A_HEADS, :]
        kc = caches[g][0, :, 0:A_HEADS, :]
        vc = caches[g][0, :, A_HEADS:2 * A_HEADS, :]
        kn = z_ref[0, :, base + A_HEADS:base + 2 * A_HEADS, :]
        vn = z_ref[0, :, base + 2 * A_HEADS:base + 3 * A_HEADS, :]
        s_c = jnp.sum(kc * q[None], axis=-1, keepdims=True) * scale
        s_n = jnp.sum(kn * q[None], axis=-1, keepdims=True) * scale
        if dil == 1:
            s_c = jnp.where(m_idx >= t, s_c, -jnp.inf)
            s_n = jnp.where(n_idx <= t, s_n, -jnp.inf)
        else:
            s_n = jnp.where(n_idx == t, s_n, -jnp.inf)
        mx = jnp.maximum(jnp.max(s_c, axis=0), jnp.max(s_n, axis=0))
        p_c = jnp.exp(s_c - mx[None])
        p_n = jnp.exp(s_n - mx[None])
        den = jnp.sum(p_c, axis=0) + jnp.sum(p_n, axis=0)
        num = jnp.sum(p_c * vc, axis=0) + jnp.sum(p_n * vn, axis=0)
        outs.append(num / den)
        lses.append(mx + jnp.log(den))
    lmax = jnp.maximum(jnp.maximum(lses[0], lses[1]), lses[2])
    es = [jnp.exp(l - lmax) for l in lses]
    tot = es[0] + es[1] + es[2]
    o = (es[0] / tot) * outs[0] + (es[1] / tot) * outs[1] + (es[2] / tot) * outs[2]
    gate = z_ref[0, t, 3 * len(A_GROUPS) * A_HEADS:(3 * len(A_GROUPS) + 1) * A_HEADS, :]
    y_ref[0, 0] = o * _silu(gate)


def _attn_sample(zs, caches, slot, batch, t_real):
    n_in = zs.shape[1]
    z5 = zs.reshape(batch, SAMPLE_PAD, n_in // A_HEAD_DIM, A_HEAD_DIM)
    in_specs = [pl.BlockSpec((1, SAMPLE_PAD, n_in // A_HEAD_DIM, A_HEAD_DIM), lambda b, t: (b, 0, 0, 0))]
    args = [z5]
    for g, (win, dil) in enumerate(A_GROUPS):
        n_a = caches[g].shape[0]
        cv = caches[g].reshape(n_a * batch, win // dil, dil * 2 * A_HEADS, A_HEAD_DIM)
        if dil == 1:
            idx = lambda b, t: (slot * batch + b, 0, 0, 0)
        else:
            idx = lambda b, t: (slot * batch + b, 0, t, 0)
        in_specs.append(pl.BlockSpec((1, A_BAND, 2 * A_HEADS, A_HEAD_DIM), idx))
        args.append(cv)
    y = pl.pallas_call(
        _attn_sample_body,
        grid=(batch, t_real),
        in_specs=in_specs,
        out_specs=pl.BlockSpec((1, 1, A_HEADS, A_HEAD_DIM), lambda b, t: (b, t, 0, 0)),
        out_shape=jax.ShapeDtypeStruct((batch, t_real, A_HEADS, A_HEAD_DIM), F32),
        compiler_params=_params(("arbitrary", "arbitrary")),
        name="attn_sample",
    )(*args)
    return y.reshape(batch, t_real, A_HEADS * A_HEAD_DIM)


def _kv_rows_body(*refs, n_layers):
    out_ref = refs[-1]
    layer = pl.program_id(0)
    for l in range(n_layers):
        k_ref, v_ref = refs[2 * l], refs[2 * l + 1]

        @pl.when(layer == l)
        def _():
            sub = 8
            for base, src in ((0, k_ref), (A_HEADS, v_ref)):
                for c0 in range(0, A_HEADS, sub):
                    cols = [src[:, (c0 + c) * A_HEAD_DIM:(c0 + c + 1) * A_HEAD_DIM] for c in range(sub)]
                    out_ref[:, base + c0:base + c0 + sub, :] = jnp.swapaxes(jnp.stack(cols, axis=0), 0, 1)


def _kv_rows(zs, g, batch, seq, keep):
    n_layers = len(zs)
    width = A_HEADS * A_HEAD_DIM
    tt = min(keep, 512)
    first = (seq - keep) // tt
    in_specs, args = [], []
    for l in range(n_layers):
        for c in (1, 2):
            def idx(s, b, i, l=l, c=c):
                return (jnp.where(s == l, b * (seq // tt) + first + i, 0), 3 * g + c)
            in_specs.append(pl.BlockSpec((tt, width), idx))
            args.append(zs[l])
    out = pl.pallas_call(
        functools.partial(_kv_rows_body, n_layers=n_layers),
        grid=(n_layers, batch, keep // tt),
        in_specs=in_specs,
        out_specs=pl.BlockSpec((None, None, tt, 2 * A_HEADS, A_HEAD_DIM), lambda s, b, i: (s, b, i, 0, 0)),
        out_shape=jax.ShapeDtypeStruct((n_layers, batch, keep, 2 * A_HEADS, A_HEAD_DIM), F32),
        compiler_params=_params(("arbitrary", "arbitrary", "arbitrary")),
        name="kv_rows_g%d" % g,
    )(*args)
    return out.reshape(n_layers, batch, keep, 2, A_HEADS, A_HEAD_DIM)


def _tn(a, b):
    return lax.dot_general(a, b, (((0,), (0,)), ((), ())), preferred_element_type=F32)


def _retention_body(*refs, has_state):
    if has_state:
        q_ref, k_ref, v_ref, gate_ref, dec_ref, xi_ref, kd_ref, sd_ref, gn_ref, s0_ref, y_ref, st_ref = refs
    else:
        q_ref, k_ref, v_ref, gate_ref, dec_ref, xi_ref, kd_ref, sd_ref, gn_ref, y_ref, st_ref = refs
    c = pl.program_id(1)

    @pl.when(c == 0)
    def _():
        if has_state:
            st_ref[...] = s0_ref[...]
        else:
            st_ref[...] = jnp.zeros_like(st_ref)

    def head(h):
        ks = slice(h * B_DK, (h + 1) * B_DK)
        vs = slice(h * B_DV, (h + 1) * B_DV)
        qb = q_ref[:, ks].astype(BF16)
        kf = k_ref[:, ks]
        vb = v_ref[:, vs].astype(BF16)
        state = st_ref[0, h]
        scores = _nt(qb, kf.astype(BF16))
        cross = _nn(qb, state.astype(BF16))
        k_dec = (kf * kd_ref[h]).astype(BF16)
        st_ref[0, h] = sd_ref[h] * state + _tn(k_dec, vb)
        yield
        o = _nn((scores * dec_ref[h]).astype(BF16), vb) + cross * xi_ref[h]
        yield
        mu = jnp.mean(o, axis=-1, keepdims=True)
        var = jnp.mean(jnp.square(o - mu), axis=-1, keepdims=True)
        yv = (o - mu) * lax.rsqrt(var + NORM_EPS) * gn_ref[h]
        y_ref[:, vs] = (yv * _silu(gate_ref[:, vs])).astype(y_ref.dtype)

    _round_robin([head(h) for h in range(B_HEADS)])


def _retention(z, batch, chunk, t_real, gn, state0):
    rows = z.shape[0]
    nc = rows // (batch * chunk)
    lg = jnp.log(1.0 - 2.0 ** (-5.0 - jnp.arange(B_HEADS, dtype=F32)))[:, None, None]
    idx = jnp.arange(chunk, dtype=F32)
    diff = idx[:, None] - idx[None, :]
    dec = jnp.where(diff >= 0, jnp.exp(lg * jnp.maximum(diff, 0.0)), 0.0)
    xi = jnp.exp(lg * (idx[None, :, None] + 1.0))
    live = idx[None, :, None] < t_real
    kd = jnp.where(live, jnp.exp(lg * jnp.where(live, t_real - 1.0 - idx[None, :, None], 0.0)), 0.0)
    sd = jnp.exp(lg * t_real)
    n_k, n_v = B_HEADS * B_DK, B_HEADS * B_DV
    const = lambda b, c: (0, 0, 0)
    in_specs = [pl.BlockSpec((chunk, n_k), lambda b, c: (b * nc + c, 0)),
                pl.BlockSpec((chunk, n_k), lambda b, c: (b * nc + c, 1)),
                pl.BlockSpec((chunk, n_v), lambda b, c: (b * nc + c, 2 * n_k // n_v)),
                pl.BlockSpec((chunk, n_v), lambda b, c: (b * nc + c, 2 * n_k // n_v + 1)),
                pl.BlockSpec((B_HEADS, chunk, chunk), const),
                pl.BlockSpec((B_HEADS, chunk, 1), const),
                pl.BlockSpec((B_HEADS, chunk, 1), const),
                pl.BlockSpec((B_HEADS, 1, 1), const),
                pl.BlockSpec((B_HEADS, 1, B_DV), const)]
    args = [z, z, z, z, dec, xi, kd, sd, gn.reshape(B_HEADS, 1, B_DV)]
    state_spec = pl.BlockSpec((1, B_HEADS, B_DK, B_DV), lambda b, c: (b, 0, 0, 0))
    if state0 is not None:
        in_specs.append(state_spec)
        args.append(state0)
    return pl.pallas_call(
        functools.partial(_retention_body, has_state=state0 is not None),
        grid=(batch, nc),
        in_specs=in_specs,
        out_specs=[pl.BlockSpec((chunk, n_v), lambda b, c: (b * nc + c, 0)), state_spec],
        out_shape=[jax.ShapeDtypeStruct((rows, n_v), BF16),
                   jax.ShapeDtypeStruct((batch, B_HEADS, B_DK, B_DV), F32)],
        compiler_params=_params(("arbitrary", "arbitrary")),
        name="retention",
    )(*args)


def _log_sigmoid(x):
    return jnp.minimum(x, 0.0) - jnp.log(1.0 + jnp.exp(-jnp.abs(x)))


def _gla_head(q, k, v, lr, wg, bg, state, consts, chunk, sub, t_sub):
    nsub = chunk // sub
    tri, pair_live, pair_col, pick_t = consts
    pre = _nn_hi(lr, wg)
    yield
    log_a = _log_sigmoid(pre + bg) * (LOG2_E / C_TAU)
    bl = _nn_exact_lhs(tri, log_a)
    yield

    def block_rows(vals):
        return jnp.concatenate([jnp.broadcast_to(x, (sub, C_DK)) for x in vals], axis=0)

    tot = [bl[j * sub + t_sub - 1:j * sub + t_sub, :] for j in range(nsub)]
    beta = [jnp.zeros((1, C_DK), F32)]
    for j in range(nsub):
        beta.append(beta[-1] + tot[j])
    qs = q * (C_DK ** -0.5)
    q_in = qs * jnp.exp2(bl)
    k_out = k * jnp.exp2(block_rows(tot) - bl)
    if t_sub < sub:
        k_out = jnp.where(lax.broadcasted_iota(jnp.int32, (chunk, 1), 0) % sub < t_sub, k_out, 0.0)
    vb = v.astype(BF16)

    o = _nn((q_in * block_rows([jnp.exp2(x) for x in beta[:nsub]])).astype(BF16), state.astype(BF16))
    yield

    def rep_rows(x):
        return jnp.concatenate([jnp.broadcast_to(x[t:t + 1, :], (sub, C_DK)) for t in range(sub)], axis=0)

    def tile_rows(x):
        return jnp.concatenate([x] * sub, axis=0)

    a_rows = []
    for i in range(nsub):
        r = slice(i * sub, (i + 1) * sub)
        pair = rep_rows(qs[r]) * tile_rows(k[r]) * jnp.exp2(rep_rows(bl[r]) - tile_rows(bl[r]))
        att = jnp.where(pair_live, jnp.sum(pair, axis=1, keepdims=True), 0.0)
        placed = jnp.where(pair_col == i * sub, att, 0.0).astype(BF16)
        a_i = _nn(pick_t, placed)
        if i > 0:
            between = [jnp.broadcast_to(jnp.exp2(beta[i] - beta[j + 1]), (sub, C_DK)) for j in range(i)]
            between.append(jnp.zeros(((nsub - i) * sub, C_DK), F32))
            a_i += _nt(q_in[r].astype(BF16), (k_out * jnp.concatenate(between, axis=0)).astype(BF16))
        a_rows.append(a_i)
        if i % 2 == 1:
            yield
    a = a_rows[0] if nsub == 1 else jnp.concatenate(a_rows, axis=0)
    o += _nn(a.astype(BF16), vb)
    yield
    k_end = k_out * block_rows([jnp.exp2(beta[nsub] - beta[j + 1]) for j in range(nsub)])
    d_col = jnp.broadcast_to(jnp.exp2(beta[nsub]), (8, C_DK)).T[:, 0:1]
    return o, d_col * state + _tn(k_end.astype(BF16), vb)


def _gla_consts(chunk, sub):
    ri = lax.broadcasted_iota(jnp.int32, (chunk, chunk), 0)
    ci = lax.broadcasted_iota(jnp.int32, (chunk, chunk), 1)
    tri = jnp.logical_and(ri // sub == ci // sub, ci <= ri).astype(BF16)
    pj = lax.broadcasted_iota(jnp.int32, (sub * sub, 1), 0)
    pair_live = pj // sub >= pj % sub
    pc = lax.broadcasted_iota(jnp.int32, (sub * sub, chunk), 1)
    ps = lax.broadcasted_iota(jnp.int32, (sub * sub, chunk), 0) % sub
    et = lax.broadcasted_iota(jnp.int32, (sub, sub * sub), 0)
    ej = lax.broadcasted_iota(jnp.int32, (sub, sub * sub), 1)
    pick_t = (ej // sub == et).astype(BF16)
    return tri, pair_live, pc - ps, pick_t


def _gla_body(*refs, has_state, chunk, sub, t_sub):
    consts = _gla_consts(chunk, sub)
    if has_state:
        q_ref, k_ref, v_ref, gate_ref, lr_ref, wg_ref, bg_ref, gn_ref, s0_ref, y_ref, st_ref = refs
    else:
        q_ref, k_ref, v_ref, gate_ref, lr_ref, wg_ref, bg_ref, gn_ref, y_ref, st_ref = refs
    c = pl.program_id(1)

    @pl.when(c == 0)
    def _():
        if has_state:
            st_ref[...] = s0_ref[...]
        else:
            st_ref[...] = jnp.zeros_like(st_ref)

    lr = lr_ref[:, 0:C_RANK]
    heads = []
    for h in range(C_HEADS):
        ks = slice(h * C_DK, (h + 1) * C_DK)
        vs = slice(h * C_DV, (h + 1) * C_DV)
        heads.append(_gla_head(q_ref[:, ks], k_ref[:, ks], v_ref[:, vs], lr, wg_ref[:, ks], bg_ref[:, ks],
                               st_ref[0, h], consts, chunk, sub, t_sub))
    for h, (o, new_state) in enumerate(_round_robin(heads)):
        vs = slice(h * C_DV, (h + 1) * C_DV)
        st_ref[0, h] = new_state
        yv = o * lax.rsqrt(jnp.mean(o * o, axis=-1, keepdims=True) + NORM_EPS) * gn_ref[:, vs]
        y_ref[:, vs] = (yv * _silu(gate_ref[:, vs])).astype(y_ref.dtype)


def _gla(z, z_lr, batch, chunk, t_real, wg3, bg3, gn3, slot, state0):
    rows = z.shape[0]
    nc = rows // (batch * chunk)
    sub, t_sub = (C_SUB, C_SUB) if t_real == chunk else (chunk, t_real)
    n_k, n_v = C_HEADS * C_DK, C_HEADS * C_DV
    n_slots = wg3.shape[0]
    in_specs = [pl.BlockSpec((chunk, n_k), lambda b, c: (b * nc + c, 0)),
                pl.BlockSpec((chunk, n_k), lambda b, c: (b * nc + c, 1)),
                pl.BlockSpec((chunk, n_v), lambda b, c: (b * nc + c, 2 * n_k // n_v)),
                pl.BlockSpec((chunk, n_v), lambda b, c: (b * nc + c, 2 * n_k // n_v + 1)),
                pl.BlockSpec((chunk, LANES), lambda b, c: (b * nc + c, 0)),
                pl.BlockSpec((None, C_RANK, n_k), lambda b, c: (slot, 0, 0)),
                pl.BlockSpec((None, 1, n_k), lambda b, c: (slot, 0, 0)),
                pl.BlockSpec((None, 1, n_v), lambda b, c: (slot, 0, 0))]
    args = [z, z, z, z, z_lr, wg3, bg3.reshape(n_slots, 1, n_k), gn3.reshape(n_slots, 1, n_v)]
    state_spec = pl.BlockSpec((1, C_HEADS, C_DK, C_DV), lambda b, c: (b, 0, 0, 0))
    if state0 is not None:
        in_specs.append(state_spec)
        args.append(state0)
    return pl.pallas_call(
        functools.partial(_gla_body, has_state=state0 is not None, chunk=chunk, sub=sub, t_sub=t_sub),
        grid=(batch, nc),
        in_specs=in_specs,
        out_specs=[pl.BlockSpec((chunk, n_v), lambda b, c: (b * nc + c, 0)), state_spec],
        out_shape=[jax.ShapeDtypeStruct((rows, n_v), BF16),
                   jax.ShapeDtypeStruct((batch, C_HEADS, C_DK, C_DV), F32)],
        compiler_params=_params(("arbitrary", "arbitrary")),
        name="gla",
    )(*args)


def _rot_tables_a(pos):
    half = A_ROT // 2
    inv_freq = A_ROPE_THETA ** (-jnp.arange(half, dtype=F32) / half)
    ang = pos.astype(F32)[:, None] * inv_freq[None, :]
    cos, sin = jnp.cos(ang), jnp.sin(ang)
    n = pos.shape[0]
    rest = A_HEAD_DIM - A_ROT
    c = jnp.concatenate([cos, cos, jnp.ones((n, rest), F32)], axis=1)
    s = jnp.concatenate([-sin, sin, jnp.zeros((n, rest), F32)], axis=1)
    return c, s


def _rot_tables_b(pos):
    half = B_DK // 2
    inv_freq = B_ROPE_THETA ** (-jnp.arange(half, dtype=F32) / half)
    ang = pos.astype(F32)[:, None] * inv_freq[None, :]
    return jnp.cos(ang), jnp.sin(ang)


def kernel(x_prompt, x_sample, cache_a_kv1, cache_a_kv2, cache_a_kv3, state_b, state_c, norm_g, final_g,
           w_in_a, w_out_a, w_in_b, gn_b, w_out_b, w_in_c, w_gate2_c, b_gate_c, gn_c, w_out_c):
    bp, seq, d = x_prompt.shape
    bs, t_real, _ = x_sample.shape
    caches_a = (cache_a_kv1, cache_a_kv2, cache_a_kv3)
    width = A_HEADS * A_HEAD_DIM

    xp = x_prompt.reshape(bp * seq, d)
    xs = jnp.pad(x_sample, ((0, 0), (0, SAMPLE_PAD - t_real), (0, 0))).reshape(bs * SAMPLE_PAD, d)
    ms = xs.shape[0]
    tm_p = 1024

    pos_p = jnp.arange(seq)
    pos_s = jnp.tile(PAST_LEN + jnp.arange(SAMPLE_PAD), bs)
    tabs_a_p, tabs_a_s = _rot_tables_a(pos_p), _rot_tables_a(pos_s)
    tabs_b_p, tabs_b_s = _rot_tables_b(pos_p), _rot_tables_b(pos_s)

    hp = _rmsnorm(xp, norm_g[0], BF16)
    hs = _rmsnorm(xs, norm_g[0], BF16)
    def maybe_cast(w3):
        return _cast_bf16(w3) if w3.shape[1] * w3.shape[2] * 6 > OUTPROJ_CAST_LIMIT else w3

    w_out_a, w_out_b, w_out_c = maybe_cast(w_out_a), maybe_cast(w_out_b), maybe_cast(w_out_c)

    a_z_p, a_rows_s = [], []
    b_states_p, b_states_s, c_states_p, c_states_s = [], [], [], []
    n_layers = len(LAYER_KINDS)
    for i in range(n_layers):
        kind, slot = LAYER_KINDS[i], LAYER_SLOTS[i]
        last = i == n_layers - 1
        g_next = final_g if last else norm_g[i + 1]
        h_dtype = F32 if last else BF16
        if kind == 0:
            n_in = w_in_a.shape[2]
            zp, zs = _proj(hp, hs, w_in_a, slot, n_in, "a", tabs_a_p, tabs_a_s, seq // tm_p, tm_p, 1024)
            yp = _attn_prompt(zp, bp, seq)
            ys = _attn_sample(zs, caches_a, slot, bs, t_real)
            ys = jnp.pad(ys, ((0, 0), (0, SAMPLE_PAD - t_real), (0, 0))).reshape(ms, width).astype(BF16)
            zs3 = zs.reshape(bs, SAMPLE_PAD, n_in)
            rows_s = []
            for g in range(len(A_GROUPS)):
                lo = (3 * g + 1) * width
                rows_s.append(zs3[:, :t_real, lo:lo + 2 * width].reshape(bs, t_real, 2, A_HEADS, A_HEAD_DIM))
            a_z_p.append(zp)
            a_rows_s.append(rows_s)
            w_out = w_out_a
        elif kind == 1:
            n_in = w_in_b.shape[2]
            zp, zs = _proj(hp, hs, w_in_b, slot, n_in, "b", tabs_b_p, tabs_b_s, seq // tm_p, tm_p, 1024)
            yp, st_p = _retention(zp, bp, min(B_CHUNK, seq), min(B_CHUNK, seq), gn_b[slot], None)
            ys, st_s = _retention(zs, bs, SAMPLE_PAD, t_real, gn_b[slot], state_b[slot])
            b_states_p.append(st_p)
            b_states_s.append(st_s)
            w_out = w_out_b
        else:
            n_main = 2 * C_HEADS * C_DK + 2 * C_HEADS * C_DV
            w_lr = jnp.pad(w_in_c[slot][:, n_main:], ((0, 0), (0, LANES - C_RANK)))
            zp, zs, zp_lr, zs_lr = _proj(hp, hs, w_in_c, slot, n_main, "c", (), (), 1, tm_p, 1024, w_side=w_lr)
            yp, st_p = _gla(zp, zp_lr, bp, min(C_CHUNK, seq), min(C_CHUNK, seq),
                            w_gate2_c, b_gate_c, gn_c, slot, None)
            ys, st_s = _gla(zs, zs_lr, bs, SAMPLE_PAD, t_real, w_gate2_c, b_gate_c, gn_c, slot, state_c[slot])
            c_states_p.append(st_p)
            c_states_s.append(st_s)
            w_out = w_out_c
        xp, hp = _outproj(yp, w_out, slot, xp, g_next, h_dtype, 512, not last)
        xs, hs = _outproj(ys, w_out, slot, xs, g_next, h_dtype, ms, not last)

    y_prompt = hp.reshape(bp, seq, d)
    y_sample = hs.reshape(bs, SAMPLE_PAD, d)[:, :t_real]
    stack = lambda rows, g: jnp.stack([r[g] for r in rows])
    kv_p = [_kv_rows(a_z_p, g, bp, seq, min(win, seq)) for g, (win, _) in enumerate(A_GROUPS)]
    return (y_prompt, y_sample,
            kv_p[0], kv_p[1], kv_p[2],
            jnp.stack(b_states_p), jnp.stack(c_states_p),
            stack(a_rows_s, 0), stack(a_rows_s, 1), stack(a_rows_s, 2),
            jnp.stack(b_states_s), jnp.stack(c_states_s))
```

```python
import functools

import jax
import jax.numpy as jnp
from jax import lax
from jax.experimental import pallas as pl
from jax.experimental.pallas import tpu as pltpu

F32 = jnp.float32
BF16 = jnp.bfloat16

PAST_LEN = 16384
NORM_EPS = 1e-6
A_GROUPS = ((128, 1), (512, 4), (2048, 16))
A_HEADS = 16
A_HEAD_DIM = 128
A_ROT = A_HEAD_DIM // 4
A_ROPE_THETA = 500000.0
A_BAND = 128
B_HEADS = 8
B_DK = 256
B_DV = 512
B_CHUNK = 128
B_ROPE_THETA = 10000.0
C_HEADS = 4
C_DK = 256
C_DV = 512
C_RANK = 16
C_TAU = 16.0
C_CHUNK = 64
C_SUB = 8
LAYER_KINDS = (0, 1, 2, 0)
LAYER_SLOTS = (0, 0, 0, 1)

SAMPLE_PAD = 16
LANES = 128
SUBLANES = 8
MXU_N = 256
OUTPROJ_CAST_LIMIT = 32 * 1024 * 1024
VMEM_LIMIT = 56 * 1024 * 1024

PROJ_TM = 1024
PROJ_TN = 1024
OUTPROJ_TM = 512
ROWWISE_TM = 512
CAST_ROWS = 1024
KV_ROWS = 512


def _params(sem, vmem=VMEM_LIMIT):
    return pltpu.CompilerParams(dimension_semantics=sem, vmem_limit_bytes=vmem)


def _silu(g):
    return g * (1.0 / (1.0 + jnp.exp(-g)))


def _nt(a, b):
    return lax.dot_general(a, b, (((1,), (1,)), ((), ())), preferred_element_type=F32)


def _nn(a, b):
    return jnp.dot(a, b, preferred_element_type=F32)


def _split(x):
    hi = x.astype(BF16)
    lo = (x - hi.astype(F32)).astype(BF16)
    return hi, lo


def _split3(x):
    hi = x.astype(BF16)
    r = x - hi.astype(F32)
    mid = r.astype(BF16)
    lo = (r - mid.astype(F32)).astype(BF16)
    return hi, mid, lo


def _nn_exact_lhs(sel, x):
    hi, mid, lo = _split3(x)
    return _nn(sel, hi) + _nn(sel, mid) + _nn(sel, lo)


def _round_robin(gens):
    gens = list(gens)
    results = [None] * len(gens)
    live = list(range(len(gens)))
    while live:
        for n in list(live):
            try:
                next(gens[n])
            except StopIteration as stop:
                results[n] = stop.value
                live.remove(n)
    return results


def _staggered(gens):
    pending = list(gens)
    active = []
    while pending or active:
        if pending:
            active.append(pending.pop(0))
        for g in list(active):
            try:
                next(g)
            except StopIteration:
                active.remove(g)


def _nn_hi(a, b):
    ah, al = _split(a)
    bh, bl = _split(b)
    return _nn(ah, bh) + _nn(ah, bl) + _nn(al, bh)


def _rmsnorm_body(x_ref, g_ref, o_ref):
    x = x_ref[...]
    ms = jnp.mean(x * x, axis=-1, keepdims=True)
    o_ref[...] = (x * lax.rsqrt(ms + NORM_EPS) * g_ref[...]).astype(o_ref.dtype)


def _rmsnorm(x, g, out_dtype):
    m, d = x.shape
    tm = min(m, ROWWISE_TM)
    return pl.pallas_call(
        _rmsnorm_body,
        grid=(m // tm,),
        in_specs=[pl.BlockSpec((tm, d), lambda i: (i, 0)),
                  pl.BlockSpec((1, d), lambda i: (0, 0))],
        out_specs=pl.BlockSpec((tm, d), lambda i: (i, 0)),
        out_shape=jax.ShapeDtypeStruct((m, d), out_dtype),
        compiler_params=_params(("arbitrary",)),
        name="rmsnorm",
    )(x, g.reshape(1, d))


N_ROT_TABLES = {"a": 2, "b": 2, "c": 0}


ROT_ROWS = 128


def _rot_a(src_ref, cs, width, z_ref, tabs, j, tn):
    c_ref, s_ref = tabs
    half = A_ROT // 2
    rows = min(ROT_ROWS, src_ref.shape[0])
    first_half = lax.broadcasted_iota(jnp.int32, (rows, A_HEAD_DIM), 1) < half
    for rb in range(0, src_ref.shape[0], rows):
        rs = slice(rb, rb + rows)
        for o in range(cs, cs + width, A_HEAD_DIM):
            x = src_ref[rs, o:o + A_HEAD_DIM]
            x_hi = pltpu.roll(x, A_HEAD_DIM - half, 1)
            x_lo = pltpu.roll(x, half, 1)
            z_ref[rs, o:o + A_HEAD_DIM] = x * c_ref[rs, :] + jnp.where(first_half, x_hi, x_lo) * s_ref[rs, :]


def _rot_b(src_ref, cs, width, z_ref, tabs, j, tn):
    c_ref, s_ref = tabs
    scale = jnp.where(j * tn >= B_HEADS * B_DK, B_DK ** -0.5, 1.0).astype(F32)
    half = B_DK // 2
    rows = min(ROT_ROWS, src_ref.shape[0])
    for rb in range(0, src_ref.shape[0], rows):
        rs = slice(rb, rb + rows)
        cos, sin = c_ref[rs, :] * scale, s_ref[rs, :] * scale
        for o in range(cs, cs + width, B_DK):
            x1 = src_ref[rs, o:o + half]
            x2 = src_ref[rs, o + half:o + B_DK]
            z_ref[rs, o:o + half] = x1 * cos - x2 * sin
            z_ref[rs, o + half:o + B_DK] = x1 * sin + x2 * cos


def _proj_body(*refs, kind, tn, ni, n_steps, lagged, side):
    n_tab = N_ROT_TABLES[kind]
    refs = list(refs)
    h_ref, hs_ref, w_ref = refs[:3]
    del refs[:3]
    w_side_ref = refs.pop(0) if side else None
    tabs, tabs_s = refs[:n_tab], refs[n_tab:2 * n_tab]
    del refs[:2 * n_tab]
    z_ref, zs_ref = refs[:2]
    del refs[:2]
    if side:
        z_side_ref, zs_side_ref = refs[:2]
        del refs[:2]
    wb_ref = refs.pop(0)
    s = pl.program_id(0)
    s_mm = jnp.minimum(s, n_steps - 1)
    j, i = s_mm // ni, s_mm % ni
    if lagged:
        acc_even, acc_odd = refs
        j_prev = jnp.maximum(s - 1, 0) // ni
    else:
        j_prev = j
    if side:
        @pl.when(j == 0)
        def _():
            w_side = w_side_ref[...].astype(BF16)
            z_side_ref[...] = _nn(h_ref[...], w_side)

            @pl.when(i == 0)
            def _():
                zs_side_ref[...] = _nn(hs_ref[...], w_side)
    epilogue = {"a": _rot_a, "b": _rot_b, "c": None}[kind]

    def is_rot(jj):
        if kind == "a":
            seg = (jj * tn) // (A_HEADS * A_HEAD_DIM)
            return jnp.logical_and(seg < 3 * len(A_GROUPS), seg % 3 < 2)
        return (jj * tn) < 2 * B_HEADS * B_DK

    slab = min(tn, MXU_N)

    def finish(src_ref, cs, o_ref, tables, jj, rot):
        if rot:
            epilogue(src_ref, cs, slab, o_ref, tables, jj, tn)
        elif src_ref is not o_ref:
            rows = min(ROT_ROWS, src_ref.shape[0])
            for rb in range(0, src_ref.shape[0], rows):
                o_ref[rb:rb + rows, cs:cs + slab] = src_ref[rb:rb + rows, cs:cs + slab]

    if lagged:
        @pl.when(s == 0)
        def _():
            acc_odd[...] = jnp.zeros_like(acc_odd)

    @pl.when(jnp.logical_and(i == 0, s < n_steps))
    def _():
        wb_ref[...] = w_ref[...].astype(BF16)

        def sample(rot):
            for cs in range(0, tn, slab):
                zs_ref[:, cs:cs + slab] = _nn(hs_ref[...], wb_ref[:, cs:cs + slab])
                finish(zs_ref, cs, zs_ref, tabs_s, j, rot)

        if epilogue is None:
            sample(False)
        else:
            pl.when(is_rot(j))(lambda: sample(True))
            pl.when(jnp.logical_not(is_rot(j)))(lambda: sample(False))

    def step(acc_new, acc_old, rot):
        for cs in range(0, tn, slab):
            acc_new[:, cs:cs + slab] = _nn(h_ref[...], wb_ref[:, cs:cs + slab])
            finish(acc_old, cs, z_ref, tabs, j_prev, rot)

    if lagged:
        even = s % 2 == 0
        variants = ((even, acc_even, acc_odd), (jnp.logical_not(even), acc_odd, acc_even))
    else:
        variants = ((None, z_ref, z_ref),)
    for on, acc_new, acc_old in variants:
        if epilogue is None:
            plain = functools.partial(step, acc_new, acc_old, False)
            plain() if on is None else pl.when(on)(plain)
        else:
            rot_prev = is_rot(j_prev)
            both = lambda a, b: b if a is None else jnp.logical_and(a, b)
            pl.when(both(on, rot_prev))(functools.partial(step, acc_new, acc_old, True))
            pl.when(both(on, jnp.logical_not(rot_prev)))(functools.partial(step, acc_new, acc_old, False))


def _proj(h, hs, w3, slot, n_out, kind, tabs, tabs_s, tab_blocks, tm, tn, w_side=None):
    m, k = h.shape
    ms = hs.shape[0]
    ni = m // tm
    n_steps = (n_out // tn) * ni
    lagged = kind == "a"
    side = w_side is not None
    assert not (side and lagged)
    mm = lambda s: jnp.minimum(s, n_steps - 1)
    prev = (lambda s: jnp.maximum(s - 1, 0)) if lagged else (lambda s: s)
    const = lambda s: (0, 0)
    in_specs = [pl.BlockSpec((tm, k), lambda s: (mm(s) % ni, 0)),
                pl.BlockSpec((ms, k), const),
                pl.BlockSpec((None, k, tn), lambda s: (slot, 0, mm(s) // ni))]
    args = [h, hs, w3]
    if side:
        in_specs.append(pl.BlockSpec((k, LANES), const))
        args.append(w_side)
    in_specs += [pl.BlockSpec((tm, LANES), lambda s: ((prev(s) % ni) % tab_blocks, 0)) for _ in tabs]
    in_specs += [pl.BlockSpec((ms, LANES), const) for _ in tabs_s]
    out_specs = [pl.BlockSpec((tm, tn), lambda s: (prev(s) % ni, prev(s) // ni)),
                 pl.BlockSpec((ms, tn), lambda s: (0, mm(s) // ni))]
    out_shape = [jax.ShapeDtypeStruct((m, n_out), F32), jax.ShapeDtypeStruct((ms, n_out), F32)]
    if side:
        out_specs += [pl.BlockSpec((tm, LANES), lambda s: (jnp.minimum(s, ni - 1), 0)),
                      pl.BlockSpec((ms, LANES), const)]
        out_shape += [jax.ShapeDtypeStruct((m, LANES), F32), jax.ShapeDtypeStruct((ms, LANES), F32)]
    scratch = [pltpu.VMEM((k, tn), BF16)]
    if lagged:
        scratch += [pltpu.VMEM((tm, tn), F32), pltpu.VMEM((tm, tn), F32)]
    return pl.pallas_call(
        functools.partial(_proj_body, kind=kind, tn=tn, ni=ni, n_steps=n_steps, lagged=lagged, side=side),
        grid=(n_steps + 1 if lagged else n_steps,),
        in_specs=in_specs,
        out_specs=out_specs,
        out_shape=out_shape,
        scratch_shapes=scratch,
        compiler_params=_params(("arbitrary",)),
        name="proj_" + kind,
    )(*args, *tabs, *tabs_s)


def _cast_body(x_ref, o_ref):
    o_ref[...] = x_ref[...].astype(o_ref.dtype)


def _cast_bf16(w3):
    n, k, d = w3.shape
    rows = n * k
    tr = min(rows, CAST_ROWS)
    out = pl.pallas_call(
        _cast_body,
        grid=(rows // tr,),
        in_specs=[pl.BlockSpec((tr, d), lambda i: (i, 0))],
        out_specs=pl.BlockSpec((tr, d), lambda i: (i, 0)),
        out_shape=jax.ShapeDtypeStruct((rows, d), BF16),
        compiler_params=_params(("arbitrary",)),
        name="cast_bf16",
    )(w3.reshape(rows, d))
    return out.reshape(n, k, d)


def _outproj_body(y_ref, ys_ref, w_ref, x_ref, xs_ref, g_ref, *out_refs, cast_w, want_x):
    if cast_w:
        *out_refs, wb_ref = out_refs

        @pl.when(pl.program_id(0) == 0)
        def _():
            wb_ref[...] = w_ref[...].astype(BF16)
    else:
        wb_ref = w_ref
    if want_x:
        xo_ref, xso_ref, ho_ref, hso_ref = out_refs
    else:
        (ho_ref, hso_ref), xo_ref, xso_ref = out_refs, None, None

    def rows(y_in, x_in, x_out, h_out):
        xn = x_in[...] + _nn(y_in[...], wb_ref[...])
        if x_out is not None:
            x_out[...] = xn
        ms = jnp.mean(xn * xn, axis=-1, keepdims=True)
        h_out[...] = (xn * lax.rsqrt(ms + NORM_EPS) * g_ref[...]).astype(h_out.dtype)

    pl.when(pl.program_id(0) == 0)(lambda: rows(ys_ref, xs_ref, xso_ref, hso_ref))
    rows(y_ref, x_ref, xo_ref, ho_ref)


def _outproj(y, ys, w3, slot, x, xs, g, h_dtype, tm, want_x):
    m, kd = y.shape
    ms, d = xs.shape
    cast_w = w3.dtype == F32
    const = lambda i: (0, 0)
    row_spec = pl.BlockSpec((tm, d), lambda i: (i, 0))
    s_spec = pl.BlockSpec((ms, d), const)
    shape = lambda rows, dt: jax.ShapeDtypeStruct((rows, d), dt)
    outs = pl.pallas_call(
        functools.partial(_outproj_body, cast_w=cast_w, want_x=want_x),
        scratch_shapes=[pltpu.VMEM((kd, d), BF16)] if cast_w else [],
        grid=(m // tm,),
        in_specs=[pl.BlockSpec((tm, kd), lambda i: (i, 0)),
                  pl.BlockSpec((ms, kd), const),
                  pl.BlockSpec((None, kd, d), lambda i: (slot, 0, 0), pipeline_mode=pl.Buffered(1)),
                  row_spec, s_spec,
                  pl.BlockSpec((1, d), const)],
        out_specs=([row_spec, s_spec] if want_x else []) + [row_spec, s_spec],
        out_shape=([shape(m, F32), shape(ms, F32)] if want_x else []) + [shape(m, h_dtype), shape(ms, h_dtype)],
        compiler_params=_params(("arbitrary",)),
        name="outproj",
    )(y, ys, w3, x, xs, g.reshape(1, d))
    return tuple(outs) if want_x else (None, None, *outs)


LOG2_E = 1.4426950408889634


def _rows(start, size, stride):
    return pl.ds(start, size) if stride == 1 else pl.ds(start, size, stride=stride)


def _attn_prompt_body(*refs, seq):
    qkv_refs, gate_ref, bias_first_ref, bias_band_ref, y_ref = refs[:9], *refs[9:13]
    o_s, d_s, m_s = refs[13:]
    q_scale = (A_HEAD_DIM ** -0.5) * LOG2_E
    ones = jnp.ones((A_BAND, A_HEAD_DIM), BF16)
    for g, (_, dil) in enumerate(A_GROUPS):
        q_ref, k_ref, v_ref = qkv_refs[3 * g:3 * g + 3]

        def load_kv(rows, k_ref=k_ref, v_ref=v_ref):
            return k_ref[rows, :].astype(BF16), jnp.concatenate([v_ref[rows, :].astype(BF16), ones], axis=1)

        def chain(r, n0, n1, g=g, dil=dil, q_ref=q_ref, load_kv=load_kv):
            block_rows = lambda n: _rows(r + dil * A_BAND * n, A_BAND, dil)
            k_prev, v_prev = load_kv(block_rows(n0 - 1)) if n0 > 0 else (None, None)
            for n in range(n0, n1):
                rows = block_rows(n)
                q = (q_ref[rows, :] * q_scale).astype(BF16)
                k_cur, v_cur = load_kv(rows)
                if n == 0:
                    keys, vals, bias_ref = k_cur, v_cur, bias_first_ref
                else:
                    keys = jnp.concatenate([k_prev, k_cur], axis=0)
                    vals = jnp.concatenate([v_prev, v_cur], axis=0)
                    bias_ref = bias_band_ref
                k_prev, v_prev = k_cur, v_cur
                s = _nt(q, keys)
                yield
                s = s + bias_ref[...]
                mx = jnp.max(s, axis=-1, keepdims=True)
                od = _nn(jnp.exp2(s - mx).astype(BF16), vals)
                yield
                o_s[g, rows, :] = od[:, :A_HEAD_DIM]
                d_s[g, rows, :] = od[:, A_HEAD_DIM:]
                m_s[g, rows, :] = jnp.broadcast_to(mx, (A_BAND, A_HEAD_DIM))

        nb = seq // dil // A_BAND
        if dil == 1:
            chains = [chain(0, 0, nb // 2), chain(0, nb // 2, nb)]
        else:
            chains = [chain(r, 0, nb) for r in range(dil)]
        _staggered(chains)
    for tile in range(seq // A_BAND):
        rs = slice(tile * A_BAND, (tile + 1) * A_BAND)
        m1, m2, m3 = m_s[0, rs, :], m_s[1, rs, :], m_s[2, rs, :]
        mx = jnp.maximum(jnp.maximum(m1, m2), m3)
        f1, f2, f3 = jnp.exp2(m1 - mx), jnp.exp2(m2 - mx), jnp.exp2(m3 - mx)
        num = f1 * o_s[0, rs, :] + f2 * o_s[1, rs, :] + f3 * o_s[2, rs, :]
        den = f1 * d_s[0, rs, :] + f2 * d_s[1, rs, :] + f3 * d_s[2, rs, :]
        y_ref[rs, :] = (num / den * _silu(gate_ref[rs, :])).astype(y_ref.dtype)


def _attn_prompt(z, batch, seq):
    n_groups = len(A_GROUPS)
    in_specs = []
    for c in range(3 * n_groups + 1):
        in_specs.append(pl.BlockSpec((seq, A_HEAD_DIM), lambda b, h, c=c: (b, c * A_HEADS + h)))
    row = jnp.arange(A_BAND)[:, None]
    col = jnp.arange(2 * A_BAND)[None, :]
    bias_first = jnp.where(col[:, :A_BAND] <= row, 0.0, -jnp.inf).astype(F32)
    bias_band = jnp.where((col >= row) & (col <= row + A_BAND), 0.0, -jnp.inf).astype(F32)
    in_specs.append(pl.BlockSpec((A_BAND, A_BAND), lambda b, h: (0, 0)))
    in_specs.append(pl.BlockSpec((A_BAND, 2 * A_BAND), lambda b, h: (0, 0)))
    return pl.pallas_call(
        functools.partial(_attn_prompt_body, seq=seq),
        grid=(batch, A_HEADS),
        in_specs=in_specs,
        out_specs=pl.BlockSpec((seq, A_HEAD_DIM), lambda b, h: (b, h)),
        out_shape=jax.ShapeDtypeStruct((batch * seq, A_HEADS * A_HEAD_DIM), BF16),
        scratch_shapes=[pltpu.VMEM((n_groups, seq, A_HEAD_DIM), F32)] * 3,
        compiler_params=_params(("arbitrary", "arbitrary")),
        name="attn_prompt",
    )(*([z] * (3 * n_groups + 1)), bias_first, bias_band)


def _attn_sample_body(z_ref, c1_ref, c2_ref, c3_ref, y_ref):
    t = pl.program_id(1)
    scale = A_HEAD_DIM ** -0.5
    m_idx = lax.broadcasted_iota(jnp.int32, (A_BAND, A_HEADS, 1), 0)
    n_idx = lax.broadcasted_iota(jnp.int32, (SAMPLE_PAD, A_HEADS, 1), 0)
    caches = (c1_ref, c2_ref, c3_ref)
    outs, lses = [], []
    for g, (_, dil) in enumerate(A_GROUPS):
        base = 3 * g * A_HEADS
        q = z_ref[0, t, base:base + A_HEADS, :]
        kc = caches[g][0, :, 0:A_HEADS, :]
        vc = caches[g][0, :, A_HEADS:2 * A_HEADS, :]
        kn = z_ref[0, :, base + A_HEADS:base + 2 * A_HEADS, :]
        vn = z_ref[0, :, base + 2 * A_HEADS:base + 3 * A_HEADS, :]
        s_c = jnp.sum(kc * q[None], axis=-1, keepdims=True) * scale
        s_n = jnp.sum(kn * q[None], axis=-1, keepdims=True) * scale
        if dil == 1:
            s_c = jnp.where(m_idx >= t, s_c, -jnp.inf)
            s_n = jnp.where(n_idx <= t, s_n, -jnp.inf)
        else:
            s_n = jnp.where(n_idx == t, s_n, -jnp.inf)
        mx = jnp.maximum(jnp.max(s_c, axis=0), jnp.max(s_n, axis=0))
        p_c = jnp.exp(s_c - mx[None])
        p_n = jnp.exp(s_n - mx[None])
        den = jnp.sum(p_c, axis=0) + jnp.sum(p_n, axis=0)
        num = jnp.sum(p_c * vc, axis=0) + jnp.sum(p_n * vn, axis=0)
        outs.append(num / den)
        lses.append(mx + jnp.log(den))
    lmax = jnp.maximum(jnp.maximum(lses[0], lses[1]), lses[2])
    es = [jnp.exp(l - lmax) for l in lses]
    tot = es[0] + es[1] + es[2]
    o = (es[0] / tot) * outs[0] + (es[1] / tot) * outs[1] + (es[2] / tot) * outs[2]
    gate = z_ref[0, t, 3 * len(A_GROUPS) * A_HEADS:(3 * len(A_GROUPS) + 1) * A_HEADS, :]
    y_ref[0, 0] = o * _silu(gate)


def _attn_sample(zs, caches, slot, batch, t_real):
    n_in = zs.shape[1]
    z5 = zs.reshape(batch, SAMPLE_PAD, n_in // A_HEAD_DIM, A_HEAD_DIM)
    in_specs = [pl.BlockSpec((1, SAMPLE_PAD, n_in // A_HEAD_DIM, A_HEAD_DIM), lambda b, t: (b, 0, 0, 0))]
    args = [z5]
    for g, (win, dil) in enumerate(A_GROUPS):
        n_a = caches[g].shape[0]
        cv = caches[g].reshape(n_a * batch, win // dil, dil * 2 * A_HEADS, A_HEAD_DIM)
        if dil == 1:
            idx = lambda b, t: (slot * batch + b, 0, 0, 0)
        else:
            idx = lambda b, t: (slot * batch + b, 0, t, 0)
        in_specs.append(pl.BlockSpec((1, A_BAND, 2 * A_HEADS, A_HEAD_DIM), idx))
        args.append(cv)
    y = pl.pallas_call(
        _attn_sample_body,
        grid=(batch, t_real),
        in_specs=in_specs,
        out_specs=pl.BlockSpec((1, 1, A_HEADS, A_HEAD_DIM), lambda b, t: (b, t, 0, 0)),
        out_shape=jax.ShapeDtypeStruct((batch, t_real, A_HEADS, A_HEAD_DIM), F32),
        compiler_params=_params(("arbitrary", "arbitrary")),
        name="attn_sample",
    )(*args)
    return y.reshape(batch, t_real, A_HEADS * A_HEAD_DIM)


def _kv_rows_body(*refs, n_layers):
    out_ref = refs[-1]
    layer = pl.program_id(0)
    for l in range(n_layers):
        k_ref, v_ref = refs[2 * l], refs[2 * l + 1]

        @pl.when(layer == l)
        def _():
            sub = SUBLANES
            for base, src in ((0, k_ref), (A_HEADS, v_ref)):
                for c0 in range(0, A_HEADS, sub):
                    cols = [src[:, (c0 + c) * A_HEAD_DIM:(c0 + c + 1) * A_HEAD_DIM] for c in range(sub)]
                    out_ref[:, base + c0:base + c0 + sub, :] = jnp.swapaxes(jnp.stack(cols, axis=0), 0, 1)


def _kv_rows(zs, g, batch, seq, keep):
    n_layers = len(zs)
    width = A_HEADS * A_HEAD_DIM
    tt = min(keep, KV_ROWS)
    first = (seq - keep) // tt
    in_specs, args = [], []
    for l in range(n_layers):
        for c in (1, 2):
            def idx(s, b, i, l=l, c=c):
                return (jnp.where(s == l, b * (seq // tt) + first + i, 0), 3 * g + c)
            in_specs.append(pl.BlockSpec((tt, width), idx))
            args.append(zs[l])
    out = pl.pallas_call(
        functools.partial(_kv_rows_body, n_layers=n_layers),
        grid=(n_layers, batch, keep // tt),
        in_specs=in_specs,
        out_specs=pl.BlockSpec((None, None, tt, 2 * A_HEADS, A_HEAD_DIM), lambda s, b, i: (s, b, i, 0, 0)),
        out_shape=jax.ShapeDtypeStruct((n_layers, batch, keep, 2 * A_HEADS, A_HEAD_DIM), F32),
        compiler_params=_params(("arbitrary", "arbitrary", "arbitrary")),
        name="kv_rows_g%d" % g,
    )(*args)
    return out.reshape(n_layers, batch, keep, 2, A_HEADS, A_HEAD_DIM)


def _tn(a, b):
    return lax.dot_general(a, b, (((0,), (0,)), ((), ())), preferred_element_type=F32)


def _retention_body(*refs, has_state):
    if has_state:
        q_ref, k_ref, v_ref, gate_ref, dec_ref, xi_ref, kd_ref, sd_ref, gn_ref, s0_ref, y_ref, st_ref = refs
    else:
        q_ref, k_ref, v_ref, gate_ref, dec_ref, xi_ref, kd_ref, sd_ref, gn_ref, y_ref, st_ref = refs
    c = pl.program_id(1)

    @pl.when(c == 0)
    def _():
        if has_state:
            st_ref[...] = s0_ref[...]
        else:
            st_ref[...] = jnp.zeros_like(st_ref)

    def head(h):
        ks = slice(h * B_DK, (h + 1) * B_DK)
        vs = slice(h * B_DV, (h + 1) * B_DV)
        qb = q_ref[:, ks].astype(BF16)
        kf = k_ref[:, ks]
        vb = v_ref[:, vs].astype(BF16)
        state = st_ref[0, h]
        scores = _nt(qb, kf.astype(BF16))
        cross = _nn(qb, state.astype(BF16))
        k_dec = (kf * kd_ref[h]).astype(BF16)
        st_ref[0, h] = sd_ref[h] * state + _tn(k_dec, vb)
        yield
        o = _nn((scores * dec_ref[h]).astype(BF16), vb) + cross * xi_ref[h]
        yield
        mu = jnp.mean(o, axis=-1, keepdims=True)
        var = jnp.mean(jnp.square(o - mu), axis=-1, keepdims=True)
        yv = (o - mu) * lax.rsqrt(var + NORM_EPS) * gn_ref[h]
        y_ref[:, vs] = (yv * _silu(gate_ref[:, vs])).astype(y_ref.dtype)

    _round_robin([head(h) for h in range(B_HEADS)])


def _retention(z, batch, chunk, t_real, gn, state0):
    rows = z.shape[0]
    nc = rows // (batch * chunk)
    lg = jnp.log(1.0 - 2.0 ** (-5.0 - jnp.arange(B_HEADS, dtype=F32)))[:, None, None]
    idx = jnp.arange(chunk, dtype=F32)
    diff = idx[:, None] - idx[None, :]
    dec = jnp.where(diff >= 0, jnp.exp(lg * jnp.maximum(diff, 0.0)), 0.0)
    xi = jnp.exp(lg * (idx[None, :, None] + 1.0))
    live = idx[None, :, None] < t_real
    kd = jnp.where(live, jnp.exp(lg * jnp.where(live, t_real - 1.0 - idx[None, :, None], 0.0)), 0.0)
    sd = jnp.exp(lg * t_real)
    n_k, n_v = B_HEADS * B_DK, B_HEADS * B_DV
    const = lambda b, c: (0, 0, 0)
    in_specs = [pl.BlockSpec((chunk, n_k), lambda b, c: (b * nc + c, 0)),
                pl.BlockSpec((chunk, n_k), lambda b, c: (b * nc + c, 1)),
                pl.BlockSpec((chunk, n_v), lambda b, c: (b * nc + c, 2 * n_k // n_v)),
                pl.BlockSpec((chunk, n_v), lambda b, c: (b * nc + c, 2 * n_k // n_v + 1)),
                pl.BlockSpec((B_HEADS, chunk, chunk), const),
                pl.BlockSpec((B_HEADS, chunk, 1), const),
                pl.BlockSpec((B_HEADS, chunk, 1), const),
                pl.BlockSpec((B_HEADS, 1, 1), const),
                pl.BlockSpec((B_HEADS, 1, B_DV), const)]
    args = [z, z, z, z, dec, xi, kd, sd, gn.reshape(B_HEADS, 1, B_DV)]
    state_spec = pl.BlockSpec((1, B_HEADS, B_DK, B_DV), lambda b, c: (b, 0, 0, 0))
    if state0 is not None:
        in_specs.append(state_spec)
        args.append(state0)
    return pl.pallas_call(
        functools.partial(_retention_body, has_state=state0 is not None),
        grid=(batch, nc),
        in_specs=in_specs,
        out_specs=[pl.BlockSpec((chunk, n_v), lambda b, c: (b * nc + c, 0)), state_spec],
        out_shape=[jax.ShapeDtypeStruct((rows, n_v), BF16),
                   jax.ShapeDtypeStruct((batch, B_HEADS, B_DK, B_DV), F32)],
        compiler_params=_params(("arbitrary", "arbitrary")),
        name="retention",
    )(*args)


def _log_sigmoid(x):
    return jnp.minimum(x, 0.0) - jnp.log(1.0 + jnp.exp(-jnp.abs(x)))


def _gla_head(q, k, v, lr, wg, bg, state, consts, chunk, sub, t_sub):
    nsub = chunk // sub
    tri, pair_live, pair_col, pick_t = consts
    pre = _nn_hi(lr, wg)
    yield
    log_a = _log_sigmoid(pre + bg) * (LOG2_E / C_TAU)
    bl = _nn_exact_lhs(tri, log_a)
    yield

    def block_rows(vals):
        return jnp.concatenate([jnp.broadcast_to(x, (sub, C_DK)) for x in vals], axis=0)

    tot = [bl[j * sub + t_sub - 1:j * sub + t_sub, :] for j in range(nsub)]
    beta = [jnp.zeros((1, C_DK), F32)]
    for j in range(nsub):
        beta.append(beta[-1] + tot[j])
    qs = q * (C_DK ** -0.5)
    q_in = qs * jnp.exp2(bl)
    k_out = k * jnp.exp2(block_rows(tot) - bl)
    if t_sub < sub:
        k_out = jnp.where(lax.broadcasted_iota(jnp.int32, (chunk, 1), 0) % sub < t_sub, k_out, 0.0)
    vb = v.astype(BF16)

    o = _nn((q_in * block_rows([jnp.exp2(x) for x in beta[:nsub]])).astype(BF16), state.astype(BF16))
    yield

    def rep_rows(x):
        return jnp.concatenate([jnp.broadcast_to(x[t:t + 1, :], (sub, C_DK)) for t in range(sub)], axis=0)

    def tile_rows(x):
        return jnp.concatenate([x] * sub, axis=0)

    a_rows = []
    for i in range(nsub):
        r = slice(i * sub, (i + 1) * sub)
        pair = rep_rows(qs[r]) * tile_rows(k[r]) * jnp.exp2(rep_rows(bl[r]) - tile_rows(bl[r]))
        att = jnp.where(pair_live, jnp.sum(pair, axis=1, keepdims=True), 0.0)
        placed = jnp.where(pair_col == i * sub, att, 0.0).astype(BF16)
        a_i = _nn(pick_t, placed)
        if i > 0:
            between = [jnp.broadcast_to(jnp.exp2(beta[i] - beta[j + 1]), (sub, C_DK)) for j in range(i)]
            between.append(jnp.zeros(((nsub - i) * sub, C_DK), F32))
            a_i += _nt(q_in[r].astype(BF16), (k_out * jnp.concatenate(between, axis=0)).astype(BF16))
        a_rows.append(a_i)
        if i % 2 == 1:
            yield
    a = a_rows[0] if nsub == 1 else jnp.concatenate(a_rows, axis=0)
    o += _nn(a.astype(BF16), vb)
    yield
    k_end = k_out * block_rows([jnp.exp2(beta[nsub] - beta[j + 1]) for j in range(nsub)])
    d_col = jnp.broadcast_to(jnp.exp2(beta[nsub]), (SUBLANES, C_DK)).T[:, 0:1]
    return o, d_col * state + _tn(k_end.astype(BF16), vb)


def _gla_consts(chunk, sub):
    ri = lax.broadcasted_iota(jnp.int32, (chunk, chunk), 0)
    ci = lax.broadcasted_iota(jnp.int32, (chunk, chunk), 1)
    tri = jnp.logical_and(ri // sub == ci // sub, ci <= ri).astype(BF16)
    pj = lax.broadcasted_iota(jnp.int32, (sub * sub, 1), 0)
    pair_live = pj // sub >= pj % sub
    pc = lax.broadcasted_iota(jnp.int32, (sub * sub, chunk), 1)
    ps = lax.broadcasted_iota(jnp.int32, (sub * sub, chunk), 0) % sub
    et = lax.broadcasted_iota(jnp.int32, (sub, sub * sub), 0)
    ej = lax.broadcasted_iota(jnp.int32, (sub, sub * sub), 1)
    pick_t = (ej // sub == et).astype(BF16)
    return tri, pair_live, pc - ps, pick_t


def _gla_body(*refs, has_state, chunk, sub, t_sub):
    consts = _gla_consts(chunk, sub)
    if has_state:
        q_ref, k_ref, v_ref, gate_ref, lr_ref, wg_ref, bg_ref, gn_ref, s0_ref, y_ref, st_ref = refs
    else:
        q_ref, k_ref, v_ref, gate_ref, lr_ref, wg_ref, bg_ref, gn_ref, y_ref, st_ref = refs
    c = pl.program_id(1)

    @pl.when(c == 0)
    def _():
        if has_state:
            st_ref[...] = s0_ref[...]
        else:
            st_ref[...] = jnp.zeros_like(st_ref)

    lr = lr_ref[:, 0:C_RANK]
    heads = []
    for h in range(C_HEADS):
        ks = slice(h * C_DK, (h + 1) * C_DK)
        vs = slice(h * C_DV, (h + 1) * C_DV)
        heads.append(_gla_head(q_ref[:, ks], k_ref[:, ks], v_ref[:, vs], lr, wg_ref[:, ks], bg_ref[:, ks],
                               st_ref[0, h], consts, chunk, sub, t_sub))
    for h, (o, new_state) in enumerate(_round_robin(heads)):
        vs = slice(h * C_DV, (h + 1) * C_DV)
        st_ref[0, h] = new_state
        yv = o * lax.rsqrt(jnp.mean(o * o, axis=-1, keepdims=True) + NORM_EPS) * gn_ref[:, vs]
        y_ref[:, vs] = (yv * _silu(gate_ref[:, vs])).astype(y_ref.dtype)


def _gla(z, z_lr, batch, chunk, t_real, wg3, bg3, gn3, slot, state0):
    rows = z.shape[0]
    nc = rows // (batch * chunk)
    sub, t_sub = (C_SUB, C_SUB) if t_real == chunk else (chunk, t_real)
    n_k, n_v = C_HEADS * C_DK, C_HEADS * C_DV
    n_slots = wg3.shape[0]
    in_specs = [pl.BlockSpec((chunk, n_k), lambda b, c: (b * nc + c, 0)),
                pl.BlockSpec((chunk, n_k), lambda b, c: (b * nc + c, 1)),
                pl.BlockSpec((chunk, n_v), lambda b, c: (b * nc + c, 2 * n_k // n_v)),
                pl.BlockSpec((chunk, n_v), lambda b, c: (b * nc + c, 2 * n_k // n_v + 1)),
                pl.BlockSpec((chunk, LANES), lambda b, c: (b * nc + c, 0)),
                pl.BlockSpec((None, C_RANK, n_k), lambda b, c: (slot, 0, 0)),
                pl.BlockSpec((None, 1, n_k), lambda b, c: (slot, 0, 0)),
                pl.BlockSpec((None, 1, n_v), lambda b, c: (slot, 0, 0))]
    args = [z, z, z, z, z_lr, wg3, bg3.reshape(n_slots, 1, n_k), gn3.reshape(n_slots, 1, n_v)]
    state_spec = pl.BlockSpec((1, C_HEADS, C_DK, C_DV), lambda b, c: (b, 0, 0, 0))
    if state0 is not None:
        in_specs.append(state_spec)
        args.append(state0)
    return pl.pallas_call(
        functools.partial(_gla_body, has_state=state0 is not None, chunk=chunk, sub=sub, t_sub=t_sub),
        grid=(batch, nc),
        in_specs=in_specs,
        out_specs=[pl.BlockSpec((chunk, n_v), lambda b, c: (b * nc + c, 0)), state_spec],
        out_shape=[jax.ShapeDtypeStruct((rows, n_v), BF16),
                   jax.ShapeDtypeStruct((batch, C_HEADS, C_DK, C_DV), F32)],
        compiler_params=_params(("arbitrary", "arbitrary")),
        name="gla",
    )(*args)


def _rot_tables_a(pos):
    half = A_ROT // 2
    inv_freq = A_ROPE_THETA ** (-jnp.arange(half, dtype=F32) / half)
    ang = pos.astype(F32)[:, None] * inv_freq[None, :]
    cos, sin = jnp.cos(ang), jnp.sin(ang)
    n = pos.shape[0]
    rest = A_HEAD_DIM - A_ROT
    c = jnp.concatenate([cos, cos, jnp.ones((n, rest), F32)], axis=1)
    s = jnp.concatenate([-sin, sin, jnp.zeros((n, rest), F32)], axis=1)
    return c, s


def _rot_tables_b(pos):
    half = B_DK // 2
    inv_freq = B_ROPE_THETA ** (-jnp.arange(half, dtype=F32) / half)
    ang = pos.astype(F32)[:, None] * inv_freq[None, :]
    return jnp.cos(ang), jnp.sin(ang)


def kernel(x_prompt, x_sample, cache_a_kv1, cache_a_kv2, cache_a_kv3, state_b, state_c, norm_g, final_g,
           w_in_a, w_out_a, w_in_b, gn_b, w_out_b, w_in_c, w_gate2_c, b_gate_c, gn_c, w_out_c):
    bp, seq, d = x_prompt.shape
    bs, t_real, _ = x_sample.shape
    caches_a = (cache_a_kv1, cache_a_kv2, cache_a_kv3)
    width = A_HEADS * A_HEAD_DIM

    xp = x_prompt.reshape(bp * seq, d)
    xs = jnp.pad(x_sample, ((0, 0), (0, SAMPLE_PAD - t_real), (0, 0))).reshape(bs * SAMPLE_PAD, d)
    ms = xs.shape[0]
    tm_p = min(PROJ_TM, seq)

    pos_p = jnp.arange(seq)
    pos_s = jnp.tile(PAST_LEN + jnp.arange(SAMPLE_PAD), bs)
    tabs_a_p, tabs_a_s = _rot_tables_a(pos_p), _rot_tables_a(pos_s)
    tabs_b_p, tabs_b_s = _rot_tables_b(pos_p), _rot_tables_b(pos_s)

    hp = _rmsnorm(xp, norm_g[0], BF16)
    hs = _rmsnorm(xs, norm_g[0], BF16)
    def maybe_cast(w3):
        return _cast_bf16(w3) if w3.shape[1] * w3.shape[2] * 6 > OUTPROJ_CAST_LIMIT else w3

    w_out_a, w_out_b, w_out_c = maybe_cast(w_out_a), maybe_cast(w_out_b), maybe_cast(w_out_c)

    a_z_p, a_rows_s = [], []
    b_states_p, b_states_s, c_states_p, c_states_s = [], [], [], []
    n_layers = len(LAYER_KINDS)
    for i in range(n_layers):
        kind, slot = LAYER_KINDS[i], LAYER_SLOTS[i]
        last = i == n_layers - 1
        g_next = final_g if last else norm_g[i + 1]
        h_dtype = F32 if last else BF16
        if kind == 0:
            n_in = w_in_a.shape[2]
            zp, zs = _proj(hp, hs, w_in_a, slot, n_in, "a", tabs_a_p, tabs_a_s, seq // tm_p, tm_p, PROJ_TN)
            yp = _attn_prompt(zp, bp, seq)
            ys = _attn_sample(zs, caches_a, slot, bs, t_real)
            ys = jnp.pad(ys, ((0, 0), (0, SAMPLE_PAD - t_real), (0, 0))).reshape(ms, width).astype(BF16)
            zs3 = zs.reshape(bs, SAMPLE_PAD, n_in)
            rows_s = []
            for g in range(len(A_GROUPS)):
                lo = (3 * g + 1) * width
                rows_s.append(zs3[:, :t_real, lo:lo + 2 * width].reshape(bs, t_real, 2, A_HEADS, A_HEAD_DIM))
            a_z_p.append(zp)
            a_rows_s.append(rows_s)
            w_out = w_out_a
        elif kind == 1:
            n_in = w_in_b.shape[2]
            zp, zs = _proj(hp, hs, w_in_b, slot, n_in, "b", tabs_b_p, tabs_b_s, seq // tm_p, tm_p, PROJ_TN)
            yp, st_p = _retention(zp, bp, min(B_CHUNK, seq), min(B_CHUNK, seq), gn_b[slot], None)
            ys, st_s = _retention(zs, bs, SAMPLE_PAD, t_real, gn_b[slot], state_b[slot])
            b_states_p.append(st_p)
            b_states_s.append(st_s)
            w_out = w_out_b
        else:
            n_main = 2 * C_HEADS * C_DK + 2 * C_HEADS * C_DV
            w_lr = jnp.pad(w_in_c[slot][:, n_main:], ((0, 0), (0, LANES - C_RANK)))
            zp, zs, zp_lr, zs_lr = _proj(hp, hs, w_in_c, slot, n_main, "c", (), (), 1, tm_p, PROJ_TN, w_side=w_lr)
            yp, st_p = _gla(zp, zp_lr, bp, min(C_CHUNK, seq), min(C_CHUNK, seq),
                            w_gate2_c, b_gate_c, gn_c, slot, None)
            ys, st_s = _gla(zs, zs_lr, bs, SAMPLE_PAD, t_real, w_gate2_c, b_gate_c, gn_c, slot, state_c[slot])
            c_states_p.append(st_p)
            c_states_s.append(st_s)
            w_out = w_out_c
        xp, xs, hp, hs = _outproj(yp, ys, w_out, slot, xp, xs, g_next, h_dtype, OUTPROJ_TM, not last)

    y_prompt = hp.reshape(bp, seq, d)
    y_sample = hs.reshape(bs, SAMPLE_PAD, d)[:, :t_real]
    stack = lambda rows, g: jnp.stack([r[g] for r in rows])
    kv_p = [_kv_rows(a_z_p, g, bp, seq, min(win, seq)) for g, (win, _) in enumerate(A_GROUPS)]
    return (y_prompt, y_sample,
            kv_p[0], kv_p[1], kv_p[2],
            jnp.stack(b_states_p), jnp.stack(c_states_p),
            stack(a_rows_s, 0), stack(a_rows_s, 1), stack(a_rows_s, 2),
            jnp.stack(b_states_s), jnp.stack(c_states_s))
```

```python
import functools

import jax
import jax.numpy as jnp
from jax import lax
from jax.experimental import pallas as pl
from jax.experimental.pallas import tpu as pltpu

F32 = jnp.float32
BF16 = jnp.bfloat16

PAST_LEN = 16384
NORM_EPS = 1e-6
A_GROUPS = ((128, 1), (512, 4), (2048, 16))
A_HEADS = 16
A_HEAD_DIM = 128
A_ROT = A_HEAD_DIM // 4
A_ROPE_THETA = 500000.0
A_BAND = 128
B_HEADS = 8
B_DK = 256
B_DV = 512
B_CHUNK = 128
B_ROPE_THETA = 10000.0
C_HEADS = 4
C_DK = 256
C_DV = 512
C_RANK = 16
C_TAU = 16.0
C_CHUNK = 64
C_SUB = 8
LAYER_KINDS = (0, 1, 2, 0)
LAYER_SLOTS = (0, 0, 0, 1)

SAMPLE_PAD = 16
LANES = 128
SUBLANES = 8
MXU_N = 256
OUTPROJ_CAST_LIMIT = 32 * 1024 * 1024
VMEM_LIMIT = 56 * 1024 * 1024

PROJ_TM = 1024
PROJ_TN = 1024
OUTPROJ_TM = 512
ROWWISE_TM = 512
CAST_ROWS = 1024
KV_ROWS = 512
CHUNKS_PER_STEP = 2


def _params(sem, vmem=VMEM_LIMIT):
    return pltpu.CompilerParams(dimension_semantics=sem, vmem_limit_bytes=vmem)


def _silu(g):
    return g * (1.0 / (1.0 + jnp.exp(-g)))


def _nt(a, b):
    return lax.dot_general(a, b, (((1,), (1,)), ((), ())), preferred_element_type=F32)


def _nn(a, b):
    return jnp.dot(a, b, preferred_element_type=F32)


def _split(x):
    hi = x.astype(BF16)
    lo = (x - hi.astype(F32)).astype(BF16)
    return hi, lo


def _split3(x):
    hi = x.astype(BF16)
    r = x - hi.astype(F32)
    mid = r.astype(BF16)
    lo = (r - mid.astype(F32)).astype(BF16)
    return hi, mid, lo


def _nn_exact_lhs(sel, x):
    hi, mid, lo = _split3(x)
    return _nn(sel, hi) + _nn(sel, mid) + _nn(sel, lo)


def _round_robin(gens):
    gens = list(gens)
    results = [None] * len(gens)
    live = list(range(len(gens)))
    while live:
        for n in list(live):
            try:
                next(gens[n])
            except StopIteration as stop:
                results[n] = stop.value
                live.remove(n)
    return results


def _staggered(gens):
    pending = list(gens)
    active = []
    while pending or active:
        if pending:
            active.append(pending.pop(0))
        for g in list(active):
            try:
                next(g)
            except StopIteration:
                active.remove(g)


def _nn_hi(a, b):
    ah, al = _split(a)
    bh, bl = _split(b)
    return _nn(ah, bh) + _nn(ah, bl) + _nn(al, bh)


def _rmsnorm_body(x_ref, g_ref, o_ref):
    x = x_ref[...]
    ms = jnp.mean(x * x, axis=-1, keepdims=True)
    o_ref[...] = (x * lax.rsqrt(ms + NORM_EPS) * g_ref[...]).astype(o_ref.dtype)


def _rmsnorm(x, g, out_dtype):
    m, d = x.shape
    tm = min(m, ROWWISE_TM)
    return pl.pallas_call(
        _rmsnorm_body,
        grid=(m // tm,),
        in_specs=[pl.BlockSpec((tm, d), lambda i: (i, 0)),
                  pl.BlockSpec((1, d), lambda i: (0, 0))],
        out_specs=pl.BlockSpec((tm, d), lambda i: (i, 0)),
        out_shape=jax.ShapeDtypeStruct((m, d), out_dtype),
        compiler_params=_params(("arbitrary",)),
        name="rmsnorm",
    )(x, g.reshape(1, d))


N_ROT_TABLES = {"a": 2, "b": 2, "c": 0}


ROT_ROWS = 128


def _rot_a(src_ref, cs, width, z_ref, tabs, j, tn):
    c_ref, s_ref = tabs
    half = A_ROT // 2
    rows = min(ROT_ROWS, src_ref.shape[0])
    first_half = lax.broadcasted_iota(jnp.int32, (rows, A_HEAD_DIM), 1) < half
    for rb in range(0, src_ref.shape[0], rows):
        rs = slice(rb, rb + rows)
        for o in range(cs, cs + width, A_HEAD_DIM):
            x = src_ref[rs, o:o + A_HEAD_DIM]
            x_hi = pltpu.roll(x, A_HEAD_DIM - half, 1)
            x_lo = pltpu.roll(x, half, 1)
            z_ref[rs, o:o + A_HEAD_DIM] = x * c_ref[rs, :] + jnp.where(first_half, x_hi, x_lo) * s_ref[rs, :]


def _rot_b(src_ref, cs, width, z_ref, tabs, j, tn):
    c_ref, s_ref = tabs
    scale = jnp.where(j * tn >= B_HEADS * B_DK, B_DK ** -0.5, 1.0).astype(F32)
    half = B_DK // 2
    rows = min(ROT_ROWS, src_ref.shape[0])
    for rb in range(0, src_ref.shape[0], rows):
        rs = slice(rb, rb + rows)
        cos, sin = c_ref[rs, :] * scale, s_ref[rs, :] * scale
        for o in range(cs, cs + width, B_DK):
            x1 = src_ref[rs, o:o + half]
            x2 = src_ref[rs, o + half:o + B_DK]
            z_ref[rs, o:o + half] = x1 * cos - x2 * sin
            z_ref[rs, o + half:o + B_DK] = x1 * sin + x2 * cos


def _proj_body(*refs, kind, tn, ni, n_steps, lagged, side):
    n_tab = N_ROT_TABLES[kind]
    refs = list(refs)
    h_ref, hs_ref, w_ref = refs[:3]
    del refs[:3]
    w_side_ref = refs.pop(0) if side else None
    tabs, tabs_s = refs[:n_tab], refs[n_tab:2 * n_tab]
    del refs[:2 * n_tab]
    z_ref, zs_ref = refs[:2]
    del refs[:2]
    if side:
        z_side_ref, zs_side_ref = refs[:2]
        del refs[:2]
    wb_ref = refs.pop(0)
    s = pl.program_id(0)
    s_mm = jnp.minimum(s, n_steps - 1)
    j, i = s_mm // ni, s_mm % ni
    if lagged:
        acc_even, acc_odd = refs
        j_prev = jnp.maximum(s - 1, 0) // ni
    else:
        j_prev = j
    if side:
        @pl.when(j == 0)
        def _():
            w_side = w_side_ref[...].astype(BF16)
            z_side_ref[...] = _nn(h_ref[...], w_side)

            @pl.when(i == 0)
            def _():
                zs_side_ref[...] = _nn(hs_ref[...], w_side)
    epilogue = {"a": _rot_a, "b": _rot_b, "c": None}[kind]

    def is_rot(jj):
        if kind == "a":
            seg = (jj * tn) // (A_HEADS * A_HEAD_DIM)
            return jnp.logical_and(seg < 3 * len(A_GROUPS), seg % 3 < 2)
        return (jj * tn) < 2 * B_HEADS * B_DK

    slab = min(tn, MXU_N)

    def finish(src_ref, cs, o_ref, tables, jj, rot):
        if rot:
            epilogue(src_ref, cs, slab, o_ref, tables, jj, tn)
        elif src_ref is not o_ref:
            rows = min(ROT_ROWS, src_ref.shape[0])
            for rb in range(0, src_ref.shape[0], rows):
                o_ref[rb:rb + rows, cs:cs + slab] = src_ref[rb:rb + rows, cs:cs + slab]

    if lagged:
        @pl.when(s == 0)
        def _():
            acc_odd[...] = jnp.zeros_like(acc_odd)

    @pl.when(jnp.logical_and(i == 0, s < n_steps))
    def _():
        wb_ref[...] = w_ref[...].astype(BF16)

        def sample(rot):
            for cs in range(0, tn, slab):
                zs_ref[:, cs:cs + slab] = _nn(hs_ref[...], wb_ref[:, cs:cs + slab])
                finish(zs_ref, cs, zs_ref, tabs_s, j, rot)

        if epilogue is None:
            sample(False)
        else:
            pl.when(is_rot(j))(lambda: sample(True))
            pl.when(jnp.logical_not(is_rot(j)))(lambda: sample(False))

    def step(acc_new, acc_old, rot):
        for cs in range(0, tn, slab):
            acc_new[:, cs:cs + slab] = _nn(h_ref[...], wb_ref[:, cs:cs + slab])
            finish(acc_old, cs, z_ref, tabs, j_prev, rot)

    if lagged:
        even = s % 2 == 0
        variants = ((even, acc_even, acc_odd), (jnp.logical_not(even), acc_odd, acc_even))
    else:
        variants = ((None, z_ref, z_ref),)
    for on, acc_new, acc_old in variants:
        if epilogue is None:
            plain = functools.partial(step, acc_new, acc_old, False)
            plain() if on is None else pl.when(on)(plain)
        else:
            rot_prev = is_rot(j_prev)
            both = lambda a, b: b if a is None else jnp.logical_and(a, b)
            pl.when(both(on, rot_prev))(functools.partial(step, acc_new, acc_old, True))
            pl.when(both(on, jnp.logical_not(rot_prev)))(functools.partial(step, acc_new, acc_old, False))


def _proj(h, hs, w3, slot, n_out, kind, tabs, tabs_s, tab_blocks, tm, tn, w_side=None):
    m, k = h.shape
    ms = hs.shape[0]
    ni = m // tm
    n_steps = (n_out // tn) * ni
    lagged = kind == "a"
    side = w_side is not None
    assert not (side and lagged)
    mm = lambda s: jnp.minimum(s, n_steps - 1)
    prev = (lambda s: jnp.maximum(s - 1, 0)) if lagged else (lambda s: s)
    const = lambda s: (0, 0)
    in_specs = [pl.BlockSpec((tm, k), lambda s: (mm(s) % ni, 0)),
                pl.BlockSpec((ms, k), const),
                pl.BlockSpec((None, k, tn), lambda s: (slot, 0, mm(s) // ni))]
    args = [h, hs, w3]
    if side:
        in_specs.append(pl.BlockSpec((k, LANES), const))
        args.append(w_side)
    in_specs += [pl.BlockSpec((tm, LANES), lambda s: ((prev(s) % ni) % tab_blocks, 0)) for _ in tabs]
    in_specs += [pl.BlockSpec((ms, LANES), const) for _ in tabs_s]
    out_specs = [pl.BlockSpec((tm, tn), lambda s: (prev(s) % ni, prev(s) // ni)),
                 pl.BlockSpec((ms, tn), lambda s: (0, mm(s) // ni))]
    out_shape = [jax.ShapeDtypeStruct((m, n_out), F32), jax.ShapeDtypeStruct((ms, n_out), F32)]
    if side:
        out_specs += [pl.BlockSpec((tm, LANES), lambda s: (jnp.minimum(s, ni - 1), 0)),
                      pl.BlockSpec((ms, LANES), const)]
        out_shape += [jax.ShapeDtypeStruct((m, LANES), F32), jax.ShapeDtypeStruct((ms, LANES), F32)]
    scratch = [pltpu.VMEM((k, tn), BF16)]
    if lagged:
        scratch += [pltpu.VMEM((tm, tn), F32), pltpu.VMEM((tm, tn), F32)]
    return pl.pallas_call(
        functools.partial(_proj_body, kind=kind, tn=tn, ni=ni, n_steps=n_steps, lagged=lagged, side=side),
        grid=(n_steps + 1 if lagged else n_steps,),
        in_specs=in_specs,
        out_specs=out_specs,
        out_shape=out_shape,
        scratch_shapes=scratch,
        compiler_params=_params(("arbitrary",)),
        name="proj_" + kind,
    )(*args, *tabs, *tabs_s)


def _cast_body(x_ref, o_ref):
    o_ref[...] = x_ref[...].astype(o_ref.dtype)


def _cast_bf16(w3):
    n, k, d = w3.shape
    rows = n * k
    tr = min(rows, CAST_ROWS)
    out = pl.pallas_call(
        _cast_body,
        grid=(rows // tr,),
        in_specs=[pl.BlockSpec((tr, d), lambda i: (i, 0))],
        out_specs=pl.BlockSpec((tr, d), lambda i: (i, 0)),
        out_shape=jax.ShapeDtypeStruct((rows, d), BF16),
        compiler_params=_params(("arbitrary",)),
        name="cast_bf16",
    )(w3.reshape(rows, d))
    return out.reshape(n, k, d)


def _outproj_body(y_ref, ys_ref, w_ref, x_ref, xs_ref, g_ref, *out_refs, cast_w, want_x):
    if cast_w:
        *out_refs, wb_ref = out_refs

        @pl.when(pl.program_id(0) == 0)
        def _():
            wb_ref[...] = w_ref[...].astype(BF16)
    else:
        wb_ref = w_ref
    if want_x:
        xo_ref, xso_ref, ho_ref, hso_ref = out_refs
    else:
        (ho_ref, hso_ref), xo_ref, xso_ref = out_refs, None, None

    def rows(y_in, x_in, x_out, h_out):
        xn = x_in[...] + _nn(y_in[...], wb_ref[...])
        if x_out is not None:
            x_out[...] = xn
        ms = jnp.mean(xn * xn, axis=-1, keepdims=True)
        h_out[...] = (xn * lax.rsqrt(ms + NORM_EPS) * g_ref[...]).astype(h_out.dtype)

    pl.when(pl.program_id(0) == 0)(lambda: rows(ys_ref, xs_ref, xso_ref, hso_ref))
    rows(y_ref, x_ref, xo_ref, ho_ref)


def _outproj(y, ys, w3, slot, x, xs, g, h_dtype, tm, want_x):
    m, kd = y.shape
    ms, d = xs.shape
    cast_w = w3.dtype == F32
    const = lambda i: (0, 0)
    row_spec = pl.BlockSpec((tm, d), lambda i: (i, 0))
    s_spec = pl.BlockSpec((ms, d), const)
    shape = lambda rows, dt: jax.ShapeDtypeStruct((rows, d), dt)
    outs = pl.pallas_call(
        functools.partial(_outproj_body, cast_w=cast_w, want_x=want_x),
        scratch_shapes=[pltpu.VMEM((kd, d), BF16)] if cast_w else [],
        grid=(m // tm,),
        in_specs=[pl.BlockSpec((tm, kd), lambda i: (i, 0)),
                  pl.BlockSpec((ms, kd), const),
                  pl.BlockSpec((None, kd, d), lambda i: (slot, 0, 0), pipeline_mode=pl.Buffered(1)),
                  row_spec, s_spec,
                  pl.BlockSpec((1, d), const)],
        out_specs=([row_spec, s_spec] if want_x else []) + [row_spec, s_spec],
        out_shape=([shape(m, F32), shape(ms, F32)] if want_x else []) + [shape(m, h_dtype), shape(ms, h_dtype)],
        compiler_params=_params(("arbitrary",)),
        name="outproj",
    )(y, ys, w3, x, xs, g.reshape(1, d))
    return tuple(outs) if want_x else (None, None, *outs)


LOG2_E = 1.4426950408889634


def _rows(start, size, stride):
    return pl.ds(start, size) if stride == 1 else pl.ds(start, size, stride=stride)


def _attn_prompt_body(*refs, seq):
    qkv_refs, gate_ref, bias_first_ref, bias_band_ref, y_ref = refs[:9], *refs[9:13]
    o_s, d_s, m_s = refs[13:]
    q_scale = (A_HEAD_DIM ** -0.5) * LOG2_E
    ones = jnp.ones((A_BAND, A_HEAD_DIM), BF16)
    for g, (_, dil) in enumerate(A_GROUPS):
        q_ref, k_ref, v_ref = qkv_refs[3 * g:3 * g + 3]

        def load_kv(rows, k_ref=k_ref, v_ref=v_ref):
            return k_ref[rows, :].astype(BF16), jnp.concatenate([v_ref[rows, :].astype(BF16), ones], axis=1)

        def chain(r, n0, n1, g=g, dil=dil, q_ref=q_ref, load_kv=load_kv):
            block_rows = lambda n: _rows(r + dil * A_BAND * n, A_BAND, dil)
            k_prev, v_prev = load_kv(block_rows(n0 - 1)) if n0 > 0 else (None, None)
            for n in range(n0, n1):
                rows = block_rows(n)
                q = (q_ref[rows, :] * q_scale).astype(BF16)
                k_cur, v_cur = load_kv(rows)
                if n == 0:
                    keys, vals, bias_ref = k_cur, v_cur, bias_first_ref
                else:
                    keys = jnp.concatenate([k_prev, k_cur], axis=0)
                    vals = jnp.concatenate([v_prev, v_cur], axis=0)
                    bias_ref = bias_band_ref
                k_prev, v_prev = k_cur, v_cur
                s = _nt(q, keys)
                yield
                s = s + bias_ref[...]
                mx = jnp.max(s, axis=-1, keepdims=True)
                od = _nn(jnp.exp2(s - mx).astype(BF16), vals)
                yield
                o_s[g, rows, :] = od[:, :A_HEAD_DIM]
                d_s[g, rows, :] = od[:, A_HEAD_DIM:]
                m_s[g, rows, :] = jnp.broadcast_to(mx, (A_BAND, A_HEAD_DIM))

        nb = seq // dil // A_BAND
        if dil == 1:
            chains = [chain(0, 0, nb // 2), chain(0, nb // 2, nb)]
        else:
            chains = [chain(r, 0, nb) for r in range(dil)]
        _staggered(chains)
    for tile in range(seq // A_BAND):
        rs = slice(tile * A_BAND, (tile + 1) * A_BAND)
        m1, m2, m3 = m_s[0, rs, :], m_s[1, rs, :], m_s[2, rs, :]
        mx = jnp.maximum(jnp.maximum(m1, m2), m3)
        f1, f2, f3 = jnp.exp2(m1 - mx), jnp.exp2(m2 - mx), jnp.exp2(m3 - mx)
        num = f1 * o_s[0, rs, :] + f2 * o_s[1, rs, :] + f3 * o_s[2, rs, :]
        den = f1 * d_s[0, rs, :] + f2 * d_s[1, rs, :] + f3 * d_s[2, rs, :]
        y_ref[rs, :] = (num / den * _silu(gate_ref[rs, :])).astype(y_ref.dtype)


def _attn_prompt(z, batch, seq):
    n_groups = len(A_GROUPS)
    in_specs = []
    for c in range(3 * n_groups + 1):
        in_specs.append(pl.BlockSpec((seq, A_HEAD_DIM), lambda b, h, c=c: (b, c * A_HEADS + h)))
    row = jnp.arange(A_BAND)[:, None]
    col = jnp.arange(2 * A_BAND)[None, :]
    bias_first = jnp.where(col[:, :A_BAND] <= row, 0.0, -jnp.inf).astype(F32)
    bias_band = jnp.where((col >= row) & (col <= row + A_BAND), 0.0, -jnp.inf).astype(F32)
    in_specs.append(pl.BlockSpec((A_BAND, A_BAND), lambda b, h: (0, 0)))
    in_specs.append(pl.BlockSpec((A_BAND, 2 * A_BAND), lambda b, h: (0, 0)))
    return pl.pallas_call(
        functools.partial(_attn_prompt_body, seq=seq),
        grid=(batch, A_HEADS),
        in_specs=in_specs,
        out_specs=pl.BlockSpec((seq, A_HEAD_DIM), lambda b, h: (b, h)),
        out_shape=jax.ShapeDtypeStruct((batch * seq, A_HEADS * A_HEAD_DIM), BF16),
        scratch_shapes=[pltpu.VMEM((n_groups, seq, A_HEAD_DIM), F32)] * 3,
        compiler_params=_params(("arbitrary", "arbitrary")),
        name="attn_prompt",
    )(*([z] * (3 * n_groups + 1)), bias_first, bias_band)


def _attn_sample_body(z_ref, c1_ref, c2_ref, c3_ref, y_ref):
    t = pl.program_id(1)
    scale = A_HEAD_DIM ** -0.5
    m_idx = lax.broadcasted_iota(jnp.int32, (A_BAND, A_HEADS, 1), 0)
    n_idx = lax.broadcasted_iota(jnp.int32, (SAMPLE_PAD, A_HEADS, 1), 0)
    caches = (c1_ref, c2_ref, c3_ref)
    outs, lses = [], []
    for g, (_, dil) in enumerate(A_GROUPS):
        base = 3 * g * A_HEADS
        q = z_ref[0, t, base:base + A_HEADS, :]
        kc = caches[g][0, :, 0:A_HEADS, :]
        vc = caches[g][0, :, A_HEADS:2 * A_HEADS, :]
        kn = z_ref[0, :, base + A_HEADS:base + 2 * A_HEADS, :]
        vn = z_ref[0, :, base + 2 * A_HEADS:base + 3 * A_HEADS, :]
        s_c = jnp.sum(kc * q[None], axis=-1, keepdims=True) * scale
        s_n = jnp.sum(kn * q[None], axis=-1, keepdims=True) * scale
        if dil == 1:
            s_c = jnp.where(m_idx >= t, s_c, -jnp.inf)
            s_n = jnp.where(n_idx <= t, s_n, -jnp.inf)
        else:
            s_n = jnp.where(n_idx == t, s_n, -jnp.inf)
        mx = jnp.maximum(jnp.max(s_c, axis=0), jnp.max(s_n, axis=0))
        p_c = jnp.exp(s_c - mx[None])
        p_n = jnp.exp(s_n - mx[None])
        den = jnp.sum(p_c, axis=0) + jnp.sum(p_n, axis=0)
        num = jnp.sum(p_c * vc, axis=0) + jnp.sum(p_n * vn, axis=0)
        outs.append(num / den)
        lses.append(mx + jnp.log(den))
    lmax = jnp.maximum(jnp.maximum(lses[0], lses[1]), lses[2])
    es = [jnp.exp(l - lmax) for l in lses]
    tot = es[0] + es[1] + es[2]
    o = (es[0] / tot) * outs[0] + (es[1] / tot) * outs[1] + (es[2] / tot) * outs[2]
    gate = z_ref[0, t, 3 * len(A_GROUPS) * A_HEADS:(3 * len(A_GROUPS) + 1) * A_HEADS, :]
    y_ref[0, 0] = o * _silu(gate)


def _attn_sample(zs, caches, slot, batch, t_real):
    n_in = zs.shape[1]
    z5 = zs.reshape(batch, SAMPLE_PAD, n_in // A_HEAD_DIM, A_HEAD_DIM)
    in_specs = [pl.BlockSpec((1, SAMPLE_PAD, n_in // A_HEAD_DIM, A_HEAD_DIM), lambda b, t: (b, 0, 0, 0))]
    args = [z5]
    for g, (win, dil) in enumerate(A_GROUPS):
        n_a = caches[g].shape[0]
        cv = caches[g].reshape(n_a * batch, win // dil, dil * 2 * A_HEADS, A_HEAD_DIM)
        if dil == 1:
            idx = lambda b, t: (slot * batch + b, 0, 0, 0)
        else:
            idx = lambda b, t: (slot * batch + b, 0, t, 0)
        in_specs.append(pl.BlockSpec((1, A_BAND, 2 * A_HEADS, A_HEAD_DIM), idx))
        args.append(cv)
    y = pl.pallas_call(
        _attn_sample_body,
        grid=(batch, t_real),
        in_specs=in_specs,
        out_specs=pl.BlockSpec((1, 1, A_HEADS, A_HEAD_DIM), lambda b, t: (b, t, 0, 0)),
        out_shape=jax.ShapeDtypeStruct((batch, t_real, A_HEADS, A_HEAD_DIM), F32),
        compiler_params=_params(("arbitrary", "arbitrary")),
        name="attn_sample",
    )(*args)
    return y.reshape(batch, t_real, A_HEADS * A_HEAD_DIM)


def _kv_rows_body(*refs, n_layers):
    out_ref = refs[-1]
    layer = pl.program_id(0)
    for l in range(n_layers):
        k_ref, v_ref = refs[2 * l], refs[2 * l + 1]

        @pl.when(layer == l)
        def _():
            sub = SUBLANES
            for base, src in ((0, k_ref), (A_HEADS, v_ref)):
                for c0 in range(0, A_HEADS, sub):
                    cols = [src[:, (c0 + c) * A_HEAD_DIM:(c0 + c + 1) * A_HEAD_DIM] for c in range(sub)]
                    out_ref[:, base + c0:base + c0 + sub, :] = jnp.swapaxes(jnp.stack(cols, axis=0), 0, 1)


def _kv_rows(zs, g, batch, seq, keep):
    n_layers = len(zs)
    width = A_HEADS * A_HEAD_DIM
    tt = min(keep, KV_ROWS)
    first = (seq - keep) // tt
    in_specs, args = [], []
    for l in range(n_layers):
        for c in (1, 2):
            def idx(s, b, i, l=l, c=c):
                return (jnp.where(s == l, b * (seq // tt) + first + i, 0), 3 * g + c)
            in_specs.append(pl.BlockSpec((tt, width), idx))
            args.append(zs[l])
    out = pl.pallas_call(
        functools.partial(_kv_rows_body, n_layers=n_layers),
        grid=(n_layers, batch, keep // tt),
        in_specs=in_specs,
        out_specs=pl.BlockSpec((None, None, tt, 2 * A_HEADS, A_HEAD_DIM), lambda s, b, i: (s, b, i, 0, 0)),
        out_shape=jax.ShapeDtypeStruct((n_layers, batch, keep, 2 * A_HEADS, A_HEAD_DIM), F32),
        compiler_params=_params(("arbitrary", "arbitrary", "arbitrary")),
        name="kv_rows_g%d" % g,
    )(*args)
    return out.reshape(n_layers, batch, keep, 2, A_HEADS, A_HEAD_DIM)


def _tn(a, b):
    return lax.dot_general(a, b, (((0,), (0,)), ((), ())), preferred_element_type=F32)


def _retention_body(*refs, has_state, chunk, inner):
    if has_state:
        q_ref, k_ref, v_ref, gate_ref, dec_ref, xi_ref, kd_ref, sd_ref, gn_ref, s0_ref, y_ref, st_ref = refs
    else:
        q_ref, k_ref, v_ref, gate_ref, dec_ref, xi_ref, kd_ref, sd_ref, gn_ref, y_ref, st_ref = refs
    c = pl.program_id(1)

    @pl.when(c == 0)
    def _():
        if has_state:
            st_ref[...] = s0_ref[...]
        else:
            st_ref[...] = jnp.zeros_like(st_ref)

    def head(h, rs):
        ks = slice(h * B_DK, (h + 1) * B_DK)
        vs = slice(h * B_DV, (h + 1) * B_DV)
        qb = q_ref[rs, ks].astype(BF16)
        kf = k_ref[rs, ks]
        vb = v_ref[rs, vs].astype(BF16)
        state = st_ref[0, h]
        scores = _nt(qb, kf.astype(BF16))
        cross = _nn(qb, state.astype(BF16))
        k_dec = (kf * kd_ref[h]).astype(BF16)
        st_ref[0, h] = sd_ref[h] * state + _tn(k_dec, vb)
        yield
        o = _nn((scores * dec_ref[h]).astype(BF16), vb) + cross * xi_ref[h]
        yield
        mu = jnp.mean(o, axis=-1, keepdims=True)
        var = jnp.mean(jnp.square(o - mu), axis=-1, keepdims=True)
        yv = (o - mu) * lax.rsqrt(var + NORM_EPS) * gn_ref[h]
        y_ref[rs, vs] = (yv * _silu(gate_ref[rs, vs])).astype(y_ref.dtype)

    for cc in range(inner):
        _round_robin([head(h, slice(cc * chunk, (cc + 1) * chunk)) for h in range(B_HEADS)])


def _retention(z, batch, chunk, t_real, gn, state0):
    rows = z.shape[0]
    nc = rows // (batch * chunk)
    lg = jnp.log(1.0 - 2.0 ** (-5.0 - jnp.arange(B_HEADS, dtype=F32)))[:, None, None]
    idx = jnp.arange(chunk, dtype=F32)
    diff = idx[:, None] - idx[None, :]
    dec = jnp.where(diff >= 0, jnp.exp(lg * jnp.maximum(diff, 0.0)), 0.0)
    xi = jnp.exp(lg * (idx[None, :, None] + 1.0))
    live = idx[None, :, None] < t_real
    kd = jnp.where(live, jnp.exp(lg * jnp.where(live, t_real - 1.0 - idx[None, :, None], 0.0)), 0.0)
    sd = jnp.exp(lg * t_real)
    n_k, n_v = B_HEADS * B_DK, B_HEADS * B_DV
    const = lambda b, c: (0, 0, 0)
    inner = CHUNKS_PER_STEP if nc % CHUNKS_PER_STEP == 0 else 1
    ns, step_rows = nc // inner, inner * chunk
    in_specs = [pl.BlockSpec((step_rows, n_k), lambda b, c: (b * ns + c, 0)),
                pl.BlockSpec((step_rows, n_k), lambda b, c: (b * ns + c, 1)),
                pl.BlockSpec((step_rows, n_v), lambda b, c: (b * ns + c, 2 * n_k // n_v)),
                pl.BlockSpec((step_rows, n_v), lambda b, c: (b * ns + c, 2 * n_k // n_v + 1)),
                pl.BlockSpec((B_HEADS, chunk, chunk), const),
                pl.BlockSpec((B_HEADS, chunk, 1), const),
                pl.BlockSpec((B_HEADS, chunk, 1), const),
                pl.BlockSpec((B_HEADS, 1, 1), const),
                pl.BlockSpec((B_HEADS, 1, B_DV), const)]
    args = [z, z, z, z, dec, xi, kd, sd, gn.reshape(B_HEADS, 1, B_DV)]
    state_spec = pl.BlockSpec((1, B_HEADS, B_DK, B_DV), lambda b, c: (b, 0, 0, 0))
    if state0 is not None:
        in_specs.append(state_spec)
        args.append(state0)
    return pl.pallas_call(
        functools.partial(_retention_body, has_state=state0 is not None, chunk=chunk, inner=inner),
        grid=(batch, ns),
        in_specs=in_specs,
        out_specs=[pl.BlockSpec((step_rows, n_v), lambda b, c: (b * ns + c, 0)), state_spec],
        out_shape=[jax.ShapeDtypeStruct((rows, n_v), BF16),
                   jax.ShapeDtypeStruct((batch, B_HEADS, B_DK, B_DV), F32)],
        compiler_params=_params(("arbitrary", "arbitrary")),
        name="retention",
    )(*args)


def _log_sigmoid(x):
    return jnp.minimum(x, 0.0) - jnp.log(1.0 + jnp.exp(-jnp.abs(x)))


def _gla_head(q, k, v, lr, wg, bg, state, consts, chunk, sub, t_sub):
    nsub = chunk // sub
    tri, pair_live, pair_col, pick_t = consts
    pre = _nn_hi(lr, wg)
    yield
    log_a = _log_sigmoid(pre + bg) * (LOG2_E / C_TAU)
    bl = _nn_exact_lhs(tri, log_a)
    yield

    def block_rows(vals):
        return jnp.concatenate([jnp.broadcast_to(x, (sub, C_DK)) for x in vals], axis=0)

    tot = [bl[j * sub + t_sub - 1:j * sub + t_sub, :] for j in range(nsub)]
    beta = [jnp.zeros((1, C_DK), F32)]
    for j in range(nsub):
        beta.append(beta[-1] + tot[j])
    qs = q * (C_DK ** -0.5)
    q_in = qs * jnp.exp2(bl)
    k_out = k * jnp.exp2(block_rows(tot) - bl)
    if t_sub < sub:
        k_out = jnp.where(lax.broadcasted_iota(jnp.int32, (chunk, 1), 0) % sub < t_sub, k_out, 0.0)
    vb = v.astype(BF16)

    o = _nn((q_in * block_rows([jnp.exp2(x) for x in beta[:nsub]])).astype(BF16), state.astype(BF16))
    yield

    def rep_rows(x):
        return jnp.concatenate([jnp.broadcast_to(x[t:t + 1, :], (sub, C_DK)) for t in range(sub)], axis=0)

    def tile_rows(x):
        return jnp.concatenate([x] * sub, axis=0)

    a_rows = []
    for i in range(nsub):
        r = slice(i * sub, (i + 1) * sub)
        pair = rep_rows(qs[r]) * tile_rows(k[r]) * jnp.exp2(rep_rows(bl[r]) - tile_rows(bl[r]))
        att = jnp.where(pair_live, jnp.sum(pair, axis=1, keepdims=True), 0.0)
        placed = jnp.where(pair_col == i * sub, att, 0.0).astype(BF16)
        a_i = _nn(pick_t, placed)
        if i > 0:
            between = [jnp.broadcast_to(jnp.exp2(beta[i] - beta[j + 1]), (sub, C_DK)) for j in range(i)]
            between.append(jnp.zeros(((nsub - i) * sub, C_DK), F32))
            a_i += _nt(q_in[r].astype(BF16), (k_out * jnp.concatenate(between, axis=0)).astype(BF16))
        a_rows.append(a_i)
        if i % 2 == 1:
            yield
    a = a_rows[0] if nsub == 1 else jnp.concatenate(a_rows, axis=0)
    o += _nn(a.astype(BF16), vb)
    yield
    k_end = k_out * block_rows([jnp.exp2(beta[nsub] - beta[j + 1]) for j in range(nsub)])
    d_col = jnp.broadcast_to(jnp.exp2(beta[nsub]), (SUBLANES, C_DK)).T[:, 0:1]
    return o, d_col * state + _tn(k_end.astype(BF16), vb)


def _gla_consts(chunk, sub):
    ri = lax.broadcasted_iota(jnp.int32, (chunk, chunk), 0)
    ci = lax.broadcasted_iota(jnp.int32, (chunk, chunk), 1)
    tri = jnp.logical_and(ri // sub == ci // sub, ci <= ri).astype(BF16)
    pj = lax.broadcasted_iota(jnp.int32, (sub * sub, 1), 0)
    pair_live = pj // sub >= pj % sub
    pc = lax.broadcasted_iota(jnp.int32, (sub * sub, chunk), 1)
    ps = lax.broadcasted_iota(jnp.int32, (sub * sub, chunk), 0) % sub
    et = lax.broadcasted_iota(jnp.int32, (sub, sub * sub), 0)
    ej = lax.broadcasted_iota(jnp.int32, (sub, sub * sub), 1)
    pick_t = (ej // sub == et).astype(BF16)
    return tri, pair_live, pc - ps, pick_t


def _gla_body(*refs, has_state, chunk, inner, sub, t_sub):
    consts = _gla_consts(chunk, sub)
    if has_state:
        q_ref, k_ref, v_ref, gate_ref, lr_ref, wg_ref, bg_ref, gn_ref, s0_ref, y_ref, st_ref = refs
    else:
        q_ref, k_ref, v_ref, gate_ref, lr_ref, wg_ref, bg_ref, gn_ref, y_ref, st_ref = refs
    c = pl.program_id(1)

    @pl.when(c == 0)
    def _():
        if has_state:
            st_ref[...] = s0_ref[...]
        else:
            st_ref[...] = jnp.zeros_like(st_ref)

    for cc in range(inner):
        rs = slice(cc * chunk, (cc + 1) * chunk)
        lr = lr_ref[rs, 0:C_RANK]
        heads = []
        for h in range(C_HEADS):
            ks = slice(h * C_DK, (h + 1) * C_DK)
            vs = slice(h * C_DV, (h + 1) * C_DV)
            heads.append(_gla_head(q_ref[rs, ks], k_ref[rs, ks], v_ref[rs, vs], lr, wg_ref[:, ks], bg_ref[:, ks],
                                   st_ref[0, h], consts, chunk, sub, t_sub))
        for h, (o, new_state) in enumerate(_round_robin(heads)):
            vs = slice(h * C_DV, (h + 1) * C_DV)
            st_ref[0, h] = new_state
            yv = o * lax.rsqrt(jnp.mean(o * o, axis=-1, keepdims=True) + NORM_EPS) * gn_ref[:, vs]
            y_ref[rs, vs] = (yv * _silu(gate_ref[rs, vs])).astype(y_ref.dtype)


def _gla(z, z_lr, batch, chunk, t_real, wg3, bg3, gn3, slot, state0):
    rows = z.shape[0]
    nc = rows // (batch * chunk)
    sub, t_sub = (C_SUB, C_SUB) if t_real == chunk else (chunk, t_real)
    n_k, n_v = C_HEADS * C_DK, C_HEADS * C_DV
    n_slots = wg3.shape[0]
    inner = CHUNKS_PER_STEP if nc % CHUNKS_PER_STEP == 0 else 1
    ns, step_rows = nc // inner, inner * chunk
    in_specs = [pl.BlockSpec((step_rows, n_k), lambda b, c: (b * ns + c, 0)),
                pl.BlockSpec((step_rows, n_k), lambda b, c: (b * ns + c, 1)),
                pl.BlockSpec((step_rows, n_v), lambda b, c: (b * ns + c, 2 * n_k // n_v)),
                pl.BlockSpec((step_rows, n_v), lambda b, c: (b * ns + c, 2 * n_k // n_v + 1)),
                pl.BlockSpec((step_rows, LANES), lambda b, c: (b * ns + c, 0)),
                pl.BlockSpec((None, C_RANK, n_k), lambda b, c: (slot, 0, 0)),
                pl.BlockSpec((None, 1, n_k), lambda b, c: (slot, 0, 0)),
                pl.BlockSpec((None, 1, n_v), lambda b, c: (slot, 0, 0))]
    args = [z, z, z, z, z_lr, wg3, bg3.reshape(n_slots, 1, n_k), gn3.reshape(n_slots, 1, n_v)]
    state_spec = pl.BlockSpec((1, C_HEADS, C_DK, C_DV), lambda b, c: (b, 0, 0, 0))
    if state0 is not None:
        in_specs.append(state_spec)
        args.append(state0)
    return pl.pallas_call(
        functools.partial(_gla_body, has_state=state0 is not None, chunk=chunk, inner=inner, sub=sub, t_sub=t_sub),
        grid=(batch, ns),
        in_specs=in_specs,
        out_specs=[pl.BlockSpec((step_rows, n_v), lambda b, c: (b * ns + c, 0)), state_spec],
        out_shape=[jax.ShapeDtypeStruct((rows, n_v), BF16),
                   jax.ShapeDtypeStruct((batch, C_HEADS, C_DK, C_DV), F32)],
        compiler_params=_params(("arbitrary", "arbitrary")),
        name="gla",
    )(*args)


def _rot_tables_a(pos):
    half = A_ROT // 2
    inv_freq = A_ROPE_THETA ** (-jnp.arange(half, dtype=F32) / half)
    ang = pos.astype(F32)[:, None] * inv_freq[None, :]
    cos, sin = jnp.cos(ang), jnp.sin(ang)
    n = pos.shape[0]
    rest = A_HEAD_DIM - A_ROT
    c = jnp.concatenate([cos, cos, jnp.ones((n, rest), F32)], axis=1)
    s = jnp.concatenate([-sin, sin, jnp.zeros((n, rest), F32)], axis=1)
    return c, s


def _rot_tables_b(pos):
    half = B_DK // 2
    inv_freq = B_ROPE_THETA ** (-jnp.arange(half, dtype=F32) / half)
    ang = pos.astype(F32)[:, None] * inv_freq[None, :]
    return jnp.cos(ang), jnp.sin(ang)


def kernel(x_prompt, x_sample, cache_a_kv1, cache_a_kv2, cache_a_kv3, state_b, state_c, norm_g, final_g,
           w_in_a, w_out_a, w_in_b, gn_b, w_out_b, w_in_c, w_gate2_c, b_gate_c, gn_c, w_out_c):
    bp, seq, d = x_prompt.shape
    bs, t_real, _ = x_sample.shape
    caches_a = (cache_a_kv1, cache_a_kv2, cache_a_kv3)
    width = A_HEADS * A_HEAD_DIM

    xp = x_prompt.reshape(bp * seq, d)
    xs = jnp.pad(x_sample, ((0, 0), (0, SAMPLE_PAD - t_real), (0, 0))).reshape(bs * SAMPLE_PAD, d)
    ms = xs.shape[0]
    tm_p = min(PROJ_TM, seq)

    pos_p = jnp.arange(seq)
    pos_s = jnp.tile(PAST_LEN + jnp.arange(SAMPLE_PAD), bs)
    tabs_a_p, tabs_a_s = _rot_tables_a(pos_p), _rot_tables_a(pos_s)
    tabs_b_p, tabs_b_s = _rot_tables_b(pos_p), _rot_tables_b(pos_s)

    hp = _rmsnorm(xp, norm_g[0], BF16)
    hs = _rmsnorm(xs, norm_g[0], BF16)
    def maybe_cast(w3):
        return _cast_bf16(w3) if w3.shape[1] * w3.shape[2] * 6 > OUTPROJ_CAST_LIMIT else w3

    w_out_a, w_out_b, w_out_c = maybe_cast(w_out_a), maybe_cast(w_out_b), maybe_cast(w_out_c)

    a_z_p, a_rows_s = [], []
    b_states_p, b_states_s, c_states_p, c_states_s = [], [], [], []
    n_layers = len(LAYER_KINDS)
    for i in range(n_layers):
        kind, slot = LAYER_KINDS[i], LAYER_SLOTS[i]
        last = i == n_layers - 1
        g_next = final_g if last else norm_g[i + 1]
        h_dtype = F32 if last else BF16
        if kind == 0:
            n_in = w_in_a.shape[2]
            zp, zs = _proj(hp, hs, w_in_a, slot, n_in, "a", tabs_a_p, tabs_a_s, seq // tm_p, tm_p, PROJ_TN)
            yp = _attn_prompt(zp, bp, seq)
            ys = _attn_sample(zs, caches_a, slot, bs, t_real)
            ys = jnp.pad(ys, ((0, 0), (0, SAMPLE_PAD - t_real), (0, 0))).reshape(ms, width).astype(BF16)
            zs3 = zs.reshape(bs, SAMPLE_PAD, n_in)
            rows_s = []
            for g in range(len(A_GROUPS)):
                lo = (3 * g + 1) * width
                rows_s.append(zs3[:, :t_real, lo:lo + 2 * width].reshape(bs, t_real, 2, A_HEADS, A_HEAD_DIM))
            a_z_p.append(zp)
            a_rows_s.append(rows_s)
            w_out = w_out_a
        elif kind == 1:
            n_in = w_in_b.shape[2]
            zp, zs = _proj(hp, hs, w_in_b, slot, n_in, "b", tabs_b_p, tabs_b_s, seq // tm_p, tm_p, PROJ_TN)
            yp, st_p = _retention(zp, bp, min(B_CHUNK, seq), min(B_CHUNK, seq), gn_b[slot], None)
            ys, st_s = _retention(zs, bs, SAMPLE_PAD, t_real, gn_b[slot], state_b[slot])
            b_states_p.append(st_p)
            b_states_s.append(st_s)
            w_out = w_out_b
        else:
            n_main = 2 * C_HEADS * C_DK + 2 * C_HEADS * C_DV
            w_lr = jnp.pad(w_in_c[slot][:, n_main:], ((0, 0), (0, LANES - C_RANK)))
            zp, zs, zp_lr, zs_lr = _proj(hp, hs, w_in_c, slot, n_main, "c", (), (), 1, tm_p, PROJ_TN, w_side=w_lr)
            yp, st_p = _gla(zp, zp_lr, bp, min(C_CHUNK, seq), min(C_CHUNK, seq),
                            w_gate2_c, b_gate_c, gn_c, slot, None)
            ys, st_s = _gla(zs, zs_lr, bs, SAMPLE_PAD, t_real, w_gate2_c, b_gate_c, gn_c, slot, state_c[slot])
            c_states_p.append(st_p)
            c_states_s.append(st_s)
            w_out = w_out_c
        xp, xs, hp, hs = _outproj(yp, ys, w_out, slot, xp, xs, g_next, h_dtype, OUTPROJ_TM, not last)

    y_prompt = hp.reshape(bp, seq, d)
    y_sample = hs.reshape(bs, SAMPLE_PAD, d)[:, :t_real]
    stack = lambda rows, g: jnp.stack([r[g] for r in rows])
    kv_p = [_kv_rows(a_z_p, g, bp, seq, min(win, seq)) for g, (win, _) in enumerate(A_GROUPS)]
    return (y_prompt, y_sample,
            kv_p[0], kv_p[1], kv_p[2],
            jnp.stack(b_states_p), jnp.stack(c_states_p),
            stack(a_rows_s, 0), stack(a_rows_s, 1), stack(a_rows_s, 2),
            jnp.stack(b_states_s), jnp.stack(c_states_s))
```

```python
import functools

import jax
import jax.numpy as jnp
from jax import lax
from jax.experimental import pallas as pl
from jax.experimental.pallas import tpu as pltpu

F32 = jnp.float32
BF16 = jnp.bfloat16

PAST_LEN = 16384
NORM_EPS = 1e-6
A_GROUPS = ((128, 1), (512, 4), (2048, 16))
A_HEADS = 16
A_HEAD_DIM = 128
A_ROT = A_HEAD_DIM // 4
A_ROPE_THETA = 500000.0
A_BAND = 128
B_HEADS = 8
B_DK = 256
B_DV = 512
B_CHUNK = 128
B_ROPE_THETA = 10000.0
C_HEADS = 4
C_DK = 256
C_DV = 512
C_RANK = 16
C_TAU = 16.0
C_CHUNK = 64
C_SUB = 8
LAYER_KINDS = (0, 1, 2, 0)
LAYER_SLOTS = (0, 0, 0, 1)

SAMPLE_PAD = 16
LANES = 128
SUBLANES = 8
MXU_N = 256
OUTPROJ_CAST_LIMIT = 32 * 1024 * 1024
VMEM_LIMIT = 56 * 1024 * 1024

PROJ_TM = 1024
PROJ_TN = 1024
OUTPROJ_TM = 512
ROWWISE_TM = 512
CAST_ROWS = 1024
KV_ROWS = 512
B_CHUNKS_PER_STEP = 2
C_CHUNKS_PER_STEP = 1


def _params(sem, vmem=VMEM_LIMIT):
    return pltpu.CompilerParams(dimension_semantics=sem, vmem_limit_bytes=vmem)


def _silu(g):
    return g * (1.0 / (1.0 + jnp.exp(-g)))


def _nt(a, b):
    return lax.dot_general(a, b, (((1,), (1,)), ((), ())), preferred_element_type=F32)


def _nn(a, b):
    return jnp.dot(a, b, preferred_element_type=F32)


def _split(x):
    hi = x.astype(BF16)
    lo = (x - hi.astype(F32)).astype(BF16)
    return hi, lo


def _split3(x):
    hi = x.astype(BF16)
    r = x - hi.astype(F32)
    mid = r.astype(BF16)
    lo = (r - mid.astype(F32)).astype(BF16)
    return hi, mid, lo


def _nn_exact_lhs(sel, x):
    hi, mid, lo = _split3(x)
    return _nn(sel, hi) + _nn(sel, mid) + _nn(sel, lo)


def _round_robin(gens):
    gens = list(gens)
    results = [None] * len(gens)
    live = list(range(len(gens)))
    while live:
        for n in list(live):
            try:
                next(gens[n])
            except StopIteration as stop:
                results[n] = stop.value
                live.remove(n)
    return results


def _staggered(gens):
    pending = list(gens)
    active = []
    while pending or active:
        if pending:
            active.append(pending.pop(0))
        for g in list(active):
            try:
                next(g)
            except StopIteration:
                active.remove(g)


def _nn_hi(a, b):
    ah, al = _split(a)
    bh, bl = _split(b)
    return _nn(ah, bh) + _nn(ah, bl) + _nn(al, bh)


def _rmsnorm_body(x_ref, g_ref, o_ref):
    x = x_ref[...]
    ms = jnp.mean(x * x, axis=-1, keepdims=True)
    o_ref[...] = (x * lax.rsqrt(ms + NORM_EPS) * g_ref[...]).astype(o_ref.dtype)


def _rmsnorm(x, g, out_dtype):
    m, d = x.shape
    tm = min(m, ROWWISE_TM)
    return pl.pallas_call(
        _rmsnorm_body,
        grid=(m // tm,),
        in_specs=[pl.BlockSpec((tm, d), lambda i: (i, 0)),
                  pl.BlockSpec((1, d), lambda i: (0, 0))],
        out_specs=pl.BlockSpec((tm, d), lambda i: (i, 0)),
        out_shape=jax.ShapeDtypeStruct((m, d), out_dtype),
        compiler_params=_params(("arbitrary",)),
        name="rmsnorm",
    )(x, g.reshape(1, d))


N_ROT_TABLES = {"a": 2, "b": 2, "c": 0}


ROT_ROWS = 128


def _rot_a(src_ref, cs, width, z_ref, tabs, j, tn):
    c_ref, s_ref = tabs
    half = A_ROT // 2
    rows = min(ROT_ROWS, src_ref.shape[0])
    first_half = lax.broadcasted_iota(jnp.int32, (rows, A_HEAD_DIM), 1) < half
    for rb in range(0, src_ref.shape[0], rows):
        rs = slice(rb, rb + rows)
        for o in range(cs, cs + width, A_HEAD_DIM):
            x = src_ref[rs, o:o + A_HEAD_DIM]
            x_hi = pltpu.roll(x, A_HEAD_DIM - half, 1)
            x_lo = pltpu.roll(x, half, 1)
            z_ref[rs, o:o + A_HEAD_DIM] = x * c_ref[rs, :] + jnp.where(first_half, x_hi, x_lo) * s_ref[rs, :]


def _rot_b(src_ref, cs, width, z_ref, tabs, j, tn):
    c_ref, s_ref = tabs
    scale = jnp.where(j * tn >= B_HEADS * B_DK, B_DK ** -0.5, 1.0).astype(F32)
    half = B_DK // 2
    rows = min(ROT_ROWS, src_ref.shape[0])
    for rb in range(0, src_ref.shape[0], rows):
        rs = slice(rb, rb + rows)
        cos, sin = c_ref[rs, :] * scale, s_ref[rs, :] * scale
        for o in range(cs, cs + width, B_DK):
            x1 = src_ref[rs, o:o + half]
            x2 = src_ref[rs, o + half:o + B_DK]
            z_ref[rs, o:o + half] = x1 * cos - x2 * sin
            z_ref[rs, o + half:o + B_DK] = x1 * sin + x2 * cos


def _proj_body(*refs, kind, tn, ni, n_steps, lagged, side):
    n_tab = N_ROT_TABLES[kind]
    refs = list(refs)
    h_ref, hs_ref, w_ref = refs[:3]
    del refs[:3]
    w_side_ref = refs.pop(0) if side else None
    tabs, tabs_s = refs[:n_tab], refs[n_tab:2 * n_tab]
    del refs[:2 * n_tab]
    z_ref, zs_ref = refs[:2]
    del refs[:2]
    if side:
        z_side_ref, zs_side_ref = refs[:2]
        del refs[:2]
    wb_ref = refs.pop(0)
    s = pl.program_id(0)
    s_mm = jnp.minimum(s, n_steps - 1)
    j, i = s_mm // ni, s_mm % ni
    if lagged:
        acc_even, acc_odd = refs
        j_prev = jnp.maximum(s - 1, 0) // ni
    else:
        j_prev = j
    if side:
        @pl.when(j == 0)
        def _():
            w_side = w_side_ref[...].astype(BF16)
            z_side_ref[...] = _nn(h_ref[...], w_side)

            @pl.when(i == 0)
            def _():
                zs_side_ref[...] = _nn(hs_ref[...], w_side)
    epilogue = {"a": _rot_a, "b": _rot_b, "c": None}[kind]

    def is_rot(jj):
        if kind == "a":
            seg = (jj * tn) // (A_HEADS * A_HEAD_DIM)
            return jnp.logical_and(seg < 3 * len(A_GROUPS), seg % 3 < 2)
        return (jj * tn) < 2 * B_HEADS * B_DK

    slab = min(tn, MXU_N)

    def finish(src_ref, cs, o_ref, tables, jj, rot):
        if rot:
            epilogue(src_ref, cs, slab, o_ref, tables, jj, tn)
        elif src_ref is not o_ref:
            rows = min(ROT_ROWS, src_ref.shape[0])
            for rb in range(0, src_ref.shape[0], rows):
                o_ref[rb:rb + rows, cs:cs + slab] = src_ref[rb:rb + rows, cs:cs + slab]

    if lagged:
        @pl.when(s == 0)
        def _():
            acc_odd[...] = jnp.zeros_like(acc_odd)

    @pl.when(jnp.logical_and(i == 0, s < n_steps))
    def _():
        wb_ref[...] = w_ref[...].astype(BF16)

        def sample(rot):
            for cs in range(0, tn, slab):
                zs_ref[:, cs:cs + slab] = _nn(hs_ref[...], wb_ref[:, cs:cs + slab])
                finish(zs_ref, cs, zs_ref, tabs_s, j, rot)

        if epilogue is None:
            sample(False)
        else:
            pl.when(is_rot(j))(lambda: sample(True))
            pl.when(jnp.logical_not(is_rot(j)))(lambda: sample(False))

    def step(acc_new, acc_old, rot):
        for cs in range(0, tn, slab):
            acc_new[:, cs:cs + slab] = _nn(h_ref[...], wb_ref[:, cs:cs + slab])
            finish(acc_old, cs, z_ref, tabs, j_prev, rot)

    if lagged:
        even = s % 2 == 0
        variants = ((even, acc_even, acc_odd), (jnp.logical_not(even), acc_odd, acc_even))
    else:
        variants = ((None, z_ref, z_ref),)
    for on, acc_new, acc_old in variants:
        if epilogue is None:
            plain = functools.partial(step, acc_new, acc_old, False)
            plain() if on is None else pl.when(on)(plain)
        else:
            rot_prev = is_rot(j_prev)
            both = lambda a, b: b if a is None else jnp.logical_and(a, b)
            pl.when(both(on, rot_prev))(functools.partial(step, acc_new, acc_old, True))
            pl.when(both(on, jnp.logical_not(rot_prev)))(functools.partial(step, acc_new, acc_old, False))


def _proj(h, hs, w3, slot, n_out, kind, tabs, tabs_s, tab_blocks, tm, tn, w_side=None):
    m, k = h.shape
    ms = hs.shape[0]
    ni = m // tm
    n_steps = (n_out // tn) * ni
    lagged = kind == "a"
    side = w_side is not None
    assert not (side and lagged)
    mm = lambda s: jnp.minimum(s, n_steps - 1)
    prev = (lambda s: jnp.maximum(s - 1, 0)) if lagged else (lambda s: s)
    const = lambda s: (0, 0)
    in_specs = [pl.BlockSpec((tm, k), lambda s: (mm(s) % ni, 0)),
                pl.BlockSpec((ms, k), const),
                pl.BlockSpec((None, k, tn), lambda s: (slot, 0, mm(s) // ni))]
    args = [h, hs, w3]
    if side:
        in_specs.append(pl.BlockSpec((k, LANES), const))
        args.append(w_side)
    in_specs += [pl.BlockSpec((tm, LANES), lambda s: ((prev(s) % ni) % tab_blocks, 0)) for _ in tabs]
    in_specs += [pl.BlockSpec((ms, LANES), const) for _ in tabs_s]
    out_specs = [pl.BlockSpec((tm, tn), lambda s: (prev(s) % ni, prev(s) // ni)),
                 pl.BlockSpec((ms, tn), lambda s: (0, mm(s) // ni))]
    out_shape = [jax.ShapeDtypeStruct((m, n_out), F32), jax.ShapeDtypeStruct((ms, n_out), F32)]
    if side:
        out_specs += [pl.BlockSpec((tm, LANES), lambda s: (jnp.minimum(s, ni - 1), 0)),
                      pl.BlockSpec((ms, LANES), const)]
        out_shape += [jax.ShapeDtypeStruct((m, LANES), F32), jax.ShapeDtypeStruct((ms, LANES), F32)]
    scratch = [pltpu.VMEM((k, tn), BF16)]
    if lagged:
        scratch += [pltpu.VMEM((tm, tn), F32), pltpu.VMEM((tm, tn), F32)]
    return pl.pallas_call(
        functools.partial(_proj_body, kind=kind, tn=tn, ni=ni, n_steps=n_steps, lagged=lagged, side=side),
        grid=(n_steps + 1 if lagged else n_steps,),
        in_specs=in_specs,
        out_specs=out_specs,
        out_shape=out_shape,
        scratch_shapes=scratch,
        compiler_params=_params(("arbitrary",)),
        name="proj_" + kind,
    )(*args, *tabs, *tabs_s)


def _cast_body(x_ref, o_ref):
    o_ref[...] = x_ref[...].astype(o_ref.dtype)


def _cast_bf16(w3):
    n, k, d = w3.shape
    rows = n * k
    tr = min(rows, CAST_ROWS)
    out = pl.pallas_call(
        _cast_body,
        grid=(rows // tr,),
        in_specs=[pl.BlockSpec((tr, d), lambda i: (i, 0))],
        out_specs=pl.BlockSpec((tr, d), lambda i: (i, 0)),
        out_shape=jax.ShapeDtypeStruct((rows, d), BF16),
        compiler_params=_params(("arbitrary",)),
        name="cast_bf16",
    )(w3.reshape(rows, d))
    return out.reshape(n, k, d)


def _outproj_body(y_ref, ys_ref, w_ref, x_ref, xs_ref, g_ref, *out_refs, cast_w, want_x):
    if cast_w:
        *out_refs, wb_ref = out_refs

        @pl.when(pl.program_id(0) == 0)
        def _():
            wb_ref[...] = w_ref[...].astype(BF16)
    else:
        wb_ref = w_ref
    if want_x:
        xo_ref, xso_ref, ho_ref, hso_ref = out_refs
    else:
        (ho_ref, hso_ref), xo_ref, xso_ref = out_refs, None, None

    def rows(y_in, x_in, x_out, h_out):
        xn = x_in[...] + _nn(y_in[...], wb_ref[...])
        if x_out is not None:
            x_out[...] = xn
        ms = jnp.mean(xn * xn, axis=-1, keepdims=True)
        h_out[...] = (xn * lax.rsqrt(ms + NORM_EPS) * g_ref[...]).astype(h_out.dtype)

    pl.when(pl.program_id(0) == 0)(lambda: rows(ys_ref, xs_ref, xso_ref, hso_ref))
    rows(y_ref, x_ref, xo_ref, ho_ref)


def _outproj(y, ys, w3, slot, x, xs, g, h_dtype, tm, want_x):
    m, kd = y.shape
    ms, d = xs.shape
    cast_w = w3.dtype == F32
    const = lambda i: (0, 0)
    row_spec = pl.BlockSpec((tm, d), lambda i: (i, 0))
    s_spec = pl.BlockSpec((ms, d), const)
    shape = lambda rows, dt: jax.ShapeDtypeStruct((rows, d), dt)
    outs = pl.pallas_call(
        functools.partial(_outproj_body, cast_w=cast_w, want_x=want_x),
        scratch_shapes=[pltpu.VMEM((kd, d), BF16)] if cast_w else [],
        grid=(m // tm,),
        in_specs=[pl.BlockSpec((tm, kd), lambda i: (i, 0)),
                  pl.BlockSpec((ms, kd), const),
                  pl.BlockSpec((None, kd, d), lambda i: (slot, 0, 0), pipeline_mode=pl.Buffered(1)),
                  row_spec, s_spec,
                  pl.BlockSpec((1, d), const)],
        out_specs=([row_spec, s_spec] if want_x else []) + [row_spec, s_spec],
        out_shape=([shape(m, F32), shape(ms, F32)] if want_x else []) + [shape(m, h_dtype), shape(ms, h_dtype)],
        compiler_params=_params(("arbitrary",)),
        name="outproj",
    )(y, ys, w3, x, xs, g.reshape(1, d))
    return tuple(outs) if want_x else (None, None, *outs)


LOG2_E = 1.4426950408889634


def _rows(start, size, stride):
    return pl.ds(start, size) if stride == 1 else pl.ds(start, size, stride=stride)


def _attn_prompt_body(*refs, seq):
    qkv_refs, gate_ref, bias_first_ref, bias_band_ref, y_ref = refs[:9], *refs[9:13]
    o_s, d_s, m_s = refs[13:]
    q_scale = (A_HEAD_DIM ** -0.5) * LOG2_E
    ones = jnp.ones((A_BAND, A_HEAD_DIM), BF16)
    for g, (_, dil) in enumerate(A_GROUPS):
        q_ref, k_ref, v_ref = qkv_refs[3 * g:3 * g + 3]

        def load_kv(rows, k_ref=k_ref, v_ref=v_ref):
            return k_ref[rows, :].astype(BF16), jnp.concatenate([v_ref[rows, :].astype(BF16), ones], axis=1)

        def chain(r, n0, n1, g=g, dil=dil, q_ref=q_ref, load_kv=load_kv):
            block_rows = lambda n: _rows(r + dil * A_BAND * n, A_BAND, dil)
            k_prev, v_prev = load_kv(block_rows(n0 - 1)) if n0 > 0 else (None, None)
            for n in range(n0, n1):
                rows = block_rows(n)
                q = (q_ref[rows, :] * q_scale).astype(BF16)
                k_cur, v_cur = load_kv(rows)
                if n == 0:
                    keys, vals, bias_ref = k_cur, v_cur, bias_first_ref
                else:
                    keys = jnp.concatenate([k_prev, k_cur], axis=0)
                    vals = jnp.concatenate([v_prev, v_cur], axis=0)
                    bias_ref = bias_band_ref
                k_prev, v_prev = k_cur, v_cur
                s = _nt(q, keys)
                yield
                s = s + bias_ref[...]
                mx = jnp.max(s, axis=-1, keepdims=True)
                od = _nn(jnp.exp2(s - mx).astype(BF16), vals)
                yield
                o_s[g, rows, :] = od[:, :A_HEAD_DIM]
                d_s[g, rows, :] = od[:, A_HEAD_DIM:]
                m_s[g, rows, :] = jnp.broadcast_to(mx, (A_BAND, A_HEAD_DIM))

        nb = seq // dil // A_BAND
        if dil == 1:
            chains = [chain(0, 0, nb // 2), chain(0, nb // 2, nb)]
        else:
            chains = [chain(r, 0, nb) for r in range(dil)]
        _staggered(chains)
    for tile in range(seq // A_BAND):
        rs = slice(tile * A_BAND, (tile + 1) * A_BAND)
        m1, m2, m3 = m_s[0, rs, :], m_s[1, rs, :], m_s[2, rs, :]
        mx = jnp.maximum(jnp.maximum(m1, m2), m3)
        f1, f2, f3 = jnp.exp2(m1 - mx), jnp.exp2(m2 - mx), jnp.exp2(m3 - mx)
        num = f1 * o_s[0, rs, :] + f2 * o_s[1, rs, :] + f3 * o_s[2, rs, :]
        den = f1 * d_s[0, rs, :] + f2 * d_s[1, rs, :] + f3 * d_s[2, rs, :]
        y_ref[rs, :] = (num / den * _silu(gate_ref[rs, :])).astype(y_ref.dtype)


def _attn_prompt(z, batch, seq):
    n_groups = len(A_GROUPS)
    in_specs = []
    for c in range(3 * n_groups + 1):
        in_specs.append(pl.BlockSpec((seq, A_HEAD_DIM), lambda b, h, c=c: (b, c * A_HEADS + h)))
    row = jnp.arange(A_BAND)[:, None]
    col = jnp.arange(2 * A_BAND)[None, :]
    bias_first = jnp.where(col[:, :A_BAND] <= row, 0.0, -jnp.inf).astype(F32)
    bias_band = jnp.where((col >= row) & (col <= row + A_BAND), 0.0, -jnp.inf).astype(F32)
    in_specs.append(pl.BlockSpec((A_BAND, A_BAND), lambda b, h: (0, 0)))
    in_specs.append(pl.BlockSpec((A_BAND, 2 * A_BAND), lambda b, h: (0, 0)))
    return pl.pallas_call(
        functools.partial(_attn_prompt_body, seq=seq),
        grid=(batch, A_HEADS),
        in_specs=in_specs,
        out_specs=pl.BlockSpec((seq, A_HEAD_DIM), lambda b, h: (b, h)),
        out_shape=jax.ShapeDtypeStruct((batch * seq, A_HEADS * A_HEAD_DIM), BF16),
        scratch_shapes=[pltpu.VMEM((n_groups, seq, A_HEAD_DIM), F32)] * 3,
        compiler_params=_params(("arbitrary", "arbitrary")),
        name="attn_prompt",
    )(*([z] * (3 * n_groups + 1)), bias_first, bias_band)


def _attn_sample_body(z_ref, c1_ref, c2_ref, c3_ref, y_ref):
    t = pl.program_id(1)
    scale = A_HEAD_DIM ** -0.5
    m_idx = lax.broadcasted_iota(jnp.int32, (A_BAND, A_HEADS, 1), 0)
    n_idx = lax.broadcasted_iota(jnp.int32, (SAMPLE_PAD, A_HEADS, 1), 0)
    caches = (c1_ref, c2_ref, c3_ref)
    outs, lses = [], []
    for g, (_, dil) in enumerate(A_GROUPS):
        base = 3 * g * A_HEADS
        q = z_ref[0, t, base:base + A_HEADS, :]
        kc = caches[g][0, :, 0:A_HEADS, :]
        vc = caches[g][0, :, A_HEADS:2 * A_HEADS, :]
        kn = z_ref[0, :, base + A_HEADS:base + 2 * A_HEADS, :]
        vn = z_ref[0, :, base + 2 * A_HEADS:base + 3 * A_HEADS, :]
        s_c = jnp.sum(kc * q[None], axis=-1, keepdims=True) * scale
        s_n = jnp.sum(kn * q[None], axis=-1, keepdims=True) * scale
        if dil == 1:
            s_c = jnp.where(m_idx >= t, s_c, -jnp.inf)
            s_n = jnp.where(n_idx <= t, s_n, -jnp.inf)
        else:
            s_n = jnp.where(n_idx == t, s_n, -jnp.inf)
        mx = jnp.maximum(jnp.max(s_c, axis=0), jnp.max(s_n, axis=0))
        p_c = jnp.exp(s_c - mx[None])
        p_n = jnp.exp(s_n - mx[None])
        den = jnp.sum(p_c, axis=0) + jnp.sum(p_n, axis=0)
        num = jnp.sum(p_c * vc, axis=0) + jnp.sum(p_n * vn, axis=0)
        outs.append(num / den)
        lses.append(mx + jnp.log(den))
    lmax = jnp.maximum(jnp.maximum(lses[0], lses[1]), lses[2])
    es = [jnp.exp(l - lmax) for l in lses]
    tot = es[0] + es[1] + es[2]
    o = (es[0] / tot) * outs[0] + (es[1] / tot) * outs[1] + (es[2] / tot) * outs[2]
    gate = z_ref[0, t, 3 * len(A_GROUPS) * A_HEADS:(3 * len(A_GROUPS) + 1) * A_HEADS, :]
    y_ref[0, 0] = o * _silu(gate)


def _attn_sample(zs, caches, slot, batch, t_real):
    n_in = zs.shape[1]
    z5 = zs.reshape(batch, SAMPLE_PAD, n_in // A_HEAD_DIM, A_HEAD_DIM)
    in_specs = [pl.BlockSpec((1, SAMPLE_PAD, n_in // A_HEAD_DIM, A_HEAD_DIM), lambda b, t: (b, 0, 0, 0))]
    args = [z5]
    for g, (win, dil) in enumerate(A_GROUPS):
        n_a = caches[g].shape[0]
        cv = caches[g].reshape(n_a * batch, win // dil, dil * 2 * A_HEADS, A_HEAD_DIM)
        if dil == 1:
            idx = lambda b, t: (slot * batch + b, 0, 0, 0)
        else:
            idx = lambda b, t: (slot * batch + b, 0, t, 0)
        in_specs.append(pl.BlockSpec((1, A_BAND, 2 * A_HEADS, A_HEAD_DIM), idx))
        args.append(cv)
    y = pl.pallas_call(
        _attn_sample_body,
        grid=(batch, t_real),
        in_specs=in_specs,
        out_specs=pl.BlockSpec((1, 1, A_HEADS, A_HEAD_DIM), lambda b, t: (b, t, 0, 0)),
        out_shape=jax.ShapeDtypeStruct((batch, t_real, A_HEADS, A_HEAD_DIM), F32),
        compiler_params=_params(("arbitrary", "arbitrary")),
        name="attn_sample",
    )(*args)
    return y.reshape(batch, t_real, A_HEADS * A_HEAD_DIM)


def _kv_rows_body(*refs, n_layers):
    out_ref = refs[-1]
    layer = pl.program_id(0)
    for l in range(n_layers):
        k_ref, v_ref = refs[2 * l], refs[2 * l + 1]

        @pl.when(layer == l)
        def _():
            sub = SUBLANES
            for base, src in ((0, k_ref), (A_HEADS, v_ref)):
                for c0 in range(0, A_HEADS, sub):
                    cols = [src[:, (c0 + c) * A_HEAD_DIM:(c0 + c + 1) * A_HEAD_DIM] for c in range(sub)]
                    out_ref[:, base + c0:base + c0 + sub, :] = jnp.swapaxes(jnp.stack(cols, axis=0), 0, 1)


def _kv_rows(zs, g, batch, seq, keep):
    n_layers = len(zs)
    width = A_HEADS * A_HEAD_DIM
    tt = min(keep, KV_ROWS)
    first = (seq - keep) // tt
    in_specs, args = [], []
    for l in range(n_layers):
        for c in (1, 2):
            def idx(s, b, i, l=l, c=c):
                return (jnp.where(s == l, b * (seq // tt) + first + i, 0), 3 * g + c)
            in_specs.append(pl.BlockSpec((tt, width), idx))
            args.append(zs[l])
    out = pl.pallas_call(
        functools.partial(_kv_rows_body, n_layers=n_layers),
        grid=(n_layers, batch, keep // tt),
        in_specs=in_specs,
        out_specs=pl.BlockSpec((None, None, tt, 2 * A_HEADS, A_HEAD_DIM), lambda s, b, i: (s, b, i, 0, 0)),
        out_shape=jax.ShapeDtypeStruct((n_layers, batch, keep, 2 * A_HEADS, A_HEAD_DIM), F32),
        compiler_params=_params(("arbitrary", "arbitrary", "arbitrary")),
        name="kv_rows_g%d" % g,
    )(*args)
    return out.reshape(n_layers, batch, keep, 2, A_HEADS, A_HEAD_DIM)


def _tn(a, b):
    return lax.dot_general(a, b, (((0,), (0,)), ((), ())), preferred_element_type=F32)


def _retention_body(*refs, has_state, chunk, inner):
    if has_state:
        q_ref, k_ref, v_ref, gate_ref, dec_ref, xi_ref, kd_ref, sd_ref, gn_ref, s0_ref, y_ref, st_ref = refs
    else:
        q_ref, k_ref, v_ref, gate_ref, dec_ref, xi_ref, kd_ref, sd_ref, gn_ref, y_ref, st_ref = refs
    c = pl.program_id(1)

    @pl.when(c == 0)
    def _():
        if has_state:
            st_ref[...] = s0_ref[...]
        else:
            st_ref[...] = jnp.zeros_like(st_ref)

    def head(h, rs):
        ks = slice(h * B_DK, (h + 1) * B_DK)
        vs = slice(h * B_DV, (h + 1) * B_DV)
        qb = q_ref[rs, ks].astype(BF16)
        kf = k_ref[rs, ks]
        vb = v_ref[rs, vs].astype(BF16)
        state = st_ref[0, h]
        scores = _nt(qb, kf.astype(BF16))
        cross = _nn(qb, state.astype(BF16))
        k_dec = (kf * kd_ref[h]).astype(BF16)
        st_ref[0, h] = sd_ref[h] * state + _tn(k_dec, vb)
        yield
        o = _nn((scores * dec_ref[h]).astype(BF16), vb) + cross * xi_ref[h]
        yield
        mu = jnp.mean(o, axis=-1, keepdims=True)
        var = jnp.mean(jnp.square(o - mu), axis=-1, keepdims=True)
        yv = (o - mu) * lax.rsqrt(var + NORM_EPS) * gn_ref[h]
        y_ref[rs, vs] = (yv * _silu(gate_ref[rs, vs])).astype(y_ref.dtype)

    for cc in range(inner):
        _round_robin([head(h, slice(cc * chunk, (cc + 1) * chunk)) for h in range(B_HEADS)])


def _retention(z, batch, chunk, t_real, gn, state0):
    rows = z.shape[0]
    nc = rows // (batch * chunk)
    lg = jnp.log(1.0 - 2.0 ** (-5.0 - jnp.arange(B_HEADS, dtype=F32)))[:, None, None]
    idx = jnp.arange(chunk, dtype=F32)
    diff = idx[:, None] - idx[None, :]
    dec = jnp.where(diff >= 0, jnp.exp(lg * jnp.maximum(diff, 0.0)), 0.0)
    xi = jnp.exp(lg * (idx[None, :, None] + 1.0))
    live = idx[None, :, None] < t_real
    kd = jnp.where(live, jnp.exp(lg * jnp.where(live, t_real - 1.0 - idx[None, :, None], 0.0)), 0.0)
    sd = jnp.exp(lg * t_real)
    n_k, n_v = B_HEADS * B_DK, B_HEADS * B_DV
    const = lambda b, c: (0, 0, 0)
    inner = B_CHUNKS_PER_STEP if nc % B_CHUNKS_PER_STEP == 0 else 1
    ns, step_rows = nc // inner, inner * chunk
    in_specs = [pl.BlockSpec((step_rows, n_k), lambda b, c: (b * ns + c, 0)),
                pl.BlockSpec((step_rows, n_k), lambda b, c: (b * ns + c, 1)),
                pl.BlockSpec((step_rows, n_v), lambda b, c: (b * ns + c, 2 * n_k // n_v)),
                pl.BlockSpec((step_rows, n_v), lambda b, c: (b * ns + c, 2 * n_k // n_v + 1)),
                pl.BlockSpec((B_HEADS, chunk, chunk), const),
                pl.BlockSpec((B_HEADS, chunk, 1), const),
                pl.BlockSpec((B_HEADS, chunk, 1), const),
                pl.BlockSpec((B_HEADS, 1, 1), const),
                pl.BlockSpec((B_HEADS, 1, B_DV), const)]
    args = [z, z, z, z, dec, xi, kd, sd, gn.reshape(B_HEADS, 1, B_DV)]
    state_spec = pl.BlockSpec((1, B_HEADS, B_DK, B_DV), lambda b, c: (b, 0, 0, 0))
    if state0 is not None:
        in_specs.append(state_spec)
        args.append(state0)
    return pl.pallas_call(
        functools.partial(_retention_body, has_state=state0 is not None, chunk=chunk, inner=inner),
        grid=(batch, ns),
        in_specs=in_specs,
        out_specs=[pl.BlockSpec((step_rows, n_v), lambda b, c: (b * ns + c, 0)), state_spec],
        out_shape=[jax.ShapeDtypeStruct((rows, n_v), BF16),
                   jax.ShapeDtypeStruct((batch, B_HEADS, B_DK, B_DV), F32)],
        compiler_params=_params(("arbitrary", "arbitrary")),
        name="retention",
    )(*args)


def _log_sigmoid(x):
    return jnp.minimum(x, 0.0) - jnp.log(1.0 + jnp.exp(-jnp.abs(x)))


def _gla_head(q, k, v, lr, wg, bg, state, consts, chunk, sub, t_sub):
    nsub = chunk // sub
    tri, pair_live, pair_col, pick_t = consts
    pre = _nn_hi(lr, wg)
    yield
    log_a = _log_sigmoid(pre + bg) * (LOG2_E / C_TAU)
    bl = _nn_exact_lhs(tri, log_a)
    yield

    def block_rows(vals):
        return jnp.concatenate([jnp.broadcast_to(x, (sub, C_DK)) for x in vals], axis=0)

    tot = [bl[j * sub + t_sub - 1:j * sub + t_sub, :] for j in range(nsub)]
    beta = [jnp.zeros((1, C_DK), F32)]
    for j in range(nsub):
        beta.append(beta[-1] + tot[j])
    qs = q * (C_DK ** -0.5)
    q_in = qs * jnp.exp2(bl)
    k_out = k * jnp.exp2(block_rows(tot) - bl)
    if t_sub < sub:
        k_out = jnp.where(lax.broadcasted_iota(jnp.int32, (chunk, 1), 0) % sub < t_sub, k_out, 0.0)
    vb = v.astype(BF16)

    o = _nn((q_in * block_rows([jnp.exp2(x) for x in beta[:nsub]])).astype(BF16), state.astype(BF16))
    yield

    def rep_rows(x):
        return jnp.concatenate([jnp.broadcast_to(x[t:t + 1, :], (sub, C_DK)) for t in range(sub)], axis=0)

    def tile_rows(x):
        return jnp.concatenate([x] * sub, axis=0)

    a_rows = []
    for i in range(nsub):
        r = slice(i * sub, (i + 1) * sub)
        pair = rep_rows(qs[r]) * tile_rows(k[r]) * jnp.exp2(rep_rows(bl[r]) - tile_rows(bl[r]))
        att = jnp.where(pair_live, jnp.sum(pair, axis=1, keepdims=True), 0.0)
        placed = jnp.where(pair_col == i * sub, att, 0.0).astype(BF16)
        a_i = _nn(pick_t, placed)
        if i > 0:
            between = [jnp.broadcast_to(jnp.exp2(beta[i] - beta[j + 1]), (sub, C_DK)) for j in range(i)]
            between.append(jnp.zeros(((nsub - i) * sub, C_DK), F32))
            a_i += _nt(q_in[r].astype(BF16), (k_out * jnp.concatenate(between, axis=0)).astype(BF16))
        a_rows.append(a_i)
        if i % 2 == 1:
            yield
    a = a_rows[0] if nsub == 1 else jnp.concatenate(a_rows, axis=0)
    o += _nn(a.astype(BF16), vb)
    yield
    k_end = k_out * block_rows([jnp.exp2(beta[nsub] - beta[j + 1]) for j in range(nsub)])
    d_col = jnp.broadcast_to(jnp.exp2(beta[nsub]), (SUBLANES, C_DK)).T[:, 0:1]
    return o, d_col * state + _tn(k_end.astype(BF16), vb)


def _gla_consts(chunk, sub):
    ri = lax.broadcasted_iota(jnp.int32, (chunk, chunk), 0)
    ci = lax.broadcasted_iota(jnp.int32, (chunk, chunk), 1)
    tri = jnp.logical_and(ri // sub == ci // sub, ci <= ri).astype(BF16)
    pj = lax.broadcasted_iota(jnp.int32, (sub * sub, 1), 0)
    pair_live = pj // sub >= pj % sub
    pc = lax.broadcasted_iota(jnp.int32, (sub * sub, chunk), 1)
    ps = lax.broadcasted_iota(jnp.int32, (sub * sub, chunk), 0) % sub
    et = lax.broadcasted_iota(jnp.int32, (sub, sub * sub), 0)
    ej = lax.broadcasted_iota(jnp.int32, (sub, sub * sub), 1)
    pick_t = (ej // sub == et).astype(BF16)
    return tri, pair_live, pc - ps, pick_t


def _gla_body(*refs, has_state, chunk, inner, sub, t_sub):
    consts = _gla_consts(chunk, sub)
    if has_state:
        q_ref, k_ref, v_ref, gate_ref, lr_ref, wg_ref, bg_ref, gn_ref, s0_ref, y_ref, st_ref = refs
    else:
        q_ref, k_ref, v_ref, gate_ref, lr_ref, wg_ref, bg_ref, gn_ref, y_ref, st_ref = refs
    c = pl.program_id(1)

    @pl.when(c == 0)
    def _():
        if has_state:
            st_ref[...] = s0_ref[...]
        else:
            st_ref[...] = jnp.zeros_like(st_ref)

    for cc in range(inner):
        rs = slice(cc * chunk, (cc + 1) * chunk)
        lr = lr_ref[rs, 0:C_RANK]
        heads = []
        for h in range(C_HEADS):
            ks = slice(h * C_DK, (h + 1) * C_DK)
            vs = slice(h * C_DV, (h + 1) * C_DV)
            heads.append(_gla_head(q_ref[rs, ks], k_ref[rs, ks], v_ref[rs, vs], lr, wg_ref[:, ks], bg_ref[:, ks],
                                   st_ref[0, h], consts, chunk, sub, t_sub))
        for h, (o, new_state) in enumerate(_round_robin(heads)):
            vs = slice(h * C_DV, (h + 1) * C_DV)
            st_ref[0, h] = new_state
            yv = o * lax.rsqrt(jnp.mean(o * o, axis=-1, keepdims=True) + NORM_EPS) * gn_ref[:, vs]
            y_ref[rs, vs] = (yv * _silu(gate_ref[rs, vs])).astype(y_ref.dtype)


def _gla(z, z_lr, batch, chunk, t_real, wg3, bg3, gn3, slot, state0):
    rows = z.shape[0]
    nc = rows // (batch * chunk)
    sub, t_sub = (C_SUB, C_SUB) if t_real == chunk else (chunk, t_real)
    n_k, n_v = C_HEADS * C_DK, C_HEADS * C_DV
    n_slots = wg3.shape[0]
    inner = C_CHUNKS_PER_STEP if nc % C_CHUNKS_PER_STEP == 0 else 1
    ns, step_rows = nc // inner, inner * chunk
    in_specs = [pl.BlockSpec((step_rows, n_k), lambda b, c: (b * ns + c, 0)),
                pl.BlockSpec((step_rows, n_k), lambda b, c: (b * ns + c, 1)),
                pl.BlockSpec((step_rows, n_v), lambda b, c: (b * ns + c, 2 * n_k // n_v)),
                pl.BlockSpec((step_rows, n_v), lambda b, c: (b * ns + c, 2 * n_k // n_v + 1)),
                pl.BlockSpec((step_rows, LANES), lambda b, c: (b * ns + c, 0)),
                pl.BlockSpec((None, C_RANK, n_k), lambda b, c: (slot, 0, 0)),
                pl.BlockSpec((None, 1, n_k), lambda b, c: (slot, 0, 0)),
                pl.BlockSpec((None, 1, n_v), lambda b, c: (slot, 0, 0))]
    args = [z, z, z, z, z_lr, wg3, bg3.reshape(n_slots, 1, n_k), gn3.reshape(n_slots, 1, n_v)]
    state_spec = pl.BlockSpec((1, C_HEADS, C_DK, C_DV), lambda b, c: (b, 0, 0, 0))
    if state0 is not None:
        in_specs.append(state_spec)
        args.append(state0)
    return pl.pallas_call(
        functools.partial(_gla_body, has_state=state0 is not None, chunk=chunk, inner=inner, sub=sub, t_sub=t_sub),
        grid=(batch, ns),
        in_specs=in_specs,
        out_specs=[pl.BlockSpec((step_rows, n_v), lambda b, c: (b * ns + c, 0)), state_spec],
        out_shape=[jax.ShapeDtypeStruct((rows, n_v), BF16),
                   jax.ShapeDtypeStruct((batch, C_HEADS, C_DK, C_DV), F32)],
        compiler_params=_params(("arbitrary", "arbitrary")),
        name="gla",
    )(*args)


def _rot_tables_a(pos):
    half = A_ROT // 2
    inv_freq = A_ROPE_THETA ** (-jnp.arange(half, dtype=F32) / half)
    ang = pos.astype(F32)[:, None] * inv_freq[None, :]
    cos, sin = jnp.cos(ang), jnp.sin(ang)
    n = pos.shape[0]
    rest = A_HEAD_DIM - A_ROT
    c = jnp.concatenate([cos, cos, jnp.ones((n, rest), F32)], axis=1)
    s = jnp.concatenate([-sin, sin, jnp.zeros((n, rest), F32)], axis=1)
    return c, s


def _rot_tables_b(pos):
    half = B_DK // 2
    inv_freq = B_ROPE_THETA ** (-jnp.arange(half, dtype=F32) / half)
    ang = pos.astype(F32)[:, None] * inv_freq[None, :]
    return jnp.cos(ang), jnp.sin(ang)


def kernel(x_prompt, x_sample, cache_a_kv1, cache_a_kv2, cache_a_kv3, state_b, state_c, norm_g, final_g,
           w_in_a, w_out_a, w_in_b, gn_b, w_out_b, w_in_c, w_gate2_c, b_gate_c, gn_c, w_out_c):
    bp, seq, d = x_prompt.shape
    bs, t_real, _ = x_sample.shape
    caches_a = (cache_a_kv1, cache_a_kv2, cache_a_kv3)
    width = A_HEADS * A_HEAD_DIM

    xp = x_prompt.reshape(bp * seq, d)
    xs = jnp.pad(x_sample, ((0, 0), (0, SAMPLE_PAD - t_real), (0, 0))).reshape(bs * SAMPLE_PAD, d)
    ms = xs.shape[0]
    tm_p = min(PROJ_TM, seq)

    pos_p = jnp.arange(seq)
    pos_s = jnp.tile(PAST_LEN + jnp.arange(SAMPLE_PAD), bs)
    tabs_a_p, tabs_a_s = _rot_tables_a(pos_p), _rot_tables_a(pos_s)
    tabs_b_p, tabs_b_s = _rot_tables_b(pos_p), _rot_tables_b(pos_s)

    hp = _rmsnorm(xp, norm_g[0], BF16)
    hs = _rmsnorm(xs, norm_g[0], BF16)
    def maybe_cast(w3):
        return _cast_bf16(w3) if w3.shape[1] * w3.shape[2] * 6 > OUTPROJ_CAST_LIMIT else w3

    w_out_a, w_out_b, w_out_c = maybe_cast(w_out_a), maybe_cast(w_out_b), maybe_cast(w_out_c)

    a_z_p, a_rows_s = [], []
    b_states_p, b_states_s, c_states_p, c_states_s = [], [], [], []
    n_layers = len(LAYER_KINDS)
    for i in range(n_layers):
        kind, slot = LAYER_KINDS[i], LAYER_SLOTS[i]
        last = i == n_layers - 1
        g_next = final_g if last else norm_g[i + 1]
        h_dtype = F32 if last else BF16
        if kind == 0:
            n_in = w_in_a.shape[2]
            zp, zs = _proj(hp, hs, w_in_a, slot, n_in, "a", tabs_a_p, tabs_a_s, seq // tm_p, tm_p, PROJ_TN)
            yp = _attn_prompt(zp, bp, seq)
            ys = _attn_sample(zs, caches_a, slot, bs, t_real)
            ys = jnp.pad(ys, ((0, 0), (0, SAMPLE_PAD - t_real), (0, 0))).reshape(ms, width).astype(BF16)
            zs3 = zs.reshape(bs, SAMPLE_PAD, n_in)
            rows_s = []
            for g in range(len(A_GROUPS)):
                lo = (3 * g + 1) * width
                rows_s.append(zs3[:, :t_real, lo:lo + 2 * width].reshape(bs, t_real, 2, A_HEADS, A_HEAD_DIM))
            a_z_p.append(zp)
            a_rows_s.append(rows_s)
            w_out = w_out_a
        elif kind == 1:
            n_in = w_in_b.shape[2]
            zp, zs = _proj(hp, hs, w_in_b, slot, n_in, "b", tabs_b_p, tabs_b_s, seq // tm_p, tm_p, PROJ_TN)
            yp, st_p = _retention(zp, bp, min(B_CHUNK, seq), min(B_CHUNK, seq), gn_b[slot], None)
            ys, st_s = _retention(zs, bs, SAMPLE_PAD, t_real, gn_b[slot], state_b[slot])
            b_states_p.append(st_p)
            b_states_s.append(st_s)
            w_out = w_out_b
        else:
            n_main = 2 * C_HEADS * C_DK + 2 * C_HEADS * C_DV
            w_lr = jnp.pad(w_in_c[slot][:, n_main:], ((0, 0), (0, LANES - C_RANK)))
            zp, zs, zp_lr, zs_lr = _proj(hp, hs, w_in_c, slot, n_main, "c", (), (), 1, tm_p, PROJ_TN, w_side=w_lr)
            yp, st_p = _gla(zp, zp_lr, bp, min(C_CHUNK, seq), min(C_CHUNK, seq),
                            w_gate2_c, b_gate_c, gn_c, slot, None)
            ys, st_s = _gla(zs, zs_lr, bs, SAMPLE_PAD, t_real, w_gate2_c, b_gate_c, gn_c, slot, state_c[slot])
            c_states_p.append(st_p)
            c_states_s.append(st_s)
            w_out = w_out_c
        xp, xs, hp, hs = _outproj(yp, ys, w_out, slot, xp, xs, g_next, h_dtype, OUTPROJ_TM, not last)

    y_prompt = hp.reshape(bp, seq, d)
    y_sample = hs.reshape(bs, SAMPLE_PAD, d)[:, :t_real]
    stack = lambda rows, g: jnp.stack([r[g] for r in rows])
    kv_p = [_kv_rows(a_z_p, g, bp, seq, min(win, seq)) for g, (win, _) in enumerate(A_GROUPS)]
    return (y_prompt, y_sample,
            kv_p[0], kv_p[1], kv_p[2],
            jnp.stack(b_states_p), jnp.stack(c_states_p),
            stack(a_rows_s, 0), stack(a_rows_s, 1), stack(a_rows_s, 2),
            jnp.stack(b_states_s), jnp.stack(c_states_s))
```

```python
import functools

import jax
import jax.numpy as jnp
from jax import lax
from jax.experimental import pallas as pl
from jax.experimental.pallas import tpu as pltpu

F32 = jnp.float32
BF16 = jnp.bfloat16

PAST_LEN = 16384
NORM_EPS = 1e-6
A_GROUPS = ((128, 1), (512, 4), (2048, 16))
A_HEADS = 16
A_HEAD_DIM = 128
A_ROT = A_HEAD_DIM // 4
A_ROPE_THETA = 500000.0
A_BAND = 128
B_HEADS = 8
B_DK = 256
B_DV = 512
B_CHUNK = 256
B_ROPE_THETA = 10000.0
C_HEADS = 4
C_DK = 256
C_DV = 512
C_RANK = 16
C_TAU = 16.0
C_CHUNK = 64
C_SUB = 8
LAYER_KINDS = (0, 1, 2, 0)
LAYER_SLOTS = (0, 0, 0, 1)

SAMPLE_PAD = 16
LANES = 128
SUBLANES = 8
MXU_N = 256
OUTPROJ_CAST_LIMIT = 32 * 1024 * 1024
VMEM_LIMIT = 56 * 1024 * 1024

PROJ_TM = 1024
PROJ_TN = 1024
OUTPROJ_TM = 512
ROWWISE_TM = 512
CAST_ROWS = 1024
KV_ROWS = 512
B_CHUNKS_PER_STEP = 1
C_CHUNKS_PER_STEP = 1


def _params(sem, vmem=VMEM_LIMIT):
    return pltpu.CompilerParams(dimension_semantics=sem, vmem_limit_bytes=vmem)


def _silu(g):
    return g * (1.0 / (1.0 + jnp.exp(-g)))


def _nt(a, b):
    return lax.dot_general(a, b, (((1,), (1,)), ((), ())), preferred_element_type=F32)


def _nn(a, b):
    return jnp.dot(a, b, preferred_element_type=F32)


def _split(x):
    hi = x.astype(BF16)
    lo = (x - hi.astype(F32)).astype(BF16)
    return hi, lo


def _split3(x):
    hi = x.astype(BF16)
    r = x - hi.astype(F32)
    mid = r.astype(BF16)
    lo = (r - mid.astype(F32)).astype(BF16)
    return hi, mid, lo


def _nn_exact_lhs(sel, x):
    hi, mid, lo = _split3(x)
    return _nn(sel, hi) + _nn(sel, mid) + _nn(sel, lo)


def _round_robin(gens):
    gens = list(gens)
    results = [None] * len(gens)
    live = list(range(len(gens)))
    while live:
        for n in list(live):
            try:
                next(gens[n])
            except StopIteration as stop:
                results[n] = stop.value
                live.remove(n)
    return results


def _staggered(gens):
    pending = list(gens)
    active = []
    while pending or active:
        if pending:
            active.append(pending.pop(0))
        for g in list(active):
            try:
                next(g)
            except StopIteration:
                active.remove(g)


def _nn_hi(a, b):
    ah, al = _split(a)
    bh, bl = _split(b)
    return _nn(ah, bh) + _nn(ah, bl) + _nn(al, bh)


def _rmsnorm_body(x_ref, g_ref, o_ref):
    x = x_ref[...]
    ms = jnp.mean(x * x, axis=-1, keepdims=True)
    o_ref[...] = (x * lax.rsqrt(ms + NORM_EPS) * g_ref[...]).astype(o_ref.dtype)


def _rmsnorm(x, g, out_dtype):
    m, d = x.shape
    tm = min(m, ROWWISE_TM)
    return pl.pallas_call(
        _rmsnorm_body,
        grid=(m // tm,),
        in_specs=[pl.BlockSpec((tm, d), lambda i: (i, 0)),
                  pl.BlockSpec((1, d), lambda i: (0, 0))],
        out_specs=pl.BlockSpec((tm, d), lambda i: (i, 0)),
        out_shape=jax.ShapeDtypeStruct((m, d), out_dtype),
        compiler_params=_params(("arbitrary",)),
        name="rmsnorm",
    )(x, g.reshape(1, d))


N_ROT_TABLES = {"a": 2, "b": 2, "c": 0}


ROT_ROWS = 128


def _rot_a(src_ref, cs, width, z_ref, tabs, j, tn):
    c_ref, s_ref = tabs
    half = A_ROT // 2
    rows = min(ROT_ROWS, src_ref.shape[0])
    first_half = lax.broadcasted_iota(jnp.int32, (rows, A_HEAD_DIM), 1) < half
    for rb in range(0, src_ref.shape[0], rows):
        rs = slice(rb, rb + rows)
        for o in range(cs, cs + width, A_HEAD_DIM):
            x = src_ref[rs, o:o + A_HEAD_DIM]
            x_hi = pltpu.roll(x, A_HEAD_DIM - half, 1)
            x_lo = pltpu.roll(x, half, 1)
            z_ref[rs, o:o + A_HEAD_DIM] = x * c_ref[rs, :] + jnp.where(first_half, x_hi, x_lo) * s_ref[rs, :]


def _rot_b(src_ref, cs, width, z_ref, tabs, j, tn):
    c_ref, s_ref = tabs
    scale = jnp.where(j * tn >= B_HEADS * B_DK, B_DK ** -0.5, 1.0).astype(F32)
    half = B_DK // 2
    rows = min(ROT_ROWS, src_ref.shape[0])
    for rb in range(0, src_ref.shape[0], rows):
        rs = slice(rb, rb + rows)
        cos, sin = c_ref[rs, :] * scale, s_ref[rs, :] * scale
        for o in range(cs, cs + width, B_DK):
            x1 = src_ref[rs, o:o + half]
            x2 = src_ref[rs, o + half:o + B_DK]
            z_ref[rs, o:o + half] = x1 * cos - x2 * sin
            z_ref[rs, o + half:o + B_DK] = x1 * sin + x2 * cos


def _proj_body(*refs, kind, tn, ni, n_steps, lagged, side):
    n_tab = N_ROT_TABLES[kind]
    refs = list(refs)
    h_ref, hs_ref, w_ref = refs[:3]
    del refs[:3]
    w_side_ref = refs.pop(0) if side else None
    tabs, tabs_s = refs[:n_tab], refs[n_tab:2 * n_tab]
    del refs[:2 * n_tab]
    z_ref, zs_ref = refs[:2]
    del refs[:2]
    if side:
        z_side_ref, zs_side_ref = refs[:2]
        del refs[:2]
    wb_ref = refs.pop(0)
    s = pl.program_id(0)
    s_mm = jnp.minimum(s, n_steps - 1)
    j, i = s_mm // ni, s_mm % ni
    if lagged:
        acc_even, acc_odd = refs
        j_prev = jnp.maximum(s - 1, 0) // ni
    else:
        j_prev = j
    if side:
        @pl.when(j == 0)
        def _():
            w_side = w_side_ref[...].astype(BF16)
            z_side_ref[...] = _nn(h_ref[...], w_side)

            @pl.when(i == 0)
            def _():
                zs_side_ref[...] = _nn(hs_ref[...], w_side)
    epilogue = {"a": _rot_a, "b": _rot_b, "c": None}[kind]

    def is_rot(jj):
        if kind == "a":
            seg = (jj * tn) // (A_HEADS * A_HEAD_DIM)
            return jnp.logical_and(seg < 3 * len(A_GROUPS), seg % 3 < 2)
        return (jj * tn) < 2 * B_HEADS * B_DK

    slab = min(tn, MXU_N)

    def finish(src_ref, cs, o_ref, tables, jj, rot):
        if rot:
            epilogue(src_ref, cs, slab, o_ref, tables, jj, tn)
        elif src_ref is not o_ref:
            rows = min(ROT_ROWS, src_ref.shape[0])
            for rb in range(0, src_ref.shape[0], rows):
                o_ref[rb:rb + rows, cs:cs + slab] = src_ref[rb:rb + rows, cs:cs + slab]

    if lagged:
        @pl.when(s == 0)
        def _():
            acc_odd[...] = jnp.zeros_like(acc_odd)

    @pl.when(jnp.logical_and(i == 0, s < n_steps))
    def _():
        wb_ref[...] = w_ref[...].astype(BF16)

        def sample(rot):
            for cs in range(0, tn, slab):
                zs_ref[:, cs:cs + slab] = _nn(hs_ref[...], wb_ref[:, cs:cs + slab])
                finish(zs_ref, cs, zs_ref, tabs_s, j, rot)

        if epilogue is None:
            sample(False)
        else:
            pl.when(is_rot(j))(lambda: sample(True))
            pl.when(jnp.logical_not(is_rot(j)))(lambda: sample(False))

    def step(acc_new, acc_old, rot):
        for cs in range(0, tn, slab):
            acc_new[:, cs:cs + slab] = _nn(h_ref[...], wb_ref[:, cs:cs + slab])
            finish(acc_old, cs, z_ref, tabs, j_prev, rot)

    if lagged:
        even = s % 2 == 0
        variants = ((even, acc_even, acc_odd), (jnp.logical_not(even), acc_odd, acc_even))
    else:
        variants = ((None, z_ref, z_ref),)
    for on, acc_new, acc_old in variants:
        if epilogue is None:
            plain = functools.partial(step, acc_new, acc_old, False)
            plain() if on is None else pl.when(on)(plain)
        else:
            rot_prev = is_rot(j_prev)
            both = lambda a, b: b if a is None else jnp.logical_and(a, b)
            pl.when(both(on, rot_prev))(functools.partial(step, acc_new, acc_old, True))
            pl.when(both(on, jnp.logical_not(rot_prev)))(functools.partial(step, acc_new, acc_old, False))


def _proj(h, hs, w3, slot, n_out, kind, tabs, tabs_s, tab_blocks, tm, tn, w_side=None):
    m, k = h.shape
    ms = hs.shape[0]
    ni = m // tm
    n_steps = (n_out // tn) * ni
    lagged = kind == "a"
    side = w_side is not None
    assert not (side and lagged)
    mm = lambda s: jnp.minimum(s, n_steps - 1)
    prev = (lambda s: jnp.maximum(s - 1, 0)) if lagged else (lambda s: s)
    const = lambda s: (0, 0)
    in_specs = [pl.BlockSpec((tm, k), lambda s: (mm(s) % ni, 0)),
                pl.BlockSpec((ms, k), const),
                pl.BlockSpec((None, k, tn), lambda s: (slot, 0, mm(s) // ni))]
    args = [h, hs, w3]
    if side:
        in_specs.append(pl.BlockSpec((k, LANES), const))
        args.append(w_side)
    in_specs += [pl.BlockSpec((tm, LANES), lambda s: ((prev(s) % ni) % tab_blocks, 0)) for _ in tabs]
    in_specs += [pl.BlockSpec((ms, LANES), const) for _ in tabs_s]
    out_specs = [pl.BlockSpec((tm, tn), lambda s: (prev(s) % ni, prev(s) // ni)),
                 pl.BlockSpec((ms, tn), lambda s: (0, mm(s) // ni))]
    out_shape = [jax.ShapeDtypeStruct((m, n_out), F32), jax.ShapeDtypeStruct((ms, n_out), F32)]
    if side:
        out_specs += [pl.BlockSpec((tm, LANES), lambda s: (jnp.minimum(s, ni - 1), 0)),
                      pl.BlockSpec((ms, LANES), const)]
        out_shape += [jax.ShapeDtypeStruct((m, LANES), F32), jax.ShapeDtypeStruct((ms, LANES), F32)]
    scratch = [pltpu.VMEM((k, tn), BF16)]
    if lagged:
        scratch += [pltpu.VMEM((tm, tn), F32), pltpu.VMEM((tm, tn), F32)]
    return pl.pallas_call(
        functools.partial(_proj_body, kind=kind, tn=tn, ni=ni, n_steps=n_steps, lagged=lagged, side=side),
        grid=(n_steps + 1 if lagged else n_steps,),
        in_specs=in_specs,
        out_specs=out_specs,
        out_shape=out_shape,
        scratch_shapes=scratch,
        compiler_params=_params(("arbitrary",)),
        name="proj_" + kind,
    )(*args, *tabs, *tabs_s)


def _cast_body(x_ref, o_ref):
    o_ref[...] = x_ref[...].astype(o_ref.dtype)


def _cast_bf16(w3):
    n, k, d = w3.shape
    rows = n * k
    tr = min(rows, CAST_ROWS)
    out = pl.pallas_call(
        _cast_body,
        grid=(rows // tr,),
        in_specs=[pl.BlockSpec((tr, d), lambda i: (i, 0))],
        out_specs=pl.BlockSpec((tr, d), lambda i: (i, 0)),
        out_shape=jax.ShapeDtypeStruct((rows, d), BF16),
        compiler_params=_params(("arbitrary",)),
        name="cast_bf16",
    )(w3.reshape(rows, d))
    return out.reshape(n, k, d)


def _outproj_body(y_ref, ys_ref, w_ref, x_ref, xs_ref, g_ref, *out_refs, cast_w, want_x):
    if cast_w:
        *out_refs, wb_ref = out_refs

        @pl.when(pl.program_id(0) == 0)
        def _():
            wb_ref[...] = w_ref[...].astype(BF16)
    else:
        wb_ref = w_ref
    if want_x:
        xo_ref, xso_ref, ho_ref, hso_ref = out_refs
    else:
        (ho_ref, hso_ref), xo_ref, xso_ref = out_refs, None, None

    def rows(y_in, x_in, x_out, h_out):
        xn = x_in[...] + _nn(y_in[...], wb_ref[...])
        if x_out is not None:
            x_out[...] = xn
        ms = jnp.mean(xn * xn, axis=-1, keepdims=True)
        h_out[...] = (xn * lax.rsqrt(ms + NORM_EPS) * g_ref[...]).astype(h_out.dtype)

    pl.when(pl.program_id(0) == 0)(lambda: rows(ys_ref, xs_ref, xso_ref, hso_ref))
    rows(y_ref, x_ref, xo_ref, ho_ref)


def _outproj(y, ys, w3, slot, x, xs, g, h_dtype, tm, want_x):
    m, kd = y.shape
    ms, d = xs.shape
    cast_w = w3.dtype == F32
    const = lambda i: (0, 0)
    row_spec = pl.BlockSpec((tm, d), lambda i: (i, 0))
    s_spec = pl.BlockSpec((ms, d), const)
    shape = lambda rows, dt: jax.ShapeDtypeStruct((rows, d), dt)
    outs = pl.pallas_call(
        functools.partial(_outproj_body, cast_w=cast_w, want_x=want_x),
        scratch_shapes=[pltpu.VMEM((kd, d), BF16)] if cast_w else [],
        grid=(m // tm,),
        in_specs=[pl.BlockSpec((tm, kd), lambda i: (i, 0)),
                  pl.BlockSpec((ms, kd), const),
                  pl.BlockSpec((None, kd, d), lambda i: (slot, 0, 0), pipeline_mode=pl.Buffered(1)),
                  row_spec, s_spec,
                  pl.BlockSpec((1, d), const)],
        out_specs=([row_spec, s_spec] if want_x else []) + [row_spec, s_spec],
        out_shape=([shape(m, F32), shape(ms, F32)] if want_x else []) + [shape(m, h_dtype), shape(ms, h_dtype)],
        compiler_params=_params(("arbitrary",)),
        name="outproj",
    )(y, ys, w3, x, xs, g.reshape(1, d))
    return tuple(outs) if want_x else (None, None, *outs)


LOG2_E = 1.4426950408889634


def _rows(start, size, stride):
    return pl.ds(start, size) if stride == 1 else pl.ds(start, size, stride=stride)


def _attn_prompt_body(*refs, seq):
    qkv_refs, gate_ref, bias_first_ref, bias_band_ref, y_ref = refs[:9], *refs[9:13]
    o_s, d_s, m_s = refs[13:]
    q_scale = (A_HEAD_DIM ** -0.5) * LOG2_E
    ones = jnp.ones((A_BAND, A_HEAD_DIM), BF16)
    for g, (_, dil) in enumerate(A_GROUPS):
        q_ref, k_ref, v_ref = qkv_refs[3 * g:3 * g + 3]

        def load_kv(rows, k_ref=k_ref, v_ref=v_ref):
            return k_ref[rows, :].astype(BF16), jnp.concatenate([v_ref[rows, :].astype(BF16), ones], axis=1)

        def chain(r, n0, n1, g=g, dil=dil, q_ref=q_ref, load_kv=load_kv):
            block_rows = lambda n: _rows(r + dil * A_BAND * n, A_BAND, dil)
            k_prev, v_prev = load_kv(block_rows(n0 - 1)) if n0 > 0 else (None, None)
            for n in range(n0, n1):
                rows = block_rows(n)
                q = (q_ref[rows, :] * q_scale).astype(BF16)
                k_cur, v_cur = load_kv(rows)
                if n == 0:
                    keys, vals, bias_ref = k_cur, v_cur, bias_first_ref
                else:
                    keys = jnp.concatenate([k_prev, k_cur], axis=0)
                    vals = jnp.concatenate([v_prev, v_cur], axis=0)
                    bias_ref = bias_band_ref
                k_prev, v_prev = k_cur, v_cur
                s = _nt(q, keys)
                yield
                s = s + bias_ref[...]
                mx = jnp.max(s, axis=-1, keepdims=True)
                od = _nn(jnp.exp2(s - mx).astype(BF16), vals)
                yield
                o_s[g, rows, :] = od[:, :A_HEAD_DIM]
                d_s[g, rows, :] = od[:, A_HEAD_DIM:]
                m_s[g, rows, :] = jnp.broadcast_to(mx, (A_BAND, A_HEAD_DIM))

        nb = seq // dil // A_BAND
        if dil == 1:
            chains = [chain(0, 0, nb // 2), chain(0, nb // 2, nb)]
        else:
            chains = [chain(r, 0, nb) for r in range(dil)]
        _staggered(chains)
    for tile in range(seq // A_BAND):
        rs = slice(tile * A_BAND, (tile + 1) * A_BAND)
        m1, m2, m3 = m_s[0, rs, :], m_s[1, rs, :], m_s[2, rs, :]
        mx = jnp.maximum(jnp.maximum(m1, m2), m3)
        f1, f2, f3 = jnp.exp2(m1 - mx), jnp.exp2(m2 - mx), jnp.exp2(m3 - mx)
        num = f1 * o_s[0, rs, :] + f2 * o_s[1, rs, :] + f3 * o_s[2, rs, :]
        den = f1 * d_s[0, rs, :] + f2 * d_s[1, rs, :] + f3 * d_s[2, rs, :]
        y_ref[rs, :] = (num / den * _silu(gate_ref[rs, :])).astype(y_ref.dtype)


def _attn_prompt(z, batch, seq):
    n_groups = len(A_GROUPS)
    in_specs = []
    for c in range(3 * n_groups + 1):
        in_specs.append(pl.BlockSpec((seq, A_HEAD_DIM), lambda b, h, c=c: (b, c * A_HEADS + h)))
    row = jnp.arange(A_BAND)[:, None]
    col = jnp.arange(2 * A_BAND)[None, :]
    bias_first = jnp.where(col[:, :A_BAND] <= row, 0.0, -jnp.inf).astype(F32)
    bias_band = jnp.where((col >= row) & (col <= row + A_BAND), 0.0, -jnp.inf).astype(F32)
    in_specs.append(pl.BlockSpec((A_BAND, A_BAND), lambda b, h: (0, 0)))
    in_specs.append(pl.BlockSpec((A_BAND, 2 * A_BAND), lambda b, h: (0, 0)))
    return pl.pallas_call(
        functools.partial(_attn_prompt_body, seq=seq),
        grid=(batch, A_HEADS),
        in_specs=in_specs,
        out_specs=pl.BlockSpec((seq, A_HEAD_DIM), lambda b, h: (b, h)),
        out_shape=jax.ShapeDtypeStruct((batch * seq, A_HEADS * A_HEAD_DIM), BF16),
        scratch_shapes=[pltpu.VMEM((n_groups, seq, A_HEAD_DIM), F32)] * 3,
        compiler_params=_params(("arbitrary", "arbitrary")),
        name="attn_prompt",
    )(*([z] * (3 * n_groups + 1)), bias_first, bias_band)


def _attn_sample_body(z_ref, c1_ref, c2_ref, c3_ref, y_ref):
    t = pl.program_id(1)
    scale = A_HEAD_DIM ** -0.5
    m_idx = lax.broadcasted_iota(jnp.int32, (A_BAND, A_HEADS, 1), 0)
    n_idx = lax.broadcasted_iota(jnp.int32, (SAMPLE_PAD, A_HEADS, 1), 0)
    caches = (c1_ref, c2_ref, c3_ref)
    outs, lses = [], []
    for g, (_, dil) in enumerate(A_GROUPS):
        base = 3 * g * A_HEADS
        q = z_ref[0, t, base:base + A_HEADS, :]
        kc = caches[g][0, :, 0:A_HEADS, :]
        vc = caches[g][0, :, A_HEADS:2 * A_HEADS, :]
        kn = z_ref[0, :, base + A_HEADS:base + 2 * A_HEADS, :]
        vn = z_ref[0, :, base + 2 * A_HEADS:base + 3 * A_HEADS, :]
        s_c = jnp.sum(kc * q[None], axis=-1, keepdims=True) * scale
        s_n = jnp.sum(kn * q[None], axis=-1, keepdims=True) * scale
        if dil == 1:
            s_c = jnp.where(m_idx >= t, s_c, -jnp.inf)
            s_n = jnp.where(n_idx <= t, s_n, -jnp.inf)
        else:
            s_n = jnp.where(n_idx == t, s_n, -jnp.inf)
        mx = jnp.maximum(jnp.max(s_c, axis=0), jnp.max(s_n, axis=0))
        p_c = jnp.exp(s_c - mx[None])
        p_n = jnp.exp(s_n - mx[None])
        den = jnp.sum(p_c, axis=0) + jnp.sum(p_n, axis=0)
        num = jnp.sum(p_c * vc, axis=0) + jnp.sum(p_n * vn, axis=0)
        outs.append(num / den)
        lses.append(mx + jnp.log(den))
    lmax = jnp.maximum(jnp.maximum(lses[0], lses[1]), lses[2])
    es = [jnp.exp(l - lmax) for l in lses]
    tot = es[0] + es[1] + es[2]
    o = (es[0] / tot) * outs[0] + (es[1] / tot) * outs[1] + (es[2] / tot) * outs[2]
    gate = z_ref[0, t, 3 * len(A_GROUPS) * A_HEADS:(3 * len(A_GROUPS) + 1) * A_HEADS, :]
    y_ref[0, 0] = o * _silu(gate)


def _attn_sample(zs, caches, slot, batch, t_real):
    n_in = zs.shape[1]
    z5 = zs.reshape(batch, SAMPLE_PAD, n_in // A_HEAD_DIM, A_HEAD_DIM)
    in_specs = [pl.BlockSpec((1, SAMPLE_PAD, n_in // A_HEAD_DIM, A_HEAD_DIM), lambda b, t: (b, 0, 0, 0))]
    args = [z5]
    for g, (win, dil) in enumerate(A_GROUPS):
        n_a = caches[g].shape[0]
        cv = caches[g].reshape(n_a * batch, win // dil, dil * 2 * A_HEADS, A_HEAD_DIM)
        if dil == 1:
            idx = lambda b, t: (slot * batch + b, 0, 0, 0)
        else:
            idx = lambda b, t: (slot * batch + b, 0, t, 0)
        in_specs.append(pl.BlockSpec((1, A_BAND, 2 * A_HEADS, A_HEAD_DIM), idx))
        args.append(cv)
    y = pl.pallas_call(
        _attn_sample_body,
        grid=(batch, t_real),
        in_specs=in_specs,
        out_specs=pl.BlockSpec((1, 1, A_HEADS, A_HEAD_DIM), lambda b, t: (b, t, 0, 0)),
        out_shape=jax.ShapeDtypeStruct((batch, t_real, A_HEADS, A_HEAD_DIM), F32),
        compiler_params=_params(("arbitrary", "arbitrary")),
        name="attn_sample",
    )(*args)
    return y.reshape(batch, t_real, A_HEADS * A_HEAD_DIM)


def _kv_rows_body(*refs, n_layers):
    out_ref = refs[-1]
    layer = pl.program_id(0)
    for l in range(n_layers):
        k_ref, v_ref = refs[2 * l], refs[2 * l + 1]

        @pl.when(layer == l)
        def _():
            sub = SUBLANES
            for base, src in ((0, k_ref), (A_HEADS, v_ref)):
                for c0 in range(0, A_HEADS, sub):
                    cols = [src[:, (c0 + c) * A_HEAD_DIM:(c0 + c + 1) * A_HEAD_DIM] for c in range(sub)]
                    out_ref[:, base + c0:base + c0 + sub, :] = jnp.swapaxes(jnp.stack(cols, axis=0), 0, 1)


def _kv_rows(zs, g, batch, seq, keep):
    n_layers = len(zs)
    width = A_HEADS * A_HEAD_DIM
    tt = min(keep, KV_ROWS)
    first = (seq - keep) // tt
    in_specs, args = [], []
    for l in range(n_layers):
        for c in (1, 2):
            def idx(s, b, i, l=l, c=c):
                return (jnp.where(s == l, b * (seq // tt) + first + i, 0), 3 * g + c)
            in_specs.append(pl.BlockSpec((tt, width), idx))
            args.append(zs[l])
    out = pl.pallas_call(
        functools.partial(_kv_rows_body, n_layers=n_layers),
        grid=(n_layers, batch, keep // tt),
        in_specs=in_specs,
        out_specs=pl.BlockSpec((None, None, tt, 2 * A_HEADS, A_HEAD_DIM), lambda s, b, i: (s, b, i, 0, 0)),
        out_shape=jax.ShapeDtypeStruct((n_layers, batch, keep, 2 * A_HEADS, A_HEAD_DIM), F32),
        compiler_params=_params(("arbitrary", "arbitrary", "arbitrary")),
        name="kv_rows_g%d" % g,
    )(*args)
    return out.reshape(n_layers, batch, keep, 2, A_HEADS, A_HEAD_DIM)


def _tn(a, b):
    return lax.dot_general(a, b, (((0,), (0,)), ((), ())), preferred_element_type=F32)


def _retention_body(*refs, has_state, chunk, inner):
    if has_state:
        q_ref, k_ref, v_ref, gate_ref, dec_ref, xi_ref, kd_ref, sd_ref, gn_ref, s0_ref, y_ref, st_ref = refs
    else:
        q_ref, k_ref, v_ref, gate_ref, dec_ref, xi_ref, kd_ref, sd_ref, gn_ref, y_ref, st_ref = refs
    c = pl.program_id(1)

    @pl.when(c == 0)
    def _():
        if has_state:
            st_ref[...] = s0_ref[...]
        else:
            st_ref[...] = jnp.zeros_like(st_ref)

    def head(h, rs):
        ks = slice(h * B_DK, (h + 1) * B_DK)
        vs = slice(h * B_DV, (h + 1) * B_DV)
        qb = q_ref[rs, ks].astype(BF16)
        kf = k_ref[rs, ks]
        vb = v_ref[rs, vs].astype(BF16)
        state = st_ref[0, h]
        scores = _nt(qb, kf.astype(BF16))
        cross = _nn(qb, state.astype(BF16))
        k_dec = (kf * kd_ref[h]).astype(BF16)
        st_ref[0, h] = sd_ref[h] * state + _tn(k_dec, vb)
        yield
        o = _nn((scores * dec_ref[h]).astype(BF16), vb) + cross * xi_ref[h]
        yield
        mu = jnp.mean(o, axis=-1, keepdims=True)
        var = jnp.mean(jnp.square(o - mu), axis=-1, keepdims=True)
        yv = (o - mu) * lax.rsqrt(var + NORM_EPS) * gn_ref[h]
        y_ref[rs, vs] = (yv * _silu(gate_ref[rs, vs])).astype(y_ref.dtype)

    for cc in range(inner):
        _round_robin([head(h, slice(cc * chunk, (cc + 1) * chunk)) for h in range(B_HEADS)])


def _retention(z, batch, chunk, t_real, gn, state0):
    rows = z.shape[0]
    nc = rows // (batch * chunk)
    lg = jnp.log(1.0 - 2.0 ** (-5.0 - jnp.arange(B_HEADS, dtype=F32)))[:, None, None]
    idx = jnp.arange(chunk, dtype=F32)
    diff = idx[:, None] - idx[None, :]
    dec = jnp.where(diff >= 0, jnp.exp(lg * jnp.maximum(diff, 0.0)), 0.0)
    xi = jnp.exp(lg * (idx[None, :, None] + 1.0))
    live = idx[None, :, None] < t_real
    kd = jnp.where(live, jnp.exp(lg * jnp.where(live, t_real - 1.0 - idx[None, :, None], 0.0)), 0.0)
    sd = jnp.exp(lg * t_real)
    n_k, n_v = B_HEADS * B_DK, B_HEADS * B_DV
    const = lambda b, c: (0, 0, 0)
    inner = B_CHUNKS_PER_STEP if nc % B_CHUNKS_PER_STEP == 0 else 1
    ns, step_rows = nc // inner, inner * chunk
    in_specs = [pl.BlockSpec((step_rows, n_k), lambda b, c: (b * ns + c, 0)),
                pl.BlockSpec((step_rows, n_k), lambda b, c: (b * ns + c, 1)),
                pl.BlockSpec((step_rows, n_v), lambda b, c: (b * ns + c, 2 * n_k // n_v)),
                pl.BlockSpec((step_rows, n_v), lambda b, c: (b * ns + c, 2 * n_k // n_v + 1)),
                pl.BlockSpec((B_HEADS, chunk, chunk), const),
                pl.BlockSpec((B_HEADS, chunk, 1), const),
                pl.BlockSpec((B_HEADS, chunk, 1), const),
                pl.BlockSpec((B_HEADS, 1, 1), const),
                pl.BlockSpec((B_HEADS, 1, B_DV), const)]
    args = [z, z, z, z, dec, xi, kd, sd, gn.reshape(B_HEADS, 1, B_DV)]
    state_spec = pl.BlockSpec((1, B_HEADS, B_DK, B_DV), lambda b, c: (b, 0, 0, 0))
    if state0 is not None:
        in_specs.append(state_spec)
        args.append(state0)
    return pl.pallas_call(
        functools.partial(_retention_body, has_state=state0 is not None, chunk=chunk, inner=inner),
        grid=(batch, ns),
        in_specs=in_specs,
        out_specs=[pl.BlockSpec((step_rows, n_v), lambda b, c: (b * ns + c, 0)), state_spec],
        out_shape=[jax.ShapeDtypeStruct((rows, n_v), BF16),
                   jax.ShapeDtypeStruct((batch, B_HEADS, B_DK, B_DV), F32)],
        compiler_params=_params(("arbitrary", "arbitrary")),
        name="retention",
    )(*args)


def _log_sigmoid(x):
    return jnp.minimum(x, 0.0) - jnp.log(1.0 + jnp.exp(-jnp.abs(x)))


def _gla_head(q, k, v, lr, wg, bg, state, consts, chunk, sub, t_sub):
    nsub = chunk // sub
    tri, pair_live, pair_col, pick_t = consts
    pre = _nn_hi(lr, wg)
    yield
    log_a = _log_sigmoid(pre + bg) * (LOG2_E / C_TAU)
    bl = _nn_exact_lhs(tri, log_a)
    yield

    def block_rows(vals):
        return jnp.concatenate([jnp.broadcast_to(x, (sub, C_DK)) for x in vals], axis=0)

    tot = [bl[j * sub + t_sub - 1:j * sub + t_sub, :] for j in range(nsub)]
    beta = [jnp.zeros((1, C_DK), F32)]
    for j in range(nsub):
        beta.append(beta[-1] + tot[j])
    qs = q * (C_DK ** -0.5)
    q_in = qs * jnp.exp2(bl)
    k_out = k * jnp.exp2(block_rows(tot) - bl)
    if t_sub < sub:
        k_out = jnp.where(lax.broadcasted_iota(jnp.int32, (chunk, 1), 0) % sub < t_sub, k_out, 0.0)
    vb = v.astype(BF16)

    o = _nn((q_in * block_rows([jnp.exp2(x) for x in beta[:nsub]])).astype(BF16), state.astype(BF16))
    yield

    def rep_rows(x):
        return jnp.concatenate([jnp.broadcast_to(x[t:t + 1, :], (sub, C_DK)) for t in range(sub)], axis=0)

    def tile_rows(x):
        return jnp.concatenate([x] * sub, axis=0)

    a_rows = []
    for i in range(nsub):
        r = slice(i * sub, (i + 1) * sub)
        pair = rep_rows(qs[r]) * tile_rows(k[r]) * jnp.exp2(rep_rows(bl[r]) - tile_rows(bl[r]))
        att = jnp.where(pair_live, jnp.sum(pair, axis=1, keepdims=True), 0.0)
        placed = jnp.where(pair_col == i * sub, att, 0.0).astype(BF16)
        a_i = _nn(pick_t, placed)
        if i > 0:
            between = [jnp.broadcast_to(jnp.exp2(beta[i] - beta[j + 1]), (sub, C_DK)) for j in range(i)]
            between.append(jnp.zeros(((nsub - i) * sub, C_DK), F32))
            a_i += _nt(q_in[r].astype(BF16), (k_out * jnp.concatenate(between, axis=0)).astype(BF16))
        a_rows.append(a_i)
        if i % 2 == 1:
            yield
    a = a_rows[0] if nsub == 1 else jnp.concatenate(a_rows, axis=0)
    o += _nn(a.astype(BF16), vb)
    yield
    k_end = k_out * block_rows([jnp.exp2(beta[nsub] - beta[j + 1]) for j in range(nsub)])
    d_col = jnp.broadcast_to(jnp.exp2(beta[nsub]), (SUBLANES, C_DK)).T[:, 0:1]
    return o, d_col * state + _tn(k_end.astype(BF16), vb)


def _gla_consts(chunk, sub):
    ri = lax.broadcasted_iota(jnp.int32, (chunk, chunk), 0)
    ci = lax.broadcasted_iota(jnp.int32, (chunk, chunk), 1)
    tri = jnp.logical_and(ri // sub == ci // sub, ci <= ri).astype(BF16)
    pj = lax.broadcasted_iota(jnp.int32, (sub * sub, 1), 0)
    pair_live = pj // sub >= pj % sub
    pc = lax.broadcasted_iota(jnp.int32, (sub * sub, chunk), 1)
    ps = lax.broadcasted_iota(jnp.int32, (sub * sub, chunk), 0) % sub
    et = lax.broadcasted_iota(jnp.int32, (sub, sub * sub), 0)
    ej = lax.broadcasted_iota(jnp.int32, (sub, sub * sub), 1)
    pick_t = (ej // sub == et).astype(BF16)
    return tri, pair_live, pc - ps, pick_t


def _gla_body(*refs, has_state, chunk, inner, sub, t_sub):
    consts = _gla_consts(chunk, sub)
    if has_state:
        q_ref, k_ref, v_ref, gate_ref, lr_ref, wg_ref, bg_ref, gn_ref, s0_ref, y_ref, st_ref = refs
    else:
        q_ref, k_ref, v_ref, gate_ref, lr_ref, wg_ref, bg_ref, gn_ref, y_ref, st_ref = refs
    c = pl.program_id(1)

    @pl.when(c == 0)
    def _():
        if has_state:
            st_ref[...] = s0_ref[...]
        else:
            st_ref[...] = jnp.zeros_like(st_ref)

    for cc in range(inner):
        rs = slice(cc * chunk, (cc + 1) * chunk)
        lr = lr_ref[rs, 0:C_RANK]
        heads = []
        for h in range(C_HEADS):
            ks = slice(h * C_DK, (h + 1) * C_DK)
            vs = slice(h * C_DV, (h + 1) * C_DV)
            heads.append(_gla_head(q_ref[rs, ks], k_ref[rs, ks], v_ref[rs, vs], lr, wg_ref[:, ks], bg_ref[:, ks],
                                   st_ref[0, h], consts, chunk, sub, t_sub))
        for h, (o, new_state) in enumerate(_round_robin(heads)):
            vs = slice(h * C_DV, (h + 1) * C_DV)
            st_ref[0, h] = new_state
            yv = o * lax.rsqrt(jnp.mean(o * o, axis=-1, keepdims=True) + NORM_EPS) * gn_ref[:, vs]
            y_ref[rs, vs] = (yv * _silu(gate_ref[rs, vs])).astype(y_ref.dtype)


def _gla(z, z_lr, batch, chunk, t_real, wg3, bg3, gn3, slot, state0):
    rows = z.shape[0]
    nc = rows // (batch * chunk)
    sub, t_sub = (C_SUB, C_SUB) if t_real == chunk else (chunk, t_real)
    n_k, n_v = C_HEADS * C_DK, C_HEADS * C_DV
    n_slots = wg3.shape[0]
    inner = C_CHUNKS_PER_STEP if nc % C_CHUNKS_PER_STEP == 0 else 1
    ns, step_rows = nc // inner, inner * chunk
    in_specs = [pl.BlockSpec((step_rows, n_k), lambda b, c: (b * ns + c, 0)),
                pl.BlockSpec((step_rows, n_k), lambda b, c: (b * ns + c, 1)),
                pl.BlockSpec((step_rows, n_v), lambda b, c: (b * ns + c, 2 * n_k // n_v)),
                pl.BlockSpec((step_rows, n_v), lambda b, c: (b * ns + c, 2 * n_k // n_v + 1)),
                pl.BlockSpec((step_rows, LANES), lambda b, c: (b * ns + c, 0)),
                pl.BlockSpec((None, C_RANK, n_k), lambda b, c: (slot, 0, 0)),
                pl.BlockSpec((None, 1, n_k), lambda b, c: (slot, 0, 0)),
                pl.BlockSpec((None, 1, n_v), lambda b, c: (slot, 0, 0))]
    args = [z, z, z, z, z_lr, wg3, bg3.reshape(n_slots, 1, n_k), gn3.reshape(n_slots, 1, n_v)]
    state_spec = pl.BlockSpec((1, C_HEADS, C_DK, C_DV), lambda b, c: (b, 0, 0, 0))
    if state0 is not None:
        in_specs.append(state_spec)
        args.append(state0)
    return pl.pallas_call(
        functools.partial(_gla_body, has_state=state0 is not None, chunk=chunk, inner=inner, sub=sub, t_sub=t_sub),
        grid=(batch, ns),
        in_specs=in_specs,
        out_specs=[pl.BlockSpec((step_rows, n_v), lambda b, c: (b * ns + c, 0)), state_spec],
        out_shape=[jax.ShapeDtypeStruct((rows, n_v), BF16),
                   jax.ShapeDtypeStruct((batch, C_HEADS, C_DK, C_DV), F32)],
        compiler_params=_params(("arbitrary", "arbitrary")),
        name="gla",
    )(*args)


def _rot_tables_a(pos):
    half = A_ROT // 2
    inv_freq = A_ROPE_THETA ** (-jnp.arange(half, dtype=F32) / half)
    ang = pos.astype(F32)[:, None] * inv_freq[None, :]
    cos, sin = jnp.cos(ang), jnp.sin(ang)
    n = pos.shape[0]
    rest = A_HEAD_DIM - A_ROT
    c = jnp.concatenate([cos, cos, jnp.ones((n, rest), F32)], axis=1)
    s = jnp.concatenate([-sin, sin, jnp.zeros((n, rest), F32)], axis=1)
    return c, s


def _rot_tables_b(pos):
    half = B_DK // 2
    inv_freq = B_ROPE_THETA ** (-jnp.arange(half, dtype=F32) / half)
    ang = pos.astype(F32)[:, None] * inv_freq[None, :]
    return jnp.cos(ang), jnp.sin(ang)


def kernel(x_prompt, x_sample, cache_a_kv1, cache_a_kv2, cache_a_kv3, state_b, state_c, norm_g, final_g,
           w_in_a, w_out_a, w_in_b, gn_b, w_out_b, w_in_c, w_gate2_c, b_gate_c, gn_c, w_out_c):
    bp, seq, d = x_prompt.shape
    bs, t_real, _ = x_sample.shape
    caches_a = (cache_a_kv1, cache_a_kv2, cache_a_kv3)
    width = A_HEADS * A_HEAD_DIM

    xp = x_prompt.reshape(bp * seq, d)
    xs = jnp.pad(x_sample, ((0, 0), (0, SAMPLE_PAD - t_real), (0, 0))).reshape(bs * SAMPLE_PAD, d)
    ms = xs.shape[0]
    tm_p = min(PROJ_TM, seq)

    pos_p = jnp.arange(seq)
    pos_s = jnp.tile(PAST_LEN + jnp.arange(SAMPLE_PAD), bs)
    tabs_a_p, tabs_a_s = _rot_tables_a(pos_p), _rot_tables_a(pos_s)
    tabs_b_p, tabs_b_s = _rot_tables_b(pos_p), _rot_tables_b(pos_s)

    hp = _rmsnorm(xp, norm_g[0], BF16)
    hs = _rmsnorm(xs, norm_g[0], BF16)
    def maybe_cast(w3):
        return _cast_bf16(w3) if w3.shape[1] * w3.shape[2] * 6 > OUTPROJ_CAST_LIMIT else w3

    w_out_a, w_out_b, w_out_c = maybe_cast(w_out_a), maybe_cast(w_out_b), maybe_cast(w_out_c)

    a_z_p, a_rows_s = [], []
    b_states_p, b_states_s, c_states_p, c_states_s = [], [], [], []
    n_layers = len(LAYER_KINDS)
    for i in range(n_layers):
        kind, slot = LAYER_KINDS[i], LAYER_SLOTS[i]
        last = i == n_layers - 1
        g_next = final_g if last else norm_g[i + 1]
        h_dtype = F32 if last else BF16
        if kind == 0:
            n_in = w_in_a.shape[2]
            zp, zs = _proj(hp, hs, w_in_a, slot, n_in, "a", tabs_a_p, tabs_a_s, seq // tm_p, tm_p, PROJ_TN)
            yp = _attn_prompt(zp, bp, seq)
            ys = _attn_sample(zs, caches_a, slot, bs, t_real)
            ys = jnp.pad(ys, ((0, 0), (0, SAMPLE_PAD - t_real), (0, 0))).reshape(ms, width).astype(BF16)
            zs3 = zs.reshape(bs, SAMPLE_PAD, n_in)
            rows_s = []
            for g in range(len(A_GROUPS)):
                lo = (3 * g + 1) * width
                rows_s.append(zs3[:, :t_real, lo:lo + 2 * width].reshape(bs, t_real, 2, A_HEADS, A_HEAD_DIM))
            a_z_p.append(zp)
            a_rows_s.append(rows_s)
            w_out = w_out_a
        elif kind == 1:
            n_in = w_in_b.shape[2]
            zp, zs = _proj(hp, hs, w_in_b, slot, n_in, "b", tabs_b_p, tabs_b_s, seq // tm_p, tm_p, PROJ_TN)
            yp, st_p = _retention(zp, bp, min(B_CHUNK, seq), min(B_CHUNK, seq), gn_b[slot], None)
            ys, st_s = _retention(zs, bs, SAMPLE_PAD, t_real, gn_b[slot], state_b[slot])
            b_states_p.append(st_p)
            b_states_s.append(st_s)
            w_out = w_out_b
        else:
            n_main = 2 * C_HEADS * C_DK + 2 * C_HEADS * C_DV
            w_lr = jnp.pad(w_in_c[slot][:, n_main:], ((0, 0), (0, LANES - C_RANK)))
            zp, zs, zp_lr, zs_lr = _proj(hp, hs, w_in_c, slot, n_main, "c", (), (), 1, tm_p, PROJ_TN, w_side=w_lr)
            yp, st_p = _gla(zp, zp_lr, bp, min(C_CHUNK, seq), min(C_CHUNK, seq),
                            w_gate2_c, b_gate_c, gn_c, slot, None)
            ys, st_s = _gla(zs, zs_lr, bs, SAMPLE_PAD, t_real, w_gate2_c, b_gate_c, gn_c, slot, state_c[slot])
            c_states_p.append(st_p)
            c_states_s.append(st_s)
            w_out = w_out_c
        xp, xs, hp, hs = _outproj(yp, ys, w_out, slot, xp, xs, g_next, h_dtype, OUTPROJ_TM, not last)

    y_prompt = hp.reshape(bp, seq, d)
    y_sample = hs.reshape(bs, SAMPLE_PAD, d)[:, :t_real]
    stack = lambda rows, g: jnp.stack([r[g] for r in rows])
    kv_p = [_kv_rows(a_z_p, g, bp, seq, min(win, seq)) for g, (win, _) in enumerate(A_GROUPS)]
    return (y_prompt, y_sample,
            kv_p[0], kv_p[1], kv_p[2],
            jnp.stack(b_states_p), jnp.stack(c_states_p),
            stack(a_rows_s, 0), stack(a_rows_s, 1), stack(a_rows_s, 2),
            jnp.stack(b_states_s), jnp.stack(c_states_s))
```
